```python
import math
import jax, jax.numpy as jnp
from jax import lax
import numpy as np

D_MODEL = 1024
BATCH = 2
SEQ = 8192
DEPTH = 2

GRID_W = 64
D_MIX = D_MODEL
EPS = 1e-6
CONV_W = 256
CONV_K = 3
NA_HEADS = 4
NA_DH = 64
NA_W = NA_HEADS * NA_DH
NA_KH = 8
NA_KW = 16
GDN_HEADS = 4
GDN_DK = 128
GDN_DV = 128
GDN_W = GDN_HEADS * GDN_DV
GDN_CONV_K = 3
GDN_CHUNK = 64
GDN_QKV = 2 * GDN_HEADS * GDN_DK + GDN_W
IN_SPLITS = (CONV_W, CONV_W, CONV_W, NA_W, NA_W, NA_W, GDN_QKV, GDN_W, 2 * GDN_HEADS, 2 * GDN_HEADS)
D_IN = 3 * CONV_W + 3 * NA_W + GDN_QKV + GDN_W + 4 * GDN_HEADS
N_GROUPS = 4
EXPERTS_PER_GROUP = 8
N_EXPERTS = N_GROUPS * EXPERTS_PER_GROUP
TOP_K = 2
D_EXPERT = 512
MOE_BLOCK = 256

kernel_name = "hybrid_parallel_conv_na_gdn_hmoe_encoder"


def _rmsnorm(x, w):
    xf = x.astype(jnp.float32)
    y = xf * lax.rsqrt(jnp.mean(xf * xf, axis=-1, keepdims=True) + EPS)
    return (y * w.astype(jnp.float32)).astype(x.dtype)


def _l2norm(x):
    xf = x.astype(jnp.float32)
    return xf * lax.rsqrt(jnp.sum(xf * xf, axis=-1, keepdims=True) + EPS)


def _dwconv(x, w):
    ch = x.shape[-1]
    return lax.conv_general_dilated(x, w[:, None, :].astype(x.dtype), window_strides=(1,), padding='SAME',
                                    dimension_numbers=('NWC', 'WIO', 'NWC'), feature_group_count=ch)


def _neighbourhood_attention(q, k, v, rpb):
    bsz, t, h, dh = q.shape
    rows = t // GRID_W
    kh = min(NA_KH, rows)
    qg = q.reshape(bsz, rows, GRID_W, h, dh)
    kg = k.reshape(bsz, rows, GRID_W, h, dh)
    vg = v.reshape(bsz, rows, GRID_W, h, dh)
    col = jnp.arange(GRID_W)
    col_idx = jnp.clip(col - NA_KW // 2, 0, GRID_W - NA_KW)[:, None] + jnp.arange(NA_KW)[None, :]
    dc = col_idx - col[:, None] + (NA_KW - 1)
    row_start = jnp.clip(jnp.arange(rows) - kh // 2, 0, rows - kh)
    scale = dh ** -0.5

    def one_row(r):
        rs = row_start[r]
        q_r = lax.dynamic_index_in_dim(qg, r, axis=1, keepdims=False)
        k_rows = lax.dynamic_slice_in_dim(kg, rs, kh, axis=1)
        v_rows = lax.dynamic_slice_in_dim(vg, rs, kh, axis=1)
        k_win = k_rows[:, :, col_idx]
        v_win = v_rows[:, :, col_idx]
        dr = rs + jnp.arange(kh) - r + (NA_KH - 1)
        bias = rpb[:, dr][:, :, dc]
        s = jnp.einsum('bchd,brckhd->bhcrk', q_r, k_win).astype(jnp.float32) * scale
        s = s + jnp.transpose(bias, (0, 2, 1, 3)).astype(jnp.float32)[None]
        p = jax.nn.softmax(s.reshape(bsz, h, GRID_W, kh * NA_KW), axis=-1).reshape(s.shape).astype(v.dtype)
        return jnp.einsum('bhcrk,brckhd->bchd', p, v_win)

    out = lax.map(one_row, jnp.arange(rows))
    return jnp.transpose(out, (1, 0, 2, 3, 4)).reshape(bsz, t, h * dh)


def _gated_delta_chunked(q, k, v, g, beta):
    bsz, h, t, dk = q.shape
    dv = v.shape[-1]
    n = t // GDN_CHUNK
    cs = lambda a: a.reshape((bsz, h, n, GDN_CHUNK) + a.shape[3:])
    q, k, v, g, beta = cs(q), cs(k), cs(v), cs(g), cs(beta)
    g = jnp.cumsum(g, axis=-1)
    idx = jnp.arange(GDN_CHUNK)
    strict = idx[:, None] > idx[None, :]
    incl = idx[:, None] >= idx[None, :]
    diff = g[..., :, None] - g[..., None, :]
    dec_strict = jnp.exp(jnp.where(strict, diff, -jnp.inf))
    dec_incl = jnp.exp(jnp.where(incl, diff, -jnp.inf))
    kb = k * beta[..., None]
    lower = jnp.einsum('bhncd,bhnsd->bhncs', kb, k) * dec_strict
    rhs = jnp.concatenate([v * beta[..., None], kb * jnp.exp(g)[..., None]], axis=-1)
    sol = lax.linalg.triangular_solve(lower, rhs, left_side=True, lower=True, unit_diagonal=True)
    u, w = sol[..., :dv], sol[..., dv:]
    intra = jnp.einsum('bhncd,bhnsd->bhncs', q, k) * dec_incl
    q_dec = q * jnp.exp(g)[..., None]
    g_last = g[..., -1]
    k_dec = k * jnp.exp(g_last[..., None] - g)[..., None]

    def step(state, xs):
        u_c, w_c, intra_c, q_c, k_c, gl_c = xs
        v_new = u_c - jnp.einsum('bhcd,bhde->bhce', w_c, state)
        o_c = jnp.einsum('bhcd,bhde->bhce', q_c, state) + jnp.einsum('bhcs,bhse->bhce', intra_c, v_new)
        state = state * jnp.exp(gl_c)[..., None, None] + jnp.einsum('bhcd,bhce->bhde', k_c, v_new)
        return state, o_c

    xs = tuple(jnp.moveaxis(a, 2, 0) for a in (u, w, intra, q_dec, k_dec, g_last))
    s0 = jnp.zeros((bsz, h, dk, dv), jnp.float32)
    _, o = lax.scan(step, s0, xs)
    return jnp.moveaxis(o, 0, 2).reshape(bsz, h, t, dv)


def _gdn_mixer(qkv, z, a, b, conv_w, a_log, dt_bias, norm_w):
    bsz, t, _ = qkv.shape
    qkv = jax.nn.silu(_dwconv(qkv, conv_w))
    q = qkv[..., :GDN_HEADS * GDN_DK].reshape(bsz, t, GDN_HEADS, GDN_DK)
    k = qkv[..., GDN_HEADS * GDN_DK:2 * GDN_HEADS * GDN_DK].reshape(bsz, t, GDN_HEADS, GDN_DK)
    v = qkv[..., 2 * GDN_HEADS * GDN_DK:].reshape(bsz, t, GDN_HEADS, GDN_DV)
    q = jnp.transpose(_l2norm(q) * (GDN_DK ** -0.5), (0, 2, 1, 3))
    k = jnp.transpose(_l2norm(k), (0, 2, 1, 3))
    v = jnp.transpose(v.astype(jnp.float32), (0, 2, 1, 3))
    a = a.astype(jnp.float32).reshape(bsz, t, 2, GDN_HEADS)
    b = b.astype(jnp.float32).reshape(bsz, t, 2, GDN_HEADS)
    g = -jnp.exp(a_log.astype(jnp.float32)) * jax.nn.softplus(a + dt_bias.astype(jnp.float32))
    beta = jax.nn.sigmoid(b)
    g = jnp.transpose(g, (2, 0, 3, 1))
    beta = jnp.transpose(beta, (2, 0, 3, 1))
    flip = lambda a_: jnp.flip(a_, axis=2)
    o_fwd = _gated_delta_chunked(q, k, v, g[0], beta[0])
    o_bwd = flip(_gated_delta_chunked(flip(q), flip(k), flip(v), flip(g[1]), flip(beta[1])))
    o = jnp.transpose(o_fwd + o_bwd, (0, 2, 1, 3))
    o = o * lax.rsqrt(jnp.mean(o * o, axis=-1, keepdims=True) + EPS) * norm_w.astype(jnp.float32)
    o = o * jax.nn.silu(z.astype(jnp.float32).reshape(bsz, t, GDN_HEADS, GDN_DV))
    return o.reshape(bsz, t, GDN_W).astype(qkv.dtype)


def _hier_moe(h, wg, bg, we, be, w1, w3, w2):
    bsz, t, d = h.shape
    xt = h.reshape(bsz * t, d)
    n = xt.shape[0]
    pg = jax.nn.softmax((xt @ wg + bg).astype(jnp.float32), axis=-1)
    grp = jnp.argmax(pg, axis=-1)
    pg_top = jnp.take_along_axis(pg, grp[:, None], axis=-1)
    el = (xt @ we + be).astype(jnp.float32).reshape(n, N_GROUPS, EXPERTS_PER_GROUP)
    el_grp = jnp.take_along_axis(el, grp[:, None, None], axis=1)[:, 0]
    top_logit, top_e = lax.top_k(el_grp, TOP_K)
    gate = pg_top * jax.nn.softmax(top_logit, axis=-1)
    expert = (grp[:, None] * EXPERTS_PER_GROUP + top_e).reshape(-1)
    n_slots = n * TOP_K
    order = jnp.argsort(expert)
    sorted_e = expert[order]
    tok = order // TOP_K
    sizes = jnp.bincount(expert, length=N_EXPERTS)
    padded = (sizes + MOE_BLOCK - 1) // MOE_BLOCK * MOE_BLOCK
    pad_end = jnp.cumsum(padded)
    pad_start = pad_end - padded
    seg_start = jnp.cumsum(sizes) - sizes
    dest = pad_start[sorted_e] + jnp.arange(n_slots) - seg_start[sorted_e]
    padded_rows = (n_slots + N_EXPERTS * (MOE_BLOCK - 1) + MOE_BLOCK - 1) // MOE_BLOCK * MOE_BLOCK
    n_blocks = padded_rows // MOE_BLOCK
    buf = jnp.zeros((padded_rows, d), xt.dtype).at[dest].set(xt[tok])
    block_e = jnp.minimum(jnp.searchsorted(pad_end, jnp.arange(n_blocks) * MOE_BLOCK, side='right'), N_EXPERTS - 1)

    def expert_block(args):
        xb, e = args
        return (jax.nn.silu(xb @ w1[e]) * (xb @ w3[e])) @ w2[e]

    ys = lax.map(expert_block, (buf.reshape(n_blocks, MOE_BLOCK, d), block_e)).reshape(padded_rows, d)[dest]
    ys = ys * gate.reshape(-1)[order][:, None].astype(ys.dtype)
    return jnp.zeros_like(xt).at[tok].add(ys).reshape(bsz, t, d)


def setup_inputs(seed: int = 0) -> dict:
    key = jax.random.key(seed)
    ks = jax.random.split(key, 22)
    f32 = jnp.float32
    nrm = lambda k_, shape, s: jax.random.normal(k_, shape, f32) * s
    x = nrm(ks[0], (BATCH, SEQ, D_MODEL), 1.0)
    c = nrm(ks[1], (BATCH, D_MODEL), 1.0)
    norm_mix_w = 1.0 + nrm(ks[2], (DEPTH, D_MODEL), 0.02)
    norm_ffn_w = 1.0 + nrm(ks[3], (DEPTH, D_MODEL), 0.02)
    w_ada = nrm(ks[4], (DEPTH, D_MODEL, 6 * D_MODEL), 0.5 * D_MODEL ** -0.5)
    b_ada = nrm(ks[5], (DEPTH, 6 * D_MODEL), 0.02)
    w_in = nrm(ks[6], (DEPTH, D_MODEL, D_IN), D_MODEL ** -0.5)
    conv_a_w = nrm(ks[7], (DEPTH, CONV_K, CONV_W), CONV_K ** -0.5)
    na_rpb = nrm(ks[8], (DEPTH, NA_HEADS, 2 * NA_KH - 1, 2 * NA_KW - 1), 0.1)
    gdn_conv_w = nrm(ks[9], (DEPTH, GDN_CONV_K, GDN_QKV), GDN_CONV_K ** -0.5)
    gdn_a_log = jnp.log(jax.random.uniform(ks[10], (DEPTH, 2, GDN_HEADS), f32, 1.0, 16.0))
    dt = jnp.exp(jax.random.uniform(ks[11], (DEPTH, 2, GDN_HEADS), f32, math.log(1e-3), math.log(1e-1)))
    gdn_dt_bias = dt + jnp.log(-jnp.expm1(-dt))
    gdn_norm_w = 1.0 + nrm(ks[12], (DEPTH, GDN_DV), 0.02)
    w_out = nrm(ks[13], (DEPTH, D_MIX, D_MODEL), D_MIX ** -0.5)
    router_group_w = nrm(ks[14], (DEPTH, D_MODEL, N_GROUPS), D_MODEL ** -0.5)
    router_group_b = nrm(ks[15], (DEPTH, N_GROUPS), 0.01)
    router_expert_w = nrm(ks[16], (DEPTH, D_MODEL, N_EXPERTS), D_MODEL ** -0.5)
    router_expert_b = nrm(ks[17], (DEPTH, N_EXPERTS), 0.01)
    expert_w1 = nrm(ks[18], (DEPTH, N_EXPERTS, D_MODEL, D_EXPERT), D_MODEL ** -0.5)
    expert_w3 = nrm(ks[19], (DEPTH, N_EXPERTS, D_MODEL, D_EXPERT), D_MODEL ** -0.5)
    expert_w2 = nrm(ks[20], (DEPTH, N_EXPERTS, D_EXPERT, D_MODEL), D_EXPERT ** -0.5)
    final_norm_w = 1.0 + nrm(ks[21], (D_MODEL,), 0.02)
    return {"x": x, "c": c, "norm_mix_w": norm_mix_w, "norm_ffn_w": norm_ffn_w, "w_ada": w_ada, "b_ada": b_ada,
            "w_in": w_in, "conv_a_w": conv_a_w, "na_rpb": na_rpb, "gdn_conv_w": gdn_conv_w,
            "gdn_a_log": gdn_a_log, "gdn_dt_bias": gdn_dt_bias, "gdn_norm_w": gdn_norm_w, "w_out": w_out,
            "router_group_w": router_group_w, "router_group_b": router_group_b,
            "router_expert_w": router_expert_w, "router_expert_b": router_expert_b,
            "expert_w1": expert_w1, "expert_w3": expert_w3, "expert_w2": expert_w2, "final_norm_w": final_norm_w}


def reference(x, c, norm_mix_w, norm_ffn_w, w_ada, b_ada, w_in, conv_a_w, na_rpb, gdn_conv_w, gdn_a_log,
              gdn_dt_bias, gdn_norm_w, w_out, router_group_w, router_group_b, router_expert_w, router_expert_b,
              expert_w1, expert_w3, expert_w2, final_norm_w):
    bsz, t, _ = x.shape
    cond = jax.nn.silu(c)
    split_at = np.cumsum(IN_SPLITS)[:-1].tolist()
    for l in range(DEPTH):
        mod = cond @ w_ada[l] + b_ada[l]
        shift1, scale1, gate1, shift2, scale2, gate2 = [m[:, None, :] for m in jnp.split(mod, 6, axis=-1)]
        hmix = _rmsnorm(x, norm_mix_w[l]) * (1.0 + scale1) + shift1
        proj = hmix @ w_in[l]
        cb, cc, cx, nq, nk, nv, gqkv, gz, ga, gb = jnp.split(proj, split_at, axis=-1)
        y_conv = cb * _dwconv(cc * cx, conv_a_w[l])
        y_na = _neighbourhood_attention(nq.reshape(bsz, t, NA_HEADS, NA_DH), nk.reshape(bsz, t, NA_HEADS, NA_DH),
                                        nv.reshape(bsz, t, NA_HEADS, NA_DH), na_rpb[l])
        y_gdn = _gdn_mixer(gqkv, gz, ga, gb, gdn_conv_w[l], gdn_a_log[l], gdn_dt_bias[l], gdn_norm_w[l])
        mixed = jnp.concatenate([y_conv, y_na, y_gdn], axis=-1) @ w_out[l]
        x = x + gate1 * mixed
        hffn = _rmsnorm(x, norm_ffn_w[l]) * (1.0 + scale2) + shift2
        y_moe = _hier_moe(hffn, router_group_w[l], router_group_b[l], router_expert_w[l], router_expert_b[l],
                          expert_w1[l], expert_w3[l], expert_w2[l])
        x = x + gate2 * y_moe
    return _rmsnorm(x, final_norm_w)
```

```python
import functools

import jax
import jax.numpy as jnp
from jax import lax
from jax.experimental import pallas as pl
from jax.experimental.pallas import tpu as pltpu

F32 = jnp.float32
BF16 = jnp.bfloat16

EPS = 1e-6
GRID_W = 64
CONV_W = 256
NA_HEADS = 4
NA_DH = 64
NA_W = NA_HEADS * NA_DH
NA_KH = 8
NA_KW = 16
GDN_HEADS = 4
GDN_DK = 128
GDN_DV = 128
GDN_W = GDN_HEADS * GDN_DV
GDN_QKV = 2 * GDN_HEADS * GDN_DK + GDN_W
GDN_CHUNK = 64
N_GROUPS = 4
EXPERTS_PER_GROUP = 8
N_EXPERTS = N_GROUPS * EXPERTS_PER_GROUP
TOP_K = 2

COL_CONV = 0
COL_NA = 3 * CONV_W
COL_GQKV = COL_NA + 3 * NA_W
COL_GZ = COL_GQKV + GDN_QKV
COL_AB = COL_GZ + GDN_W
D_IN = COL_AB + 4 * GDN_HEADS
D_IN_PAD = COL_AB + 128
LANES = 128
SUBLANES = 8

NEG = -1e30
VMEM_LIMIT = 56 * 1024 * 1024

MOE_BM = 256
ROW_TILE = 512


def _cparams(sem):
    return pltpu.CompilerParams(dimension_semantics=sem, vmem_limit_bytes=VMEM_LIMIT)


def _silu(x):
    return x * (1.0 / (1.0 + jnp.exp(-x)))


def _ada_kernel(ct_ref, w_ref, b_ref, o_ref):
    w = w_ref[0]
    nb = ct_ref.shape[1]
    for r in range(nb):
        col = _silu(ct_ref[:, r:r + 1])
        o_ref[0, r:r + 1, :] = jnp.sum(w * col, axis=0, keepdims=True) + b_ref[0]


def _ada(c, w_ada, b_ada):
    depth, d, n6 = w_ada.shape
    bsz = c.shape[0]
    tn = 512
    return pl.pallas_call(
        _ada_kernel,
        grid=(depth, n6 // tn),
        in_specs=[pl.BlockSpec((d, bsz), lambda l, j: (0, 0)),
                  pl.BlockSpec((1, d, tn), lambda l, j: (l, 0, j)),
                  pl.BlockSpec((1, 1, tn), lambda l, j: (l, 0, j))],
        out_specs=pl.BlockSpec((1, bsz, tn), lambda l, j: (l, 0, j)),
        out_shape=jax.ShapeDtypeStruct((depth, bsz, n6), F32),
        compiler_params=_cparams(("parallel", "parallel")),
        name="ada",
    )(c.T, w_ada, b_ada.reshape(depth, 1, n6))


def _modnorm(x, nw, sc, sh):
    ms = jnp.mean(x * x, axis=-1, keepdims=True)
    y = x * lax.rsqrt(ms + EPS)
    return (y * nw) * (1.0 + sc) + sh


def _inproj_kernel(x_ref, nw_ref, sc_ref, sh_ref, w_ref, wabt_ref, o_ref, ot_ref):
    h = _modnorm(x_ref[0], nw_ref[...], sc_ref[0], sh_ref[0]).astype(BF16)
    o_ref[0] = jnp.dot(h, w_ref[...], preferred_element_type=F32)
    ot_ref[0] = lax.dot_general(wabt_ref[...], h, (((1,), (1,)), ((), ())), preferred_element_type=F32)


def _inproj(x, nw, sc, sh, w_pad, wabt):
    bsz, t, d = x.shape
    tm = min(ROW_TILE, t)
    nab = wabt.shape[0]
    return pl.pallas_call(
        _inproj_kernel,
        grid=(bsz, t // tm),
        in_specs=[pl.BlockSpec((1, tm, d), lambda b, i: (b, i, 0)),
                  pl.BlockSpec((1, d), lambda b, i: (0, 0)),
                  pl.BlockSpec((1, 1, d), lambda b, i: (b, 0, 0)),
                  pl.BlockSpec((1, 1, d), lambda b, i: (b, 0, 0)),
                  pl.BlockSpec((d, D_IN_PAD), lambda b, i: (0, 0)),
                  pl.BlockSpec((nab, d), lambda b, i: (0, 0))],
        out_specs=[pl.BlockSpec((1, tm, D_IN_PAD), lambda b, i: (b, i, 0)),
                   pl.BlockSpec((1, nab, tm), lambda b, i: (b, 0, i))],
        out_shape=[jax.ShapeDtypeStruct((bsz, t, D_IN_PAD), F32),
                   jax.ShapeDtypeStruct((bsz, nab, t), F32)],
        compiler_params=_cparams(("parallel", "parallel")),
        name="inproj",
    )(x, nw, sc, sh, w_pad, wabt)


def _dwconv3(u, prev_row, next_row, w_ref):
    tt = u.shape[0]
    row = lax.broadcasted_iota(jnp.int32, u.shape, 0)
    dn = jnp.where(row == 0, prev_row, pltpu.roll(u, 1, axis=0))
    up = jnp.where(row == tt - 1, next_row, pltpu.roll(u, tt - 1, axis=0))
    return w_ref[0:1, :] * dn + w_ref[1:2, :] * u + w_ref[2:3, :] * up


def _halo_specs(tt, t, width, colblk):
    nsub = tt // SUBLANES
    last = t // SUBLANES - 1
    return [pl.BlockSpec((1, tt, width), lambda b, i: (b, i, colblk)),
            pl.BlockSpec((1, SUBLANES, width), lambda b, i: (b, jnp.maximum(i * nsub - 1, 0), colblk)),
            pl.BlockSpec((1, SUBLANES, width), lambda b, i: (b, jnp.minimum((i + 1) * nsub, last), colblk))]


def _convmix_kernel(m_ref, p_ref, n_ref, w_ref, o_ref):
    i = pl.program_id(1)
    nt = pl.num_programs(1)
    m = m_ref[0]
    u = m[:, CONV_W:2 * CONV_W] * m[:, 2 * CONV_W:]
    p = p_ref[0]
    n = n_ref[0]
    pu = p[SUBLANES - 1:SUBLANES, CONV_W:2 * CONV_W] * p[SUBLANES - 1:SUBLANES, 2 * CONV_W:]
    nu = n[0:1, CONV_W:2 * CONV_W] * n[0:1, 2 * CONV_W:]
    pu = jnp.where(i == 0, 0.0, pu)
    nu = jnp.where(i == nt - 1, 0.0, nu)
    o_ref[0] = m[:, :CONV_W] * _dwconv3(u, pu, nu, w_ref)


def _convmix(proj, conv_w):
    bsz, t, _ = proj.shape
    tt = min(ROW_TILE, t)
    width = 3 * CONV_W
    return pl.pallas_call(
        _convmix_kernel,
        grid=(bsz, t // tt),
        in_specs=_halo_specs(tt, t, width, COL_CONV // width) + [pl.BlockSpec((3, CONV_W), lambda b, i: (0, 0))],
        out_specs=pl.BlockSpec((1, tt, CONV_W), lambda b, i: (b, i, 0)),
        out_shape=jax.ShapeDtypeStruct((bsz, t, CONV_W), F32),
        compiler_params=_cparams(("parallel", "parallel")),
        name="convmix",
    )(proj, proj, proj, conv_w)


NA_RB = 8
NA_TOK = NA_RB * GRID_W
NA_WIN = NA_KH * GRID_W


def _na_bias_table(rpb):
    col = jnp.arange(GRID_W)
    cstart = jnp.clip(col - NA_KW // 2, 0, GRID_W - NA_KW)
    kc = jnp.arange(GRID_W)
    valid = (kc[None, :] >= cstart[:, None]) & (kc[None, :] < cstart[:, None] + NA_KW)
    dc = jnp.clip(kc[None, :] - col[:, None] + (NA_KW - 1), 0, 2 * NA_KW - 2)
    d0 = jnp.arange(NA_KH)
    dr = d0[:, None] + jnp.arange(NA_KH)[None, :]
    tbl = rpb[:, dr][:, :, :, dc]
    tbl = jnp.where(valid[None, None, None], tbl, NEG)
    tbl = jnp.transpose(tbl, (1, 0, 3, 2, 4))
    return tbl.reshape(NA_KH, NA_HEADS, GRID_W, NA_WIN).astype(F32)


def _na_kernel(q_ref, kp_ref, kc_ref, kn_ref, vp_ref, vc_ref, vn_ref, bias_ref, o_ref, kbuf, vbuf, *, rows):
    i = pl.program_id(1)
    kbuf[0:NA_TOK, :] = kp_ref[0].astype(BF16)
    kbuf[NA_TOK:2 * NA_TOK, :] = kc_ref[0].astype(BF16)
    kbuf[2 * NA_TOK:, :] = kn_ref[0].astype(BF16)
    vbuf[0:NA_TOK, :] = vp_ref[0].astype(BF16)
    vbuf[NA_TOK:2 * NA_TOK, :] = vc_ref[0].astype(BF16)
    vbuf[2 * NA_TOK:, :] = vn_ref[0].astype(BF16)
    scale = NA_DH ** -0.5
    for j in range(NA_RB):
        r = i * NA_RB + j
        rs = jnp.clip(r - NA_KH // 2, 0, rows - NA_KH)
        start = pl.multiple_of((rs - (i - 1) * NA_RB) * GRID_W, GRID_W)
        d0 = rs - r + (NA_KH - 1)
        kwin = kbuf[pl.ds(start, NA_WIN), :]
        vwin = vbuf[pl.ds(start, NA_WIN), :]
        qrow = (q_ref[0, j * GRID_W:(j + 1) * GRID_W, :] * scale).astype(BF16)
        outs = []
        for h in range(NA_HEADS):
            sl = slice(h * NA_DH, (h + 1) * NA_DH)
            s = lax.dot_general(qrow[:, sl], kwin[:, sl], (((1,), (1,)), ((), ())), preferred_element_type=F32)
            s = s + bias_ref[d0, h]
            m = jnp.max(s, axis=-1, keepdims=True)
            p = jnp.exp(s - m)
            l = jnp.sum(p, axis=-1, keepdims=True)
            o = jnp.dot((p / l).astype(BF16), vwin[:, sl], preferred_element_type=F32)
            outs.append(o)
        o_ref[0, j * GRID_W:(j + 1) * GRID_W, :] = jnp.concatenate(outs, axis=-1)


def _na(proj, bias_tbl):
    bsz, t, _ = proj.shape
    rows = t // GRID_W
    nblk = rows // NA_RB
    qc, kc, vc = COL_NA // NA_W, COL_NA // NA_W + 1, COL_NA // NA_W + 2

    def spec(col, shift):
        return pl.BlockSpec((1, NA_TOK, NA_W), lambda b, i: (b, jnp.clip(i + shift, 0, nblk - 1), col))

    return pl.pallas_call(
        functools.partial(_na_kernel, rows=rows),
        grid=(bsz, nblk),
        in_specs=[spec(qc, 0), spec(kc, -1), spec(kc, 0), spec(kc, 1), spec(vc, -1), spec(vc, 0), spec(vc, 1),
                  pl.BlockSpec((NA_KH, NA_HEADS, GRID_W, NA_WIN), lambda b, i: (0, 0, 0, 0))],
        out_specs=pl.BlockSpec((1, NA_TOK, NA_W), lambda b, i: (b, i, 0)),
        out_shape=jax.ShapeDtypeStruct((bsz, t, NA_W), F32),
        scratch_shapes=[pltpu.VMEM((3 * NA_TOK, NA_W), BF16), pltpu.VMEM((3 * NA_TOK, NA_W), BF16)],
        compiler_params=_cparams(("parallel", "parallel")),
        name="na",
    )(proj, proj, proj, proj, proj, proj, proj, bias_tbl)


def _gdnprep_kernel(m_ref, p_ref, n_ref, w_ref, o_ref):
    i = pl.program_id(1)
    nt = pl.num_programs(1)
    pu = jnp.where(i == 0, 0.0, p_ref[0, SUBLANES - 1:SUBLANES, :])
    nu = jnp.where(i == nt - 1, 0.0, n_ref[0, 0:1, :])
    c = _silu(_dwconv3(m_ref[0], pu, nu, w_ref))
    qk_heads = 2 * GDN_HEADS
    for hh in range(GDN_QKV // GDN_DK):
        sl = slice(hh * GDN_DK, (hh + 1) * GDN_DK)
        xh = c[:, sl]
        if hh < qk_heads:
            xh = xh * lax.rsqrt(jnp.sum(xh * xh, axis=-1, keepdims=True) + EPS)
            if hh < GDN_HEADS:
                xh = xh * (GDN_DK ** -0.5)
        o_ref[0, :, sl] = xh


def _gdnprep(proj, conv_w):
    bsz, t, _ = proj.shape
    tt = min(256, t)
    return pl.pallas_call(
        _gdnprep_kernel,
        grid=(bsz, t // tt),
        in_specs=_halo_specs(tt, t, GDN_QKV, COL_GQKV // GDN_QKV) + [pl.BlockSpec((3, GDN_QKV), lambda b, i: (0, 0))],
        out_specs=pl.BlockSpec((1, tt, GDN_QKV), lambda b, i: (b, i, 0)),
        out_shape=jax.ShapeDtypeStruct((bsz, t, GDN_QKV), F32),
        compiler_params=_cparams(("parallel", "parallel")),
        name="gdnprep",
    )(proj, proj, proj, conv_w)


GDN_CB = 2
GDN_CT = GDN_CB * GDN_CHUNK


def _softplus(x):
    return jnp.maximum(x, 0.0) + jnp.log1p(jnp.exp(-jnp.abs(x)))


def _seg_cumsum(x, axis, reverse):
    n = x.shape[axis]
    pos = lax.broadcasted_iota(jnp.int32, x.shape, axis) & (GDN_CHUNK - 1)
    s = 1
    while s < GDN_CHUNK:
        if reverse:
            x = x + jnp.where(pos < GDN_CHUNK - s, pltpu.roll(x, n - s, axis=axis), 0.0)
        else:
            x = x + jnp.where(pos >= s, pltpu.roll(x, s, axis=axis), 0.0)
        s *= 2
    return x


def _nt(a, b):
    return lax.dot_general(a, b, (((1,), (1,)), ((), ())), preferred_element_type=F32)


def _tn(a, b):
    return lax.dot_general(a, b, (((0,), (0,)), ((), ())), preferred_element_type=F32)


def _gdnscan_kernel(qf_ref, qb_ref, abf_ref, abb_ref, abtf_ref, abtb_ref, alr_ref, dtr_ref, alc_ref, dtc_ref,
                    of_ref, ob_ref, s_ref):
    i = pl.program_id(1)

    @pl.when(i == 0)
    def _():
        s_ref[...] = jnp.zeros_like(s_ref)

    c = GDN_CHUNK
    ri = lax.broadcasted_iota(jnp.int32, (c, c), 0)
    ci = lax.broadcasted_iota(jnp.int32, (c, c), 1)
    eye = (ri == ci).astype(F32)
    nh = GDN_HEADS
    for d in range(2):
        q_ref = (qf_ref, qb_ref)[d]
        ab = (abf_ref, abb_ref)[d][0]
        abt = (abtf_ref, abtb_ref)[d][0]
        o_ref = (of_ref, ob_ref)[d]
        rev = d == 1
        g_col = _seg_cumsum(-jnp.exp(alr_ref[...]) * _softplus(ab + dtr_ref[...]), 0, rev)
        beta_col = 1.0 / (1.0 + jnp.exp(-ab))
        g_row = _seg_cumsum(-jnp.exp(alc_ref[...]) * _softplus(abt[0:2 * nh] + dtc_ref[...]), 1, rev)
        incl = (ri <= ci) if rev else (ri >= ci)
        strict = (ri < ci) if rev else (ri > ci)
        for cc in (range(GDN_CB - 1, -1, -1) if rev else range(GDN_CB)):
            rows = slice(cc * c, (cc + 1) * c)
            for h in range(nh):
                col = d * nh + h
                gc = g_col[rows, col:col + 1]
                gr = g_row[col:col + 1, rows]
                bcol = beta_col[rows, 2 * nh + col:2 * nh + col + 1]
                q = q_ref[0, rows, h * GDN_DK:(h + 1) * GDN_DK]
                k = q_ref[0, rows, (nh + h) * GDN_DK:(nh + h + 1) * GDN_DK]
                v = q_ref[0, rows, (2 * nh) * GDN_DK + h * GDN_DV:(2 * nh) * GDN_DK + (h + 1) * GDN_DV]
                e_incl = jnp.exp(jnp.where(incl, gc - gr, NEG))
                e_strict = jnp.where(strict, e_incl, 0.0)
                kb = k * bcol
                kbf = k.astype(BF16)
                a = _nt(kb.astype(BF16), kbf) * e_strict
                x = eye - a
                p = a
                for _ in range(5):
                    pb = p.astype(BF16)
                    p = jnp.dot(pb, pb, preferred_element_type=F32)
                    x = x + jnp.dot(x.astype(BF16), p.astype(BF16), preferred_element_type=F32)
                eg = jnp.exp(gc)
                rhs = jnp.concatenate([v * bcol, kb * eg], axis=1).astype(BF16)
                sol = jnp.dot(x.astype(BF16), rhs, preferred_element_type=F32)
                u = sol[:, :GDN_DV]
                w = sol[:, GDN_DV:]
                intra = _nt(q.astype(BF16), kbf) * e_incl
                glast = gc[0:1, :] if rev else gc[c - 1:c, :]
                kd = k * jnp.exp(glast - gc)
                st = s_ref[col]
                z = jnp.dot(jnp.concatenate([w, q * eg], axis=0).astype(BF16), st.astype(BF16),
                            preferred_element_type=F32)
                vnew = (u - z[:c]).astype(BF16)
                o = z[c:] + jnp.dot(intra.astype(BF16), vnew, preferred_element_type=F32)
                s_ref[col] = st * jnp.exp(glast) + _tn(kd.astype(BF16), vnew)
                o_ref[0, rows, h * GDN_DV:(h + 1) * GDN_DV] = o


def _gdnscan(qkv, proj, abt, a_log, dt_bias):
    bsz, t, _ = qkv.shape
    nb = t // GDN_CT
    nab = abt.shape[1]
    pad = lambda r: jnp.pad(r.reshape(1, -1), ((0, 0), (0, LANES - r.size)))
    alr, dtr = pad(a_log), pad(dt_bias)
    alc, dtc = a_log.reshape(-1, 1), dt_bias.reshape(-1, 1)
    abcol = COL_AB // LANES
    small = lambda shape: pl.BlockSpec(shape, lambda b, i: (0, 0))
    return pl.pallas_call(
        _gdnscan_kernel,
        grid=(bsz, nb),
        in_specs=[pl.BlockSpec((1, GDN_CT, GDN_QKV), lambda b, i: (b, i, 0)),
                  pl.BlockSpec((1, GDN_CT, GDN_QKV), lambda b, i: (b, nb - 1 - i, 0)),
                  pl.BlockSpec((1, GDN_CT, LANES), lambda b, i: (b, i, abcol)),
                  pl.BlockSpec((1, GDN_CT, LANES), lambda b, i: (b, nb - 1 - i, abcol)),
                  pl.BlockSpec((1, nab, GDN_CT), lambda b, i: (b, 0, i)),
                  pl.BlockSpec((1, nab, GDN_CT), lambda b, i: (b, 0, nb - 1 - i)),
                  small((1, LANES)), small((1, LANES)), small((2 * GDN_HEADS, 1)), small((2 * GDN_HEADS, 1))],
        out_specs=[pl.BlockSpec((1, GDN_CT, GDN_W), lambda b, i: (b, i, 0)),
                   pl.BlockSpec((1, GDN_CT, GDN_W), lambda b, i: (b, nb - 1 - i, 0))],
        out_shape=[jax.ShapeDtypeStruct((bsz, t, GDN_W), F32), jax.ShapeDtypeStruct((bsz, t, GDN_W), F32)],
        scratch_shapes=[pltpu.VMEM((2 * GDN_HEADS, GDN_DK, GDN_DV), F32)],
        compiler_params=_cparams(("arbitrary", "arbitrary")),
        name="gdnscan",
    )(qkv, qkv, proj, proj, abt, abt, alr, dtr, alc, dtc)


def _outproj_kernel(x_ref, yc_ref, yn_ref, of_ref, ob_ref, z_ref, gnw_ref, wo_ref, g1_ref, nw_ref, sc_ref, sh_ref,
                    wr_ref, br_ref, xo_ref, h_ref, e_ref, g_ref):
    o = of_ref[0] + ob_ref[0]
    z = z_ref[0]
    parts = [yc_ref[0], yn_ref[0]]
    for h in range(GDN_HEADS):
        sl = slice(h * GDN_DV, (h + 1) * GDN_DV)
        oh = o[:, sl]
        oh = oh * lax.rsqrt(jnp.mean(oh * oh, axis=-1, keepdims=True) + EPS) * gnw_ref[...]
        parts.append(oh * _silu(z[:, sl]))
    mixed = jnp.dot(jnp.concatenate(parts, axis=-1).astype(BF16), wo_ref[...], preferred_element_type=F32)
    xn = x_ref[0] + g1_ref[0] * mixed
    xo_ref[0] = xn
    hf = _modnorm(xn, nw_ref[...], sc_ref[0], sh_ref[0])
    h_ref[0] = hf
    logits = jnp.dot(hf, wr_ref[...], preferred_element_type=F32, precision=lax.Precision.HIGHEST) + br_ref[...]
    lane = lax.broadcasted_iota(jnp.int32, logits.shape, 1)
    gl = jnp.where(lane < N_GROUPS, logits, NEG)
    gm = jnp.max(gl, axis=-1, keepdims=True)
    den = jnp.sum(jnp.exp(gl - gm), axis=-1, keepdims=True)
    grp = jnp.min(jnp.where(gl == gm, lane, LANES), axis=-1, keepdims=True)
    pg_top = 1.0 / den
    ex = lane - N_GROUPS
    in_grp = (ex >= grp * EXPERTS_PER_GROUP) & (ex < (grp + 1) * EXPERTS_PER_GROUP)
    el = jnp.where(in_grp, logits, NEG)
    m1 = jnp.max(el, axis=-1, keepdims=True)
    i1 = jnp.min(jnp.where(el == m1, lane, LANES), axis=-1, keepdims=True)
    el2 = jnp.where(lane == i1, NEG, el)
    m2 = jnp.max(el2, axis=-1, keepdims=True)
    i2 = jnp.min(jnp.where(el2 == m2, lane, LANES), axis=-1, keepdims=True)
    e2 = jnp.exp(m2 - m1)
    w1 = pg_top / (1.0 + e2)
    w2 = pg_top * e2 / (1.0 + e2)
    e_ref[0] = jnp.where(lane == 0, i1 - N_GROUPS, jnp.where(lane == 1, i2 - N_GROUPS, 0))
    g_ref[0] = jnp.where(lane == 0, w1, jnp.where(lane == 1, w2, 0.0))


def _outproj(x, yc, yn, of, ob, proj, gnw, wo, g1, nw, sc, sh, wr, br):
    bsz, t, d = x.shape
    tm = min(ROW_TILE, t)
    tok = lambda w: pl.BlockSpec((1, tm, w), lambda b, i: (b, i, 0))
    perb = pl.BlockSpec((1, 1, d), lambda b, i: (b, 0, 0))
    full = lambda shape: pl.BlockSpec(shape, lambda b, i: (0, 0))
    return pl.pallas_call(
        _outproj_kernel,
        grid=(bsz, t // tm),
        in_specs=[tok(d), tok(CONV_W), tok(NA_W), tok(GDN_W), tok(GDN_W),
                  pl.BlockSpec((1, tm, GDN_W), lambda b, i: (b, i, COL_GZ // GDN_W)),
                  full((1, GDN_DV)), full((d, d)), perb, full((1, d)), perb, perb,
                  full((d, LANES)), full((1, LANES))],
        out_specs=[tok(d), tok(d), tok(LANES), tok(LANES)],
        out_shape=[jax.ShapeDtypeStruct((bsz, t, d), F32), jax.ShapeDtypeStruct((bsz, t, d), F32),
                   jax.ShapeDtypeStruct((bsz, t, LANES), jnp.int32), jax.ShapeDtypeStruct((bsz, t, LANES), F32)],
        compiler_params=_cparams(("parallel", "parallel")),
        name="outproj",
    )(x, yc, yn, of, ob, proj, gnw, wo, g1, nw, sc, sh, wr, br)


def _experts_kernel(be_ref, nu_ref, tc_ref, tn_ref, gate_ref, h_hbm, w1_ref, w3_ref, w2_ref, o_ref,
                    xbuf, w1b, w3b, w2b, sem):
    i = pl.program_id(0)
    nblk = pl.num_programs(0)
    used = nu_ref[0]
    bm = xbuf.shape[1]

    def row_copy(tok, slot, r):
        return pltpu.make_async_copy(h_hbm.at[pl.ds(tok, 1), :], xbuf.at[slot, pl.ds(r, 1), :], sem.at[slot])

    def start_gather(idx_ref, slot):
        def body(r, carry):
            row_copy(idx_ref[0, 0, r], slot, r).start()
            return carry
        lax.fori_loop(0, bm, body, 0)

    def wait_gather(slot):
        def body(r, carry):
            row_copy(0, slot, r).wait()
            return carry
        lax.fori_loop(0, bm, body, 0)

    slot = i % 2

    @pl.when(i == 0)
    def _():
        start_gather(tc_ref, 0)

    @pl.when((i + 1 < nblk) & (i + 1 < used))
    def _():
        start_gather(tn_ref, 1 - slot)

    @pl.when((i == 0) | (be_ref[i] != be_ref[jnp.maximum(i - 1, 0)]))
    def _():
        w1b[...] = w1_ref[0].astype(BF16)
        w3b[...] = w3_ref[0].astype(BF16)
        w2b[...] = w2_ref[0].astype(BF16)

    @pl.when(i < used)
    def _():
        wait_gather(slot)
        xb = xbuf[slot].astype(BF16)
        h1 = jnp.dot(xb, w1b[...], preferred_element_type=F32)
        h3 = jnp.dot(xb, w3b[...], preferred_element_type=F32)
        act = (_silu(h1) * h3).astype(BF16)
        y = jnp.dot(act, w2b[...], preferred_element_type=F32)
        o_ref[...] = y * gate_ref[...]

    @pl.when(i >= used)
    def _():
        o_ref[...] = jnp.zeros_like(o_ref)


def _experts(hf, block_e, n_used, tokp, gatep, w1, w3, w2):
    n, d = hf.shape
    padded_rows = tokp.shape[0]
    bm = MOE_BM
    nblk = padded_rows // bm
    de = w1.shape[-1]
    tok3 = tokp.reshape(nblk, 1, bm)
    grid_spec = pltpu.PrefetchScalarGridSpec(
        num_scalar_prefetch=2,
        grid=(nblk,),
        in_specs=[pl.BlockSpec((1, 1, bm), lambda i, be, nu: (i, 0, 0), memory_space=pltpu.SMEM),
                  pl.BlockSpec((1, 1, bm), lambda i, be, nu: (jnp.minimum(i + 1, nblk - 1), 0, 0),
                               memory_space=pltpu.SMEM),
                  pl.BlockSpec((bm, 1), lambda i, be, nu: (i, 0)),
                  pl.BlockSpec(memory_space=pl.ANY),
                  pl.BlockSpec((1, d, de), lambda i, be, nu: (be[i], 0, 0)),
                  pl.BlockSpec((1, d, de), lambda i, be, nu: (be[i], 0, 0)),
                  pl.BlockSpec((1, de, d), lambda i, be, nu: (be[i], 0, 0))],
        out_specs=pl.BlockSpec((bm, d), lambda i, be, nu: (i, 0)),
        scratch_shapes=[pltpu.VMEM((2, bm, d), F32), pltpu.VMEM((d, de), BF16), pltpu.VMEM((d, de), BF16),
                        pltpu.VMEM((de, d), BF16), pltpu.SemaphoreType.DMA((2,))])
    return pl.pallas_call(
        _experts_kernel,
        grid_spec=grid_spec,
        out_shape=jax.ShapeDtypeStruct((padded_rows, d), F32),
        compiler_params=_cparams(("arbitrary",)),
        name="experts",
    )(block_e, n_used, tok3, tok3, gatep.reshape(padded_rows, 1), hf, w1, w3, w2)


def _combine_kernel(pc_ref, pn_ref, x_ref, g2_ref, fw_ref, ys_hbm, o_ref, ybuf, sem, *, final):
    i = pl.program_id(0)
    nt = pl.num_programs(0)
    tm = x_ref.shape[0]

    def row_copy(p, slot, r):
        return pltpu.make_async_copy(ys_hbm.at[pl.ds(p, 1), :], ybuf.at[slot, pl.ds(r, 1), :], sem.at[slot])

    def start_gather(idx_ref, slot):
        def body(r, carry):
            row_copy(idx_ref[0, 0, r], slot, r).start()
            return carry
        lax.fori_loop(0, 2 * tm, body, 0)

    def wait_gather(slot):
        def body(r, carry):
            row_copy(0, slot, r).wait()
            return carry
        lax.fori_loop(0, 2 * tm, body, 0)

    slot = i % 2

    @pl.when(i == 0)
    def _():
        start_gather(pc_ref, 0)

    @pl.when(i + 1 < nt)
    def _():
        start_gather(pn_ref, 1 - slot)

    wait_gather(slot)
    y = ybuf[slot, 0:tm, :] + ybuf[slot, tm:2 * tm, :]
    xn = x_ref[...] + g2_ref[0] * y
    if final:
        ms = jnp.mean(xn * xn, axis=-1, keepdims=True)
        xn = xn * lax.rsqrt(ms + EPS) * fw_ref[...]
    o_ref[...] = xn


def _combine(x2, g2, fw, ys, pos, t, final):
    n, d = x2.shape
    tm = 256
    nt = n // tm
    per_b = t // tm
    pos3 = jnp.transpose(pos.reshape(nt, tm, TOP_K), (0, 2, 1)).reshape(nt, 1, TOP_K * tm)
    return pl.pallas_call(
        functools.partial(_combine_kernel, final=final),
        grid=(nt,),
        in_specs=[pl.BlockSpec((1, 1, TOP_K * tm), lambda i: (i, 0, 0), memory_space=pltpu.SMEM),
                  pl.BlockSpec((1, 1, TOP_K * tm), lambda i: (jnp.minimum(i + 1, nt - 1), 0, 0),
                               memory_space=pltpu.SMEM),
                  pl.BlockSpec((tm, d), lambda i: (i, 0)),
                  pl.BlockSpec((1, 1, d), lambda i: (i // per_b, 0, 0)),
                  pl.BlockSpec((1, d), lambda i: (0, 0)),
                  pl.BlockSpec(memory_space=pl.ANY)],
        out_specs=pl.BlockSpec((tm, d), lambda i: (i, 0)),
        out_shape=jax.ShapeDtypeStruct((n, d), F32),
        scratch_shapes=[pltpu.VMEM((2, TOP_K * tm, d), F32), pltpu.SemaphoreType.DMA((2,))],
        compiler_params=_cparams(("arbitrary",)),
        name="combine",
    )(pos3, pos3, x2, g2, fw, ys)


def _dispatch_plan(eidx, gates):
    n = eidx.shape[0]
    n_slots = n * TOP_K
    bm = MOE_BM
    expert = eidx.reshape(-1)
    order = jnp.argsort(expert).astype(jnp.int32)
    sorted_e = expert[order]
    sizes = jnp.bincount(expert, length=N_EXPERTS).astype(jnp.int32)
    padded = (sizes + bm - 1) // bm * bm
    pad_end = jnp.cumsum(padded)
    pad_start = pad_end - padded
    seg_start = jnp.cumsum(sizes) - sizes
    dest = pad_start[sorted_e] + jnp.arange(n_slots, dtype=jnp.int32) - seg_start[sorted_e]
    padded_rows = (n_slots + N_EXPERTS * (bm - 1) + bm - 1) // bm * bm
    n_blocks = padded_rows // bm
    block_e = jnp.minimum(jnp.searchsorted(pad_end, jnp.arange(n_blocks, dtype=jnp.int32) * bm, side='right'),
                          N_EXPERTS - 1).astype(jnp.int32)
    n_used = (pad_end[-1] // bm).astype(jnp.int32).reshape(1)
    rowp = jnp.arange(padded_rows, dtype=jnp.int32)
    e_of_row = block_e[rowp // bm]
    off = rowp - pad_start[e_of_row]
    valid = (off >= 0) & (off < sizes[e_of_row])
    src = order[jnp.clip(seg_start[e_of_row] + off, 0, n_slots - 1)]
    tokp = jnp.where(valid, src // TOP_K, 0).astype(jnp.int32)
    gatep = jnp.where(valid, gates.reshape(-1)[src], 0.0)
    pos = jnp.zeros((n_slots,), jnp.int32).at[order].set(dest.astype(jnp.int32))
    return block_e, n_used, tokp, gatep, pos


def kernel(x, c, norm_mix_w, norm_ffn_w, w_ada, b_ada, w_in, conv_a_w, na_rpb, gdn_conv_w, gdn_a_log, gdn_dt_bias,
           gdn_norm_w, w_out, router_group_w, router_group_b, router_expert_w, router_expert_b, expert_w1,
           expert_w3, expert_w2, final_norm_w):
    bsz, t, d = x.shape
    depth = w_ada.shape[0]
    n = bsz * t
    mod = _ada(c, w_ada, b_ada)
    w_in_pad = jnp.pad(w_in.astype(BF16), ((0, 0), (0, 0), (0, D_IN_PAD - D_IN)))
    wabt = jnp.transpose(w_in[:, :, COL_AB:], (0, 2, 1)).astype(BF16)
    w_out_b = w_out.astype(BF16)
    wr = jnp.pad(jnp.concatenate([router_group_w, router_expert_w], axis=-1),
                 ((0, 0), (0, 0), (0, LANES - N_GROUPS - N_EXPERTS)))
    br = jnp.pad(jnp.concatenate([router_group_b, router_expert_b], axis=-1),
                 ((0, 0), (0, LANES - N_GROUPS - N_EXPERTS)))
    fw = final_norm_w.reshape(1, d)
    for l in range(depth):
        sh1, sc1, g1, sh2, sc2, g2 = [mod[l, :, j * d:(j + 1) * d].reshape(bsz, 1, d) for j in range(6)]
        proj, abt = _inproj(x, norm_mix_w[l].reshape(1, d), sc1, sh1, w_in_pad[l], wabt[l])
        y_conv = _convmix(proj, conv_a_w[l])
        y_na = _na(proj, _na_bias_table(na_rpb[l]))
        qkv = _gdnprep(proj, gdn_conv_w[l])
        o_f, o_b = _gdnscan(qkv, proj, abt, gdn_a_log[l], gdn_dt_bias[l])
        x, hf, eidx, gates = _outproj(x, y_conv, y_na, o_f, o_b, proj, gdn_norm_w[l].reshape(1, GDN_DV), w_out_b[l],
                                      g1, norm_ffn_w[l].reshape(1, d), sc2, sh2, wr[l], br[l].reshape(1, LANES))
        block_e, n_used, tokp, gatep, pos = _dispatch_plan(eidx.reshape(n, LANES)[:, :TOP_K],
                                                           gates.reshape(n, LANES)[:, :TOP_K])
        ys = _experts(hf.reshape(n, d), block_e, n_used, tokp, gatep, expert_w1[l], expert_w3[l], expert_w2[l])
        x = _combine(x.reshape(n, d), g2, fw, ys, pos, t, final=(l == depth - 1)).reshape(bsz, t, d)
    return x
```

```python
import functools

import jax
import jax.numpy as jnp
from jax import lax
from jax.experimental import pallas as pl
from jax.experimental.pallas import tpu as pltpu

F32 = jnp.float32
BF16 = jnp.bfloat16

EPS = 1e-6
GRID_W = 64
CONV_W = 256
NA_HEADS = 4
NA_DH = 64
NA_W = NA_HEADS * NA_DH
NA_KH = 8
NA_KW = 16
GDN_HEADS = 4
GDN_DK = 128
GDN_DV = 128
GDN_W = GDN_HEADS * GDN_DV
GDN_QKV = 2 * GDN_HEADS * GDN_DK + GDN_W
GDN_CHUNK = 64
N_GROUPS = 4
EXPERTS_PER_GROUP = 8
N_EXPERTS = N_GROUPS * EXPERTS_PER_GROUP
TOP_K = 2

COL_CONV = 0
COL_NA = 3 * CONV_W
COL_GQKV = COL_NA + 3 * NA_W
COL_GZ = COL_GQKV + GDN_QKV
COL_AB = COL_GZ + GDN_W
D_IN = COL_AB + 4 * GDN_HEADS
D_IN_PAD = COL_AB + 128
LANES = 128
SUBLANES = 8

NEG = -1e30
VMEM_LIMIT = 56 * 1024 * 1024

MOE_BM = 256
ROW_TILE = 512


def _cparams(sem):
    return pltpu.CompilerParams(dimension_semantics=sem, vmem_limit_bytes=VMEM_LIMIT)


def _silu(x):
    return x * (1.0 / (1.0 + jnp.exp(-x)))


def _ada_kernel(ct_ref, w_ref, b_ref, o_ref):
    w = w_ref[0]
    nb = ct_ref.shape[1]
    for r in range(nb):
        col = _silu(ct_ref[:, r:r + 1])
        o_ref[0, r:r + 1, :] = jnp.sum(w * col, axis=0, keepdims=True) + b_ref[0]


def _ada(c, w_ada, b_ada):
    depth, d, n6 = w_ada.shape
    bsz = c.shape[0]
    tn = 512
    return pl.pallas_call(
        _ada_kernel,
        grid=(depth, n6 // tn),
        in_specs=[pl.BlockSpec((d, bsz), lambda l, j: (0, 0)),
                  pl.BlockSpec((1, d, tn), lambda l, j: (l, 0, j)),
                  pl.BlockSpec((1, 1, tn), lambda l, j: (l, 0, j))],
        out_specs=pl.BlockSpec((1, bsz, tn), lambda l, j: (l, 0, j)),
        out_shape=jax.ShapeDtypeStruct((depth, bsz, n6), F32),
        compiler_params=_cparams(("parallel", "parallel")),
        name="ada",
    )(c.T, w_ada, b_ada.reshape(depth, 1, n6))


def _modnorm(x, nw, sc, sh):
    ms = jnp.mean(x * x, axis=-1, keepdims=True)
    y = x * lax.rsqrt(ms + EPS)
    return (y * nw) * (1.0 + sc) + sh


def _inproj_kernel(x_ref, nw_ref, sc_ref, sh_ref, w_ref, wabt_ref, o_ref, ot_ref):
    h = _modnorm(x_ref[0], nw_ref[...], sc_ref[0], sh_ref[0]).astype(BF16)
    o_ref[0] = jnp.dot(h, w_ref[...], preferred_element_type=F32)
    ot_ref[0] = lax.dot_general(wabt_ref[...], h, (((1,), (1,)), ((), ())), preferred_element_type=F32)


def _inproj(x, nw, sc, sh, w_pad, wabt):
    bsz, t, d = x.shape
    tm = min(ROW_TILE, t)
    nab = wabt.shape[0]
    return pl.pallas_call(
        _inproj_kernel,
        grid=(bsz, t // tm),
        in_specs=[pl.BlockSpec((1, tm, d), lambda b, i: (b, i, 0)),
                  pl.BlockSpec((1, d), lambda b, i: (0, 0)),
                  pl.BlockSpec((1, 1, d), lambda b, i: (b, 0, 0)),
                  pl.BlockSpec((1, 1, d), lambda b, i: (b, 0, 0)),
                  pl.BlockSpec((d, D_IN_PAD), lambda b, i: (0, 0)),
                  pl.BlockSpec((nab, d), lambda b, i: (0, 0))],
        out_specs=[pl.BlockSpec((1, tm, D_IN_PAD), lambda b, i: (b, i, 0)),
                   pl.BlockSpec((1, nab, tm), lambda b, i: (b, 0, i))],
        out_shape=[jax.ShapeDtypeStruct((bsz, t, D_IN_PAD), F32),
                   jax.ShapeDtypeStruct((bsz, nab, t), F32)],
        compiler_params=_cparams(("parallel", "parallel")),
        name="inproj",
    )(x, nw, sc, sh, w_pad, wabt)


def _dwconv3(u, prev_row, next_row, w_ref):
    tt = u.shape[0]
    row = lax.broadcasted_iota(jnp.int32, u.shape, 0)
    dn = jnp.where(row == 0, prev_row, pltpu.roll(u, 1, axis=0))
    up = jnp.where(row == tt - 1, next_row, pltpu.roll(u, tt - 1, axis=0))
    return w_ref[0:1, :] * dn + w_ref[1:2, :] * u + w_ref[2:3, :] * up


def _halo_specs(tt, t, width, colblk):
    nsub = tt // SUBLANES
    last = t // SUBLANES - 1
    return [pl.BlockSpec((1, tt, width), lambda b, i: (b, i, colblk)),
            pl.BlockSpec((1, SUBLANES, width), lambda b, i: (b, jnp.maximum(i * nsub - 1, 0), colblk)),
            pl.BlockSpec((1, SUBLANES, width), lambda b, i: (b, jnp.minimum((i + 1) * nsub, last), colblk))]


def _convmix_kernel(m_ref, p_ref, n_ref, w_ref, o_ref):
    i = pl.program_id(1)
    nt = pl.num_programs(1)
    m = m_ref[0]
    u = m[:, CONV_W:2 * CONV_W] * m[:, 2 * CONV_W:]
    p = p_ref[0]
    n = n_ref[0]
    pu = p[SUBLANES - 1:SUBLANES, CONV_W:2 * CONV_W] * p[SUBLANES - 1:SUBLANES, 2 * CONV_W:]
    nu = n[0:1, CONV_W:2 * CONV_W] * n[0:1, 2 * CONV_W:]
    pu = jnp.where(i == 0, 0.0, pu)
    nu = jnp.where(i == nt - 1, 0.0, nu)
    o_ref[0] = m[:, :CONV_W] * _dwconv3(u, pu, nu, w_ref)


def _convmix(proj, conv_w):
    bsz, t, _ = proj.shape
    tt = min(ROW_TILE, t)
    width = 3 * CONV_W
    return pl.pallas_call(
        _convmix_kernel,
        grid=(bsz, t // tt),
        in_specs=_halo_specs(tt, t, width, COL_CONV // width) + [pl.BlockSpec((3, CONV_W), lambda b, i: (0, 0))],
        out_specs=pl.BlockSpec((1, tt, CONV_W), lambda b, i: (b, i, 0)),
        out_shape=jax.ShapeDtypeStruct((bsz, t, CONV_W), F32),
        compiler_params=_cparams(("parallel", "parallel")),
        name="convmix",
    )(proj, proj, proj, conv_w)


NA_RB = 8
NA_TOK = NA_RB * GRID_W
NA_WIN = NA_KH * GRID_W


def _na_bias_table(rpb):
    col = jnp.arange(GRID_W)
    cstart = jnp.clip(col - NA_KW // 2, 0, GRID_W - NA_KW)
    kc = jnp.arange(GRID_W)
    valid = (kc[None, :] >= cstart[:, None]) & (kc[None, :] < cstart[:, None] + NA_KW)
    dc = jnp.clip(kc[None, :] - col[:, None] + (NA_KW - 1), 0, 2 * NA_KW - 2)
    d0 = jnp.arange(NA_KH)
    dr = d0[:, None] + jnp.arange(NA_KH)[None, :]
    tbl = rpb[:, dr][:, :, :, dc]
    tbl = jnp.where(valid[None, None, None], tbl, NEG)
    tbl = jnp.transpose(tbl, (1, 0, 3, 2, 4))
    return tbl.reshape(NA_KH, NA_HEADS, GRID_W, NA_WIN).astype(F32)


def _na_kernel(q_ref, kp_ref, kc_ref, kn_ref, vp_ref, vc_ref, vn_ref, bias_ref, o_ref, kbuf, vbuf, *, rows):
    i = pl.program_id(1)
    kbuf[0:NA_TOK, :] = kp_ref[0].astype(BF16)
    kbuf[NA_TOK:2 * NA_TOK, :] = kc_ref[0].astype(BF16)
    kbuf[2 * NA_TOK:, :] = kn_ref[0].astype(BF16)
    vbuf[0:NA_TOK, :] = vp_ref[0].astype(BF16)
    vbuf[NA_TOK:2 * NA_TOK, :] = vc_ref[0].astype(BF16)
    vbuf[2 * NA_TOK:, :] = vn_ref[0].astype(BF16)
    scale = NA_DH ** -0.5
    for j in range(NA_RB):
        r = i * NA_RB + j
        rs = jnp.clip(r - NA_KH // 2, 0, rows - NA_KH)
        start = pl.multiple_of((rs - (i - 1) * NA_RB) * GRID_W, GRID_W)
        d0 = rs - r + (NA_KH - 1)
        kwin = kbuf[pl.ds(start, NA_WIN), :]
        vwin = vbuf[pl.ds(start, NA_WIN), :]
        qrow = (q_ref[0, j * GRID_W:(j + 1) * GRID_W, :] * scale).astype(BF16)
        outs = []
        for h in range(NA_HEADS):
            sl = slice(h * NA_DH, (h + 1) * NA_DH)
            s = lax.dot_general(qrow[:, sl], kwin[:, sl], (((1,), (1,)), ((), ())), preferred_element_type=F32)
            s = s + bias_ref[d0, h]
            m = jnp.max(s, axis=-1, keepdims=True)
            p = jnp.exp(s - m)
            l = jnp.sum(p, axis=-1, keepdims=True)
            o = jnp.dot((p / l).astype(BF16), vwin[:, sl], preferred_element_type=F32)
            outs.append(o)
        o_ref[0, j * GRID_W:(j + 1) * GRID_W, :] = jnp.concatenate(outs, axis=-1)


def _na(proj, bias_tbl):
    bsz, t, _ = proj.shape
    rows = t // GRID_W
    nblk = rows // NA_RB
    qc, kc, vc = COL_NA // NA_W, COL_NA // NA_W + 1, COL_NA // NA_W + 2

    def spec(col, shift):
        return pl.BlockSpec((1, NA_TOK, NA_W), lambda b, i: (b, jnp.clip(i + shift, 0, nblk - 1), col))

    return pl.pallas_call(
        functools.partial(_na_kernel, rows=rows),
        grid=(bsz, nblk),
        in_specs=[spec(qc, 0), spec(kc, -1), spec(kc, 0), spec(kc, 1), spec(vc, -1), spec(vc, 0), spec(vc, 1),
                  pl.BlockSpec((NA_KH, NA_HEADS, GRID_W, NA_WIN), lambda b, i: (0, 0, 0, 0))],
        out_specs=pl.BlockSpec((1, NA_TOK, NA_W), lambda b, i: (b, i, 0)),
        out_shape=jax.ShapeDtypeStruct((bsz, t, NA_W), F32),
        scratch_shapes=[pltpu.VMEM((3 * NA_TOK, NA_W), BF16), pltpu.VMEM((3 * NA_TOK, NA_W), BF16)],
        compiler_params=_cparams(("parallel", "parallel")),
        name="na",
    )(proj, proj, proj, proj, proj, proj, proj, bias_tbl)


def _gdnprep_kernel(m_ref, p_ref, n_ref, w_ref, o_ref):
    i = pl.program_id(1)
    nt = pl.num_programs(1)
    pu = jnp.where(i == 0, 0.0, p_ref[0, SUBLANES - 1:SUBLANES, :])
    nu = jnp.where(i == nt - 1, 0.0, n_ref[0, 0:1, :])
    c = _silu(_dwconv3(m_ref[0], pu, nu, w_ref))
    qk_heads = 2 * GDN_HEADS
    for hh in range(GDN_QKV // GDN_DK):
        sl = slice(hh * GDN_DK, (hh + 1) * GDN_DK)
        xh = c[:, sl]
        if hh < qk_heads:
            xh = xh * lax.rsqrt(jnp.sum(xh * xh, axis=-1, keepdims=True) + EPS)
            if hh < GDN_HEADS:
                xh = xh * (GDN_DK ** -0.5)
        o_ref[0, :, sl] = xh


def _gdnprep(proj, conv_w):
    bsz, t, _ = proj.shape
    tt = min(256, t)
    return pl.pallas_call(
        _gdnprep_kernel,
        grid=(bsz, t // tt),
        in_specs=_halo_specs(tt, t, GDN_QKV, COL_GQKV // GDN_QKV) + [pl.BlockSpec((3, GDN_QKV), lambda b, i: (0, 0))],
        out_specs=pl.BlockSpec((1, tt, GDN_QKV), lambda b, i: (b, i, 0)),
        out_shape=jax.ShapeDtypeStruct((bsz, t, GDN_QKV), F32),
        compiler_params=_cparams(("parallel", "parallel")),
        name="gdnprep",
    )(proj, proj, proj, conv_w)


GDN_GS = 128
GDN_UNITS = 2 * GDN_HEADS
GDN_GC = GDN_GS // GDN_CHUNK


def _softplus(x):
    return jnp.maximum(x, 0.0) + jnp.log1p(jnp.exp(-jnp.abs(x)))


def _seg_cumsum(x, axis, reverse):
    n = x.shape[axis]
    pos = lax.broadcasted_iota(jnp.int32, x.shape, axis) & (GDN_CHUNK - 1)
    s = 1
    while s < GDN_CHUNK:
        if reverse:
            x = x + jnp.where(pos < GDN_CHUNK - s, pltpu.roll(x, n - s, axis=axis), 0.0)
        else:
            x = x + jnp.where(pos >= s, pltpu.roll(x, s, axis=axis), 0.0)
        s *= 2
    return x


def _nt(a, b):
    return lax.dot_general(a, b, (((1,), (1,)), ((), ())), preferred_element_type=F32)


def _tn(a, b):
    return lax.dot_general(a, b, (((0,), (0,)), ((), ())), preferred_element_type=F32)


def _mm(a, b):
    return jnp.dot(a, b, preferred_element_type=F32)


def _gdnchunk_kernel(q_ref, ab_ref, abt_ref, alr_ref, dtr_ref, alc_ref, dtc_ref,
                     u_ref, wq_ref, kd_ref, in_ref, gl_ref):
    gs, c, nh = GDN_GS, GDN_CHUNK, GDN_HEADS
    ri = lax.broadcasted_iota(jnp.int32, (gs, gs), 0)
    ci = lax.broadcasted_iota(jnp.int32, (gs, gs), 1)
    same = (ri // c) == (ci // c)
    eye = (ri == ci).astype(F32)
    rowc = lax.broadcasted_iota(jnp.int32, (gs, 1), 0) // c
    ab = ab_ref[0]
    abt = abt_ref[0]
    graw_c = -jnp.exp(alr_ref[...]) * _softplus(ab + dtr_ref[...])
    graw_r = -jnp.exp(alc_ref[...]) * _softplus(abt[0:2 * nh] + dtc_ref[...])
    beta_c = 1.0 / (1.0 + jnp.exp(-ab))
    g_col = [_seg_cumsum(graw_c, 0, False), _seg_cumsum(graw_c, 0, True)]
    g_row = [_seg_cumsum(graw_r, 1, False), _seg_cumsum(graw_r, 1, True)]
    incl = [same & (ri >= ci), same & (ri <= ci)]
    strict = [same & (ri > ci), same & (ri < ci)]

    qs, ks, vs, grams = [], [], [], []
    for h in range(nh):
        q = q_ref[0, :, h * GDN_DK:(h + 1) * GDN_DK]
        k = q_ref[0, :, (nh + h) * GDN_DK:(nh + h + 1) * GDN_DK]
        v = q_ref[0, :, 2 * nh * GDN_DK + h * GDN_DV:2 * nh * GDN_DK + (h + 1) * GDN_DV]
        qs.append(q), ks.append(k), vs.append(v)
        grams.append(_nt(jnp.concatenate([q, k], axis=0).astype(BF16), k.astype(BF16)))

    units = [(d, h) for d in range(2) for h in range(nh)]
    gcs, bcs, intras, xs, ps = [], [], [], [], []
    for d, h in units:
        col = d * nh + h
        gc = g_col[d][:, col:col + 1]
        gr = g_row[d][col:col + 1, :]
        bc = beta_c[:, 2 * nh + col:2 * nh + col + 1]
        e_incl = jnp.exp(jnp.where(incl[d], gc - gr, NEG))
        a = grams[h][gs:] * bc * jnp.where(strict[d], e_incl, 0.0)
        gcs.append(gc), bcs.append(bc)
        intras.append(grams[h][:gs] * e_incl)
        xs.append(eye - a), ps.append(a)
    for _ in range(5):
        pbs = [p.astype(BF16) for p in ps]
        ps = [_mm(pb, pb) for pb in pbs]
        xs = [x + _mm(x.astype(BF16), p.astype(BF16)) for x, p in zip(xs, ps)]
    egs = [jnp.exp(gc) for gc in gcs]
    sols = [_mm(x.astype(BF16),
                jnp.concatenate([vs[h] * bc, ks[h] * (bc * eg)], axis=1).astype(BF16))
            for (d, h), x, bc, eg in zip(units, xs, bcs, egs)]
    for (d, h), gc, eg, sol, intra in zip(units, gcs, egs, sols, intras):
        col = d * nh + h
        glast_col = jnp.zeros_like(gc)
        for n in range(GDN_GC):
            r = n * c if d == 1 else (n + 1) * c - 1
            glast = gc[r:r + 1, :]
            glast_col = jnp.where(rowc == n, glast, glast_col)
            gl_ref[0, 0, col * GDN_GC + n:col * GDN_GC + n + 1, :] = jnp.broadcast_to(jnp.exp(glast), (1, LANES))
        u_ref[0, col] = sol[:, :GDN_DV]
        w = sol[:, GDN_DV:].astype(BF16)
        qd = (qs[h] * eg).astype(BF16)
        for n in range(GDN_GC):
            wq_ref[0, col, 2 * n * c:(2 * n + 1) * c, :] = w[n * c:(n + 1) * c]
            wq_ref[0, col, (2 * n + 1) * c:(2 * n + 2) * c, :] = qd[n * c:(n + 1) * c]
        kd_ref[0, col] = (ks[h] * jnp.exp(glast_col - gc)).astype(BF16)
        in_ref[0, col] = jnp.concatenate([intra[n * c:(n + 1) * c, n * c:(n + 1) * c] for n in range(GDN_GC)],
                                         axis=0).astype(BF16)


def _gdnchunk(qkv, proj, abt, a_log, dt_bias):
    bsz, t, _ = qkv.shape
    gs, nu = GDN_GS, GDN_UNITS
    ng = t // gs
    nab = abt.shape[1]
    pad = lambda r: jnp.pad(r.reshape(1, -1), ((0, 0), (0, LANES - r.size)))
    alr, dtr = pad(a_log), pad(dt_bias)
    alc, dtc = a_log.reshape(-1, 1), dt_bias.reshape(-1, 1)
    small = lambda shape: pl.BlockSpec(shape, lambda b, i: (0, 0))
    unit = lambda rows, w: pl.BlockSpec((1, nu, rows, w), lambda b, i: (b, 0, i, 0))
    return pl.pallas_call(
        _gdnchunk_kernel,
        grid=(bsz, ng),
        in_specs=[pl.BlockSpec((1, gs, GDN_QKV), lambda b, i: (b, i, 0)),
                  pl.BlockSpec((1, gs, LANES), lambda b, i: (b, i, COL_AB // LANES)),
                  pl.BlockSpec((1, nab, gs), lambda b, i: (b, 0, i)),
                  small((1, LANES)), small((1, LANES)), small((nu, 1)), small((nu, 1))],
        out_specs=[unit(gs, GDN_DV), unit(2 * gs, GDN_DK), unit(gs, GDN_DK), unit(gs, GDN_CHUNK),
                   pl.BlockSpec((1, 1, nu * GDN_GC, LANES), lambda b, i: (b, i, 0, 0))],
        out_shape=[jax.ShapeDtypeStruct((bsz, nu, t, GDN_DV), F32),
                   jax.ShapeDtypeStruct((bsz, nu, 2 * t, GDN_DK), BF16),
                   jax.ShapeDtypeStruct((bsz, nu, t, GDN_DK), BF16),
                   jax.ShapeDtypeStruct((bsz, nu, t, GDN_CHUNK), BF16),
                   jax.ShapeDtypeStruct((bsz, ng, nu * GDN_GC, LANES), F32)],
        compiler_params=_cparams(("parallel", "parallel")),
        name="gdnchunk",
    )(qkv, proj, abt, alr, dtr, alc, dtc)


GDN_SB = 4
GDN_ST = GDN_SB * GDN_CHUNK


def _gdnscan_kernel(uf, ub, wqf, wqb, kdf, kdb, inf, inb, glf, glb, of_ref, ob_ref, s_ref):
    i = pl.program_id(0)

    @pl.when(i == 0)
    def _():
        s_ref[...] = jnp.zeros_like(s_ref)

    bsz = uf.shape[0]
    c, nh = GDN_CHUNK, GDN_HEADS
    chains = [(b, d, h) for b in range(bsz) for d in range(2) for h in range(nh)]
    for step in range(GDN_SB):
        zs, vns = [], []
        for b, d, h in chains:
            cc = step if d == 0 else GDN_SB - 1 - step
            wq = (wqf, wqb)[d]
            st = s_ref[(b * 2 + d) * nh + h]
            zs.append(_mm(wq[b, h, 2 * cc * c:(2 * cc + 2) * c, :], st.astype(BF16)))
        for (b, d, h), z in zip(chains, zs):
            cc = step if d == 0 else GDN_SB - 1 - step
            u = (uf, ub)[d]
            vns.append((u[b, h, cc * c:(cc + 1) * c, :] - z[:c]).astype(BF16))
        for (b, d, h), z, vn in zip(chains, zs, vns):
            cc = step if d == 0 else GDN_SB - 1 - step
            rows = slice(cc * c, (cc + 1) * c)
            intra = (inf, inb)[d]
            o_ref = (of_ref, ob_ref)[d]
            o_ref[b, rows, h * GDN_DV:(h + 1) * GDN_DV] = z[c:] + _mm(intra[b, h, rows, :], vn)
        for (b, d, h), vn in zip(chains, vns):
            cc = step if d == 0 else GDN_SB - 1 - step
            rows = slice(cc * c, (cc + 1) * c)
            kd = (kdf, kdb)[d]
            gl = (glf, glb)[d]
            r = (d * nh + h) * GDN_GC + cc % GDN_GC
            sidx = (b * 2 + d) * nh + h
            s_ref[sidx] = s_ref[sidx] * gl[b, cc // GDN_GC, r:r + 1, :] + _tn(kd[b, h, rows, :], vn)


def _gdnscan(u, wq, kd, intra, gl):
    bsz, nu, t, _ = u.shape
    nh = GDN_HEADS
    st = GDN_ST
    nb = t // st
    ngs = st // GDN_GS

    def unit(rows, w, d):
        if d == 0:
            return pl.BlockSpec((bsz, nh, rows, w), lambda i: (0, 0, i, 0))
        return pl.BlockSpec((bsz, nh, rows, w), lambda i: (0, 1, nb - 1 - i, 0))

    glspec = lambda d: pl.BlockSpec((bsz, ngs, nu * GDN_GC, LANES),
                                    (lambda i: (0, i, 0, 0)) if d == 0 else (lambda i: (0, nb - 1 - i, 0, 0)))
    return pl.pallas_call(
        _gdnscan_kernel,
        grid=(nb,),
        in_specs=[unit(st, GDN_DV, 0), unit(st, GDN_DV, 1), unit(2 * st, GDN_DK, 0), unit(2 * st, GDN_DK, 1),
                  unit(st, GDN_DK, 0), unit(st, GDN_DK, 1), unit(st, GDN_CHUNK, 0), unit(st, GDN_CHUNK, 1),
                  glspec(0), glspec(1)],
        out_specs=[pl.BlockSpec((bsz, st, GDN_W), lambda i: (0, i, 0)),
                   pl.BlockSpec((bsz, st, GDN_W), lambda i: (0, nb - 1 - i, 0))],
        out_shape=[jax.ShapeDtypeStruct((bsz, t, GDN_W), F32), jax.ShapeDtypeStruct((bsz, t, GDN_W), F32)],
        scratch_shapes=[pltpu.VMEM((bsz * nu, GDN_DK, GDN_DV), F32)],
        compiler_params=_cparams(("arbitrary",)),
        name="gdnscan",
    )(u, u, wq, wq, kd, kd, intra, intra, gl, gl)


def _outproj_kernel(x_ref, yc_ref, yn_ref, of_ref, ob_ref, z_ref, gnw_ref, wo_ref, g1_ref, nw_ref, sc_ref, sh_ref,
                    wr_ref, br_ref, xo_ref, h_ref, e_ref, g_ref, hist_ref):
    o = of_ref[0] + ob_ref[0]
    z = z_ref[0]
    parts = [yc_ref[0], yn_ref[0]]
    for h in range(GDN_HEADS):
        sl = slice(h * GDN_DV, (h + 1) * GDN_DV)
        oh = o[:, sl]
        oh = oh * lax.rsqrt(jnp.mean(oh * oh, axis=-1, keepdims=True) + EPS) * gnw_ref[...]
        parts.append(oh * _silu(z[:, sl]))
    mixed = jnp.dot(jnp.concatenate(parts, axis=-1).astype(BF16), wo_ref[...], preferred_element_type=F32)
    xn = x_ref[0] + g1_ref[0] * mixed
    xo_ref[0] = xn
    hf = _modnorm(xn, nw_ref[...], sc_ref[0], sh_ref[0])
    h_ref[0] = hf
    logits = jnp.dot(hf, wr_ref[...], preferred_element_type=F32, precision=lax.Precision.HIGHEST) + br_ref[...]
    lane = lax.broadcasted_iota(jnp.int32, logits.shape, 1)
    gl = jnp.where(lane < N_GROUPS, logits, NEG)
    gm = jnp.max(gl, axis=-1, keepdims=True)
    den = jnp.sum(jnp.exp(gl - gm), axis=-1, keepdims=True)
    grp = jnp.min(jnp.where(gl == gm, lane, LANES), axis=-1, keepdims=True)
    pg_top = 1.0 / den
    ex = lane - N_GROUPS
    in_grp = (ex >= grp * EXPERTS_PER_GROUP) & (ex < (grp + 1) * EXPERTS_PER_GROUP)
    el = jnp.where(in_grp, logits, NEG)
    m1 = jnp.max(el, axis=-1, keepdims=True)
    i1 = jnp.min(jnp.where(el == m1, lane, LANES), axis=-1, keepdims=True)
    el2 = jnp.where(lane == i1, NEG, el)
    m2 = jnp.max(el2, axis=-1, keepdims=True)
    i2 = jnp.min(jnp.where(el2 == m2, lane, LANES), axis=-1, keepdims=True)
    e2 = jnp.exp(m2 - m1)
    w1 = pg_top / (1.0 + e2)
    w2 = pg_top * e2 / (1.0 + e2)
    g_ref[0] = jnp.where(lane == 0, w1, jnp.where(lane == 1, w2, 0.0))
    oh1 = (lane == i1 - N_GROUPS).astype(F32)
    oh2 = (lane == i2 - N_GROUPS).astype(F32)
    both = oh1 + oh2
    tm = logits.shape[0]
    earlier = (lax.broadcasted_iota(jnp.int32, (tm, tm), 0) > lax.broadcasted_iota(jnp.int32, (tm, tm), 1))
    cnt = jnp.dot(earlier.astype(BF16), both.astype(BF16), preferred_element_type=F32)
    r1 = jnp.sum(cnt * oh1, axis=-1, keepdims=True).astype(jnp.int32)
    r2 = jnp.sum(cnt * oh2, axis=-1, keepdims=True).astype(jnp.int32)
    e_ref[0] = jnp.where(lane == 0, i1 - N_GROUPS, jnp.where(lane == 1, i2 - N_GROUPS,
                         jnp.where(lane == 2, r1, jnp.where(lane == 3, r2, 0))))
    hist_ref[0, 0] = jnp.broadcast_to(jnp.sum(both, axis=0, keepdims=True), (SUBLANES, LANES))


def _outproj(x, yc, yn, of, ob, proj, gnw, wo, g1, nw, sc, sh, wr, br):
    bsz, t, d = x.shape
    tm = min(ROW_TILE, t)
    tok = lambda w: pl.BlockSpec((1, tm, w), lambda b, i: (b, i, 0))
    perb = pl.BlockSpec((1, 1, d), lambda b, i: (b, 0, 0))
    full = lambda shape: pl.BlockSpec(shape, lambda b, i: (0, 0))
    return pl.pallas_call(
        _outproj_kernel,
        grid=(bsz, t // tm),
        in_specs=[tok(d), tok(CONV_W), tok(NA_W), tok(GDN_W), tok(GDN_W),
                  pl.BlockSpec((1, tm, GDN_W), lambda b, i: (b, i, COL_GZ // GDN_W)),
                  full((1, GDN_DV)), full((d, d)), perb, full((1, d)), perb, perb,
                  full((d, LANES)), full((1, LANES))],
        out_specs=[tok(d), tok(d), tok(LANES), tok(LANES),
                   pl.BlockSpec((1, 1, SUBLANES, LANES), lambda b, i: (b, i, 0, 0))],
        out_shape=[jax.ShapeDtypeStruct((bsz, t, d), F32), jax.ShapeDtypeStruct((bsz, t, d), F32),
                   jax.ShapeDtypeStruct((bsz, t, LANES), jnp.int32), jax.ShapeDtypeStruct((bsz, t, LANES), F32),
                   jax.ShapeDtypeStruct((bsz, t // tm, SUBLANES, LANES), F32)],
        compiler_params=_cparams(("parallel", "parallel")),
        name="outproj",
    )(x, yc, yn, of, ob, proj, gnw, wo, g1, nw, sc, sh, wr, br)


MOE_TM = 256


def _dispatch_kernel(pos_ref, h_ref, xs_in, xs_out, sem):
    del xs_in
    tm = h_ref.shape[0]

    def row_copy(r, k):
        dst = pos_ref[0, 0, k * tm + r]
        return pltpu.make_async_copy(h_ref.at[pl.ds(r, 1), :], xs_out.at[pl.ds(dst, 1), :], sem)

    def body(r, carry):
        for k in range(TOP_K):
            row_copy(r, k).start()
        return carry

    lax.fori_loop(0, tm, body, 0, unroll=8)
    for k in range(TOP_K):
        pltpu.make_async_copy(h_ref, xs_out.at[pl.ds(0, tm), :], sem).wait()


def _dispatch(hf, pos3, padded_rows):
    n, d = hf.shape
    tm = MOE_TM
    nt = n // tm
    zeros = jnp.zeros((padded_rows, d), F32)
    return pl.pallas_call(
        _dispatch_kernel,
        grid=(nt,),
        in_specs=[pl.BlockSpec((1, 1, TOP_K * tm), lambda i: (i, 0, 0), memory_space=pltpu.SMEM),
                  pl.BlockSpec((tm, d), lambda i: (i, 0)),
                  pl.BlockSpec(memory_space=pl.ANY)],
        out_specs=pl.BlockSpec(memory_space=pl.ANY),
        out_shape=jax.ShapeDtypeStruct((padded_rows, d), F32),
        scratch_shapes=[pltpu.SemaphoreType.DMA(())],
        input_output_aliases={2: 0},
        compiler_params=_cparams(("arbitrary",)),
        name="dispatch",
    )(pos3, hf, zeros)


def _experts_kernel(be_ref, nu_ref, x_ref, w1_ref, w3_ref, w2_ref, o_ref, w1b, w3b, w2b):
    i = pl.program_id(0)
    used = nu_ref[0]

    @pl.when((i == 0) | (be_ref[i] != be_ref[jnp.maximum(i - 1, 0)]))
    def _():
        w1b[...] = w1_ref[0].astype(BF16)
        w3b[...] = w3_ref[0].astype(BF16)
        w2b[...] = w2_ref[0].astype(BF16)

    @pl.when(i < used)
    def _():
        xb = x_ref[...].astype(BF16)
        h1 = jnp.dot(xb, w1b[...], preferred_element_type=F32)
        h3 = jnp.dot(xb, w3b[...], preferred_element_type=F32)
        act = (_silu(h1) * h3).astype(BF16)
        o_ref[...] = jnp.dot(act, w2b[...], preferred_element_type=F32)

    @pl.when(i >= used)
    def _():
        o_ref[...] = jnp.zeros_like(o_ref)


def _experts(xs, block_e, n_used, w1, w3, w2):
    padded_rows, d = xs.shape
    bm = MOE_BM
    nblk = padded_rows // bm
    de = w1.shape[-1]
    grid_spec = pltpu.PrefetchScalarGridSpec(
        num_scalar_prefetch=2,
        grid=(nblk,),
        in_specs=[pl.BlockSpec((bm, d), lambda i, be, nu: (jnp.minimum(i, nu[0] - 1), 0)),
                  pl.BlockSpec((1, d, de), lambda i, be, nu: (be[i], 0, 0)),
                  pl.BlockSpec((1, d, de), lambda i, be, nu: (be[i], 0, 0)),
                  pl.BlockSpec((1, de, d), lambda i, be, nu: (be[i], 0, 0))],
        out_specs=pl.BlockSpec((bm, d), lambda i, be, nu: (i, 0)),
        scratch_shapes=[pltpu.VMEM((d, de), BF16), pltpu.VMEM((d, de), BF16), pltpu.VMEM((de, d), BF16)])
    return pl.pallas_call(
        _experts_kernel,
        grid_spec=grid_spec,
        out_shape=jax.ShapeDtypeStruct((padded_rows, d), F32),
        compiler_params=_cparams(("arbitrary",)),
        name="experts",
    )(block_e, n_used, xs, w1, w3, w2)


def _combine_kernel(pc_ref, pn_ref, x_ref, gt_ref, g2_ref, fw_ref, ys_hbm, o_ref, ybuf, sem, *, final):
    i = pl.program_id(0)
    nt = pl.num_programs(0)
    tm = x_ref.shape[0]

    def start_gather(idx_ref, slot):
        def body(r, carry):
            pltpu.make_async_copy(ys_hbm.at[pl.ds(idx_ref[0, 0, r], 1), :], ybuf.at[slot, pl.ds(r, 1), :],
                                  sem.at[slot]).start()
            return carry
        lax.fori_loop(0, TOP_K * tm, body, 0, unroll=8)

    slot = i % 2

    @pl.when(i == 0)
    def _():
        start_gather(pc_ref, 0)

    @pl.when(i + 1 < nt)
    def _():
        start_gather(pn_ref, 1 - slot)

    pltpu.make_async_copy(ys_hbm.at[pl.ds(0, TOP_K * tm), :], ybuf.at[slot], sem.at[slot]).wait()
    gt = gt_ref[...]
    y = gt[:, 0:1] * ybuf[slot, 0:tm, :] + gt[:, 1:2] * ybuf[slot, tm:2 * tm, :]
    xn = x_ref[...] + g2_ref[0] * y
    if final:
        ms = jnp.mean(xn * xn, axis=-1, keepdims=True)
        xn = xn * lax.rsqrt(ms + EPS) * fw_ref[...]
    o_ref[...] = xn


def _combine(x2, gates, g2, fw, ys, pos3, t, final):
    n, d = x2.shape
    tm = MOE_TM
    nt = n // tm
    per_b = t // tm
    return pl.pallas_call(
        functools.partial(_combine_kernel, final=final),
        grid=(nt,),
        in_specs=[pl.BlockSpec((1, 1, TOP_K * tm), lambda i: (i, 0, 0), memory_space=pltpu.SMEM),
                  pl.BlockSpec((1, 1, TOP_K * tm), lambda i: (jnp.minimum(i + 1, nt - 1), 0, 0),
                               memory_space=pltpu.SMEM),
                  pl.BlockSpec((tm, d), lambda i: (i, 0)),
                  pl.BlockSpec((tm, LANES), lambda i: (i, 0)),
                  pl.BlockSpec((1, 1, d), lambda i: (i // per_b, 0, 0)),
                  pl.BlockSpec((1, d), lambda i: (0, 0)),
                  pl.BlockSpec(memory_space=pl.ANY)],
        out_specs=pl.BlockSpec((tm, d), lambda i: (i, 0)),
        out_shape=jax.ShapeDtypeStruct((n, d), F32),
        scratch_shapes=[pltpu.VMEM((2, TOP_K * tm, d), F32), pltpu.SemaphoreType.DMA((2,))],
        compiler_params=_cparams(("arbitrary",)),
        name="combine",
    )(pos3, pos3, x2, gates, g2, fw, ys)


def _moe_plan(eidx, hist, rank_tile):
    n = eidx.shape[0]
    bm = MOE_BM
    hist = hist[:, 0, :N_EXPERTS].astype(jnp.int32)
    sizes = jnp.sum(hist, axis=0)
    base = jnp.cumsum(hist, axis=0) - hist
    padded = (sizes + bm - 1) // bm * bm
    pad_end = jnp.cumsum(padded)
    tbl = (pad_end - padded)[None, :] + base
    e = eidx[:, :TOP_K].reshape(-1, rank_tile, TOP_K)
    onehot = e[..., None] == jnp.arange(N_EXPERTS, dtype=jnp.int32)
    pos = jnp.sum(jnp.where(onehot, tbl[:, None, None, :], 0), axis=-1).reshape(n, TOP_K) + eidx[:, TOP_K:2 * TOP_K]
    padded_rows = (n * TOP_K + N_EXPERTS * (bm - 1) + bm - 1) // bm * bm
    n_blocks = padded_rows // bm
    blk_start = jnp.arange(n_blocks, dtype=jnp.int32) * bm
    block_e = jnp.minimum(jnp.sum(pad_end[None, :] <= blk_start[:, None], axis=1), N_EXPERTS - 1).astype(jnp.int32)
    n_used = (pad_end[-1] // bm).astype(jnp.int32).reshape(1)
    nt = n // MOE_TM
    pos3 = jnp.transpose(pos.astype(jnp.int32).reshape(nt, MOE_TM, TOP_K), (0, 2, 1)).reshape(nt, 1, TOP_K * MOE_TM)
    return pos3, block_e, n_used, padded_rows


def kernel(x, c, norm_mix_w, norm_ffn_w, w_ada, b_ada, w_in, conv_a_w, na_rpb, gdn_conv_w, gdn_a_log, gdn_dt_bias,
           gdn_norm_w, w_out, router_group_w, router_group_b, router_expert_w, router_expert_b, expert_w1,
           expert_w3, expert_w2, final_norm_w):
    bsz, t, d = x.shape
    depth = w_ada.shape[0]
    n = bsz * t
    mod = _ada(c, w_ada, b_ada)
    w_in_pad = jnp.pad(w_in.astype(BF16), ((0, 0), (0, 0), (0, D_IN_PAD - D_IN)))
    wabt = jnp.transpose(w_in[:, :, COL_AB:], (0, 2, 1)).astype(BF16)
    w_out_b = w_out.astype(BF16)
    wr = jnp.pad(jnp.concatenate([router_group_w, router_expert_w], axis=-1),
                 ((0, 0), (0, 0), (0, LANES - N_GROUPS - N_EXPERTS)))
    br = jnp.pad(jnp.concatenate([router_group_b, router_expert_b], axis=-1),
                 ((0, 0), (0, LANES - N_GROUPS - N_EXPERTS)))
    fw = final_norm_w.reshape(1, d)
    for l in range(depth):
        sh1, sc1, g1, sh2, sc2, g2 = [mod[l, :, j * d:(j + 1) * d].reshape(bsz, 1, d) for j in range(6)]
        proj, abt = _inproj(x, norm_mix_w[l].reshape(1, d), sc1, sh1, w_in_pad[l], wabt[l])
        y_conv = _convmix(proj, conv_a_w[l])
        y_na = _na(proj, _na_bias_table(na_rpb[l]))
        qkv = _gdnprep(proj, gdn_conv_w[l])
        o_f, o_b = _gdnscan(*_gdnchunk(qkv, proj, abt, gdn_a_log[l], gdn_dt_bias[l]))
        x, hf, eidx, gates, hist = _outproj(x, y_conv, y_na, o_f, o_b, proj, gdn_norm_w[l].reshape(1, GDN_DV),
                                            w_out_b[l], g1, norm_ffn_w[l].reshape(1, d), sc2, sh2, wr[l],
                                            br[l].reshape(1, LANES))
        pos3, block_e, n_used, padded_rows = _moe_plan(eidx.reshape(n, LANES), hist.reshape(-1, SUBLANES, LANES),
                                                       min(ROW_TILE, t))
        xs = _dispatch(hf.reshape(n, d), pos3, padded_rows)
        ys = _experts(xs, block_e, n_used, expert_w1[l], expert_w3[l], expert_w2[l])
        x = _combine(x.reshape(n, d), gates.reshape(n, LANES), g2, fw, ys, pos3, t,
                     final=(l == depth - 1)).reshape(bsz, t, d)
    return x
```

```python
import functools

import jax
import jax.numpy as jnp
from jax import lax
from jax.experimental import pallas as pl
from jax.experimental.pallas import tpu as pltpu

F32 = jnp.float32
BF16 = jnp.bfloat16

EPS = 1e-6
GRID_W = 64
CONV_W = 256
NA_HEADS = 4
NA_DH = 64
NA_W = NA_HEADS * NA_DH
NA_KH = 8
NA_KW = 16
GDN_HEADS = 4
GDN_DK = 128
GDN_DV = 128
GDN_W = GDN_HEADS * GDN_DV
GDN_QKV = 2 * GDN_HEADS * GDN_DK + GDN_W
GDN_CHUNK = 64
N_GROUPS = 4
EXPERTS_PER_GROUP = 8
N_EXPERTS = N_GROUPS * EXPERTS_PER_GROUP
TOP_K = 2

COL_CONV = 0
COL_NA = 3 * CONV_W
COL_GQKV = COL_NA + 3 * NA_W
COL_GZ = COL_GQKV + GDN_QKV
COL_AB = COL_GZ + GDN_W
D_IN = COL_AB + 4 * GDN_HEADS
D_IN_PAD = COL_AB + 128
LANES = 128
SUBLANES = 8

NEG = -1e30
VMEM_LIMIT = 56 * 1024 * 1024

MOE_BM = 256
ROW_TILE = 512


def _cparams(sem):
    return pltpu.CompilerParams(dimension_semantics=sem, vmem_limit_bytes=VMEM_LIMIT)


def _silu(x):
    return x * (1.0 / (1.0 + jnp.exp(-x)))


def _ada_kernel(ct_ref, w_ref, b_ref, o_ref):
    w = w_ref[0]
    nb = ct_ref.shape[1]
    for r in range(nb):
        col = _silu(ct_ref[:, r:r + 1])
        o_ref[0, r:r + 1, :] = jnp.sum(w * col, axis=0, keepdims=True) + b_ref[0]


def _ada(c, w_ada, b_ada):
    depth, d, n6 = w_ada.shape
    bsz = c.shape[0]
    tn = 512
    return pl.pallas_call(
        _ada_kernel,
        grid=(depth, n6 // tn),
        in_specs=[pl.BlockSpec((d, bsz), lambda l, j: (0, 0)),
                  pl.BlockSpec((1, d, tn), lambda l, j: (l, 0, j)),
                  pl.BlockSpec((1, 1, tn), lambda l, j: (l, 0, j))],
        out_specs=pl.BlockSpec((1, bsz, tn), lambda l, j: (l, 0, j)),
        out_shape=jax.ShapeDtypeStruct((depth, bsz, n6), F32),
        compiler_params=_cparams(("parallel", "parallel")),
        name="ada",
    )(c.T, w_ada, b_ada.reshape(depth, 1, n6))


def _modnorm(x, nw, sc, sh):
    ms = jnp.mean(x * x, axis=-1, keepdims=True)
    y = x * lax.rsqrt(ms + EPS)
    return (y * nw) * (1.0 + sc) + sh


def _inproj_kernel(x_ref, nw_ref, sc_ref, sh_ref, w_ref, wabt_ref, o_ref, ot_ref):
    h = _modnorm(x_ref[0], nw_ref[...], sc_ref[0], sh_ref[0]).astype(BF16)
    o_ref[0] = jnp.dot(h, w_ref[0], preferred_element_type=F32)
    ot_ref[0] = lax.dot_general(wabt_ref[0], h, (((1,), (1,)), ((), ())), preferred_element_type=F32)


def _inproj(x, nw, sc, sh, w_pad, wabt, layer):
    bsz, t, d = x.shape
    tm = min(ROW_TILE, t)
    nab = wabt.shape[1]
    return pl.pallas_call(
        _inproj_kernel,
        grid=(bsz, t // tm),
        in_specs=[pl.BlockSpec((1, tm, d), lambda b, i: (b, i, 0)),
                  pl.BlockSpec((1, d), lambda b, i: (0, 0)),
                  pl.BlockSpec((1, 1, d), lambda b, i: (b, 0, 0)),
                  pl.BlockSpec((1, 1, d), lambda b, i: (b, 0, 0)),
                  pl.BlockSpec((1, d, D_IN_PAD), lambda b, i: (layer, 0, 0)),
                  pl.BlockSpec((1, nab, d), lambda b, i: (layer, 0, 0))],
        out_specs=[pl.BlockSpec((1, tm, D_IN_PAD), lambda b, i: (b, i, 0)),
                   pl.BlockSpec((1, nab, tm), lambda b, i: (b, 0, i))],
        out_shape=[jax.ShapeDtypeStruct((bsz, t, D_IN_PAD), F32),
                   jax.ShapeDtypeStruct((bsz, nab, t), F32)],
        compiler_params=_cparams(("parallel", "parallel")),
        name="inproj",
    )(x, nw, sc, sh, w_pad, wabt)


def _dwconv3(u, prev_row, next_row, w_ref):
    tt = u.shape[0]
    row = lax.broadcasted_iota(jnp.int32, u.shape, 0)
    dn = jnp.where(row == 0, prev_row, pltpu.roll(u, 1, axis=0))
    up = jnp.where(row == tt - 1, next_row, pltpu.roll(u, tt - 1, axis=0))
    return w_ref[0:1, :] * dn + w_ref[1:2, :] * u + w_ref[2:3, :] * up


def _halo_specs(tt, t, width, colblk):
    nsub = tt // SUBLANES
    last = t // SUBLANES - 1
    return [pl.BlockSpec((1, tt, width), lambda b, i: (b, i, colblk)),
            pl.BlockSpec((1, SUBLANES, width), lambda b, i: (b, jnp.maximum(i * nsub - 1, 0), colblk)),
            pl.BlockSpec((1, SUBLANES, width), lambda b, i: (b, jnp.minimum((i + 1) * nsub, last), colblk))]


def _convmix_kernel(m_ref, p_ref, n_ref, w_ref, o_ref):
    i = pl.program_id(1)
    nt = pl.num_programs(1)
    m = m_ref[0]
    u = m[:, CONV_W:2 * CONV_W] * m[:, 2 * CONV_W:]
    p = p_ref[0]
    n = n_ref[0]
    pu = p[SUBLANES - 1:SUBLANES, CONV_W:2 * CONV_W] * p[SUBLANES - 1:SUBLANES, 2 * CONV_W:]
    nu = n[0:1, CONV_W:2 * CONV_W] * n[0:1, 2 * CONV_W:]
    pu = jnp.where(i == 0, 0.0, pu)
    nu = jnp.where(i == nt - 1, 0.0, nu)
    o_ref[0] = m[:, :CONV_W] * _dwconv3(u, pu, nu, w_ref)


def _convmix(proj, conv_w):
    bsz, t, _ = proj.shape
    tt = min(ROW_TILE, t)
    width = 3 * CONV_W
    return pl.pallas_call(
        _convmix_kernel,
        grid=(bsz, t // tt),
        in_specs=_halo_specs(tt, t, width, COL_CONV // width) + [pl.BlockSpec((3, CONV_W), lambda b, i: (0, 0))],
        out_specs=pl.BlockSpec((1, tt, CONV_W), lambda b, i: (b, i, 0)),
        out_shape=jax.ShapeDtypeStruct((bsz, t, CONV_W), F32),
        compiler_params=_cparams(("parallel", "parallel")),
        name="convmix",
    )(proj, proj, proj, conv_w)


NA_RB = NA_KH // 2
NA_TOK = NA_RB * GRID_W
NA_KEYS = 3 * NA_TOK


def _na_bias_table(rpb):
    col = jnp.arange(GRID_W)
    cstart = jnp.clip(col - NA_KW // 2, 0, GRID_W - NA_KW)
    kc = jnp.arange(GRID_W)
    valid = (kc[None, :] >= cstart[:, None]) & (kc[None, :] < cstart[:, None] + NA_KW)
    dc = kc[None, :] - col[:, None] + (NA_KW - 1)
    onehot = (dc[None] == jnp.arange(2 * NA_KW - 1)[:, None, None]) & valid[None]
    cols = jnp.einsum('lhrd,dck->lhrck', rpb, onehot.astype(F32), precision=lax.Precision.HIGHEST)
    cols = jnp.where(valid, cols, NEG)
    lo = NA_KH - 1 - NA_RB
    blk = jnp.stack([cols[:, :, lo - j:lo - j + 3 * NA_RB] for j in range(NA_RB)], axis=2)
    blk = jnp.transpose(blk, (0, 1, 2, 4, 3, 5))
    return blk.reshape(rpb.shape[0], NA_HEADS, NA_TOK, NA_KEYS)


def _na_kernel(q_ref, kp_ref, kc_ref, kn_ref, vp_ref, vc_ref, vn_ref, bias_ref, o_ref, *, rows):
    i = pl.program_id(1)
    kbuf = jnp.concatenate([kp_ref[0], kc_ref[0], kn_ref[0]], axis=0).astype(BF16)
    vbuf = jnp.concatenate([vp_ref[0], vc_ref[0], vn_ref[0]], axis=0).astype(BF16)
    q = q_ref[0] * (NA_DH ** -0.5)
    head_of_lane = lax.broadcasted_iota(jnp.int32, (1, NA_W), 1) // NA_DH
    qrow = i * NA_RB + lax.broadcasted_iota(jnp.int32, (NA_TOK, NA_KEYS), 0) // GRID_W
    krow = (i - 1) * NA_RB + lax.broadcasted_iota(jnp.int32, (NA_TOK, NA_KEYS), 1) // GRID_W
    rs = jnp.clip(qrow - NA_KH // 2, 0, rows - NA_KH)
    row_mask = jnp.where((krow >= rs) & (krow < rs + NA_KH), 0.0, NEG)
    acc = jnp.zeros((NA_TOK, NA_W), F32)
    for h in range(NA_HEADS):
        mine = head_of_lane == h
        s = _nt(jnp.where(mine, q, 0.0).astype(BF16), kbuf) + (bias_ref[0, h] + row_mask)
        m = jnp.max(s, axis=-1, keepdims=True)
        p = jnp.exp(s - m)
        l = jnp.sum(p, axis=-1, keepdims=True)
        o = jnp.dot((p / l).astype(BF16), vbuf, preferred_element_type=F32)
        acc = acc + jnp.where(mine, o, 0.0)
    o_ref[0] = acc


def _na(proj, bias_tbl, layer):
    bsz, t, _ = proj.shape
    rows = t // GRID_W
    nblk = rows // NA_RB
    qc, kc, vc = COL_NA // NA_W, COL_NA // NA_W + 1, COL_NA // NA_W + 2

    def spec(col, shift):
        return pl.BlockSpec((1, NA_TOK, NA_W), lambda b, i: (b, jnp.clip(i + shift, 0, nblk - 1), col))

    return pl.pallas_call(
        functools.partial(_na_kernel, rows=rows),
        grid=(bsz, nblk),
        in_specs=[spec(qc, 0), spec(kc, -1), spec(kc, 0), spec(kc, 1), spec(vc, -1), spec(vc, 0), spec(vc, 1),
                  pl.BlockSpec((1, NA_HEADS, NA_TOK, NA_KEYS), lambda b, i: (layer, 0, 0, 0))],
        out_specs=pl.BlockSpec((1, NA_TOK, NA_W), lambda b, i: (b, i, 0)),
        out_shape=jax.ShapeDtypeStruct((bsz, t, NA_W), F32),
        compiler_params=_cparams(("parallel", "parallel")),
        name="na",
    )(proj, proj, proj, proj, proj, proj, proj, bias_tbl)


def _gdnprep_kernel(m_ref, p_ref, n_ref, w_ref, o_ref):
    i = pl.program_id(1)
    nt = pl.num_programs(1)
    pu = jnp.where(i == 0, 0.0, p_ref[0, SUBLANES - 1:SUBLANES, :])
    nu = jnp.where(i == nt - 1, 0.0, n_ref[0, 0:1, :])
    c = _silu(_dwconv3(m_ref[0], pu, nu, w_ref))
    qk_heads = 2 * GDN_HEADS
    for hh in range(GDN_QKV // GDN_DK):
        sl = slice(hh * GDN_DK, (hh + 1) * GDN_DK)
        xh = c[:, sl]
        if hh < qk_heads:
            xh = xh * lax.rsqrt(jnp.sum(xh * xh, axis=-1, keepdims=True) + EPS)
            if hh < GDN_HEADS:
                xh = xh * (GDN_DK ** -0.5)
        o_ref[0, :, sl] = xh


def _gdnprep(proj, conv_w):
    bsz, t, _ = proj.shape
    tt = min(256, t)
    return pl.pallas_call(
        _gdnprep_kernel,
        grid=(bsz, t // tt),
        in_specs=_halo_specs(tt, t, GDN_QKV, COL_GQKV // GDN_QKV) + [pl.BlockSpec((3, GDN_QKV), lambda b, i: (0, 0))],
        out_specs=pl.BlockSpec((1, tt, GDN_QKV), lambda b, i: (b, i, 0)),
        out_shape=jax.ShapeDtypeStruct((bsz, t, GDN_QKV), F32),
        compiler_params=_cparams(("parallel", "parallel")),
        name="gdnprep",
    )(proj, proj, proj, conv_w)


GDN_GS = 128
GDN_UNITS = 2 * GDN_HEADS
GDN_GC = GDN_GS // GDN_CHUNK


def _softplus(x):
    return jnp.maximum(x, 0.0) + jnp.log1p(jnp.exp(-jnp.abs(x)))


def _seg_cumsum(x, axis, reverse):
    n = x.shape[axis]
    pos = lax.broadcasted_iota(jnp.int32, x.shape, axis) & (GDN_CHUNK - 1)
    s = 1
    while s < GDN_CHUNK:
        if reverse:
            x = x + jnp.where(pos < GDN_CHUNK - s, pltpu.roll(x, n - s, axis=axis), 0.0)
        else:
            x = x + jnp.where(pos >= s, pltpu.roll(x, s, axis=axis), 0.0)
        s *= 2
    return x


def _nt(a, b):
    return lax.dot_general(a, b, (((1,), (1,)), ((), ())), preferred_element_type=F32)


def _tn(a, b):
    return lax.dot_general(a, b, (((0,), (0,)), ((), ())), preferred_element_type=F32)


def _mm(a, b):
    return jnp.dot(a, b, preferred_element_type=F32)


def _gdnchunk_kernel(q_ref, ab_ref, abt_ref, alr_ref, dtr_ref, alc_ref, dtc_ref,
                     u_ref, wq_ref, kd_ref, in_ref, gl_ref):
    gs, c, nh = GDN_GS, GDN_CHUNK, GDN_HEADS
    ri = lax.broadcasted_iota(jnp.int32, (gs, gs), 0)
    ci = lax.broadcasted_iota(jnp.int32, (gs, gs), 1)
    same = (ri // c) == (ci // c)
    eye = (ri == ci).astype(F32)
    rowc = lax.broadcasted_iota(jnp.int32, (gs, 1), 0) // c
    ab = ab_ref[0]
    abt = abt_ref[0]
    graw_c = -jnp.exp(alr_ref[...]) * _softplus(ab + dtr_ref[...])
    graw_r = -jnp.exp(alc_ref[...]) * _softplus(abt[0:2 * nh] + dtc_ref[...])
    beta_c = 1.0 / (1.0 + jnp.exp(-ab))
    g_col = [_seg_cumsum(graw_c, 0, False), _seg_cumsum(graw_c, 0, True)]
    g_row = [_seg_cumsum(graw_r, 1, False), _seg_cumsum(graw_r, 1, True)]
    incl = [same & (ri >= ci), same & (ri <= ci)]
    strict = [same & (ri > ci), same & (ri < ci)]

    qs, ks, vs, grams = [], [], [], []
    for h in range(nh):
        q = q_ref[0, :, h * GDN_DK:(h + 1) * GDN_DK]
        k = q_ref[0, :, (nh + h) * GDN_DK:(nh + h + 1) * GDN_DK]
        v = q_ref[0, :, 2 * nh * GDN_DK + h * GDN_DV:2 * nh * GDN_DK + (h + 1) * GDN_DV]
        qs.append(q), ks.append(k), vs.append(v)
        grams.append(_nt(jnp.concatenate([q, k], axis=0).astype(BF16), k.astype(BF16)))

    units = [(d, h) for d in range(2) for h in range(nh)]
    gcs, bcs, intras, xs, ps = [], [], [], [], []
    for d, h in units:
        col = d * nh + h
        gc = g_col[d][:, col:col + 1]
        gr = g_row[d][col:col + 1, :]
        bc = beta_c[:, 2 * nh + col:2 * nh + col + 1]
        e_incl = jnp.exp(jnp.where(incl[d], gc - gr, NEG))
        a = grams[h][gs:] * bc * jnp.where(strict[d], e_incl, 0.0)
        gcs.append(gc), bcs.append(bc)
        intras.append(grams[h][:gs] * e_incl)
        xs.append(eye - a), ps.append(a)
    for _ in range(5):
        pbs = [p.astype(BF16) for p in ps]
        ps = [_mm(pb, pb) for pb in pbs]
        xs = [x + _mm(x.astype(BF16), p.astype(BF16)) for x, p in zip(xs, ps)]
    egs = [jnp.exp(gc) for gc in gcs]
    sols = [_mm(x.astype(BF16),
                jnp.concatenate([vs[h] * bc, ks[h] * (bc * eg)], axis=1).astype(BF16))
            for (d, h), x, bc, eg in zip(units, xs, bcs, egs)]
    for (d, h), gc, eg, sol, intra in zip(units, gcs, egs, sols, intras):
        col = d * nh + h
        glast_col = jnp.zeros_like(gc)
        for n in range(GDN_GC):
            r = n * c if d == 1 else (n + 1) * c - 1
            glast = gc[r:r + 1, :]
            glast_col = jnp.where(rowc == n, glast, glast_col)
            gl_ref[0, 0, col * GDN_GC + n:col * GDN_GC + n + 1, :] = jnp.broadcast_to(jnp.exp(glast), (1, LANES))
        u_ref[0, col] = sol[:, :GDN_DV]
        w = sol[:, GDN_DV:].astype(BF16)
        qd = (qs[h] * eg).astype(BF16)
        for n in range(GDN_GC):
            wq_ref[0, col, 2 * n * c:(2 * n + 1) * c, :] = w[n * c:(n + 1) * c]
            wq_ref[0, col, (2 * n + 1) * c:(2 * n + 2) * c, :] = qd[n * c:(n + 1) * c]
        kd_ref[0, col] = (ks[h] * jnp.exp(glast_col - gc)).astype(BF16)
        in_ref[0, col] = jnp.concatenate([intra[n * c:(n + 1) * c, n * c:(n + 1) * c] for n in range(GDN_GC)],
                                         axis=0).astype(BF16)


def _gdnchunk(qkv, proj, abt, a_log, dt_bias):
    bsz, t, _ = qkv.shape
    gs, nu = GDN_GS, GDN_UNITS
    ng = t // gs
    nab = abt.shape[1]
    pad = lambda r: jnp.pad(r.reshape(1, -1), ((0, 0), (0, LANES - r.size)))
    alr, dtr = pad(a_log), pad(dt_bias)
    alc, dtc = a_log.reshape(-1, 1), dt_bias.reshape(-1, 1)
    small = lambda shape: pl.BlockSpec(shape, lambda b, i: (0, 0))
    unit = lambda rows, w: pl.BlockSpec((1, nu, rows, w), lambda b, i: (b, 0, i, 0))
    return pl.pallas_call(
        _gdnchunk_kernel,
        grid=(bsz, ng),
        in_specs=[pl.BlockSpec((1, gs, GDN_QKV), lambda b, i: (b, i, 0)),
                  pl.BlockSpec((1, gs, LANES), lambda b, i: (b, i, COL_AB // LANES)),
                  pl.BlockSpec((1, nab, gs), lambda b, i: (b, 0, i)),
                  small((1, LANES)), small((1, LANES)), small((nu, 1)), small((nu, 1))],
        out_specs=[unit(gs, GDN_DV), unit(2 * gs, GDN_DK), unit(gs, GDN_DK), unit(gs, GDN_CHUNK),
                   pl.BlockSpec((1, 1, nu * GDN_GC, LANES), lambda b, i: (b, i, 0, 0))],
        out_shape=[jax.ShapeDtypeStruct((bsz, nu, t, GDN_DV), F32),
                   jax.ShapeDtypeStruct((bsz, nu, 2 * t, GDN_DK), BF16),
                   jax.ShapeDtypeStruct((bsz, nu, t, GDN_DK), BF16),
                   jax.ShapeDtypeStruct((bsz, nu, t, GDN_CHUNK), BF16),
                   jax.ShapeDtypeStruct((bsz, ng, nu * GDN_GC, LANES), F32)],
        compiler_params=_cparams(("parallel", "parallel")),
        name="gdnchunk",
    )(qkv, proj, abt, alr, dtr, alc, dtc)


GDN_SB = 4
GDN_ST = GDN_SB * GDN_CHUNK


def _gdnscan_kernel(uf, ub, wqf, wqb, kdf, kdb, inf, inb, glf, glb, of_ref, ob_ref, s_ref):
    i = pl.program_id(0)

    @pl.when(i == 0)
    def _():
        s_ref[...] = jnp.zeros_like(s_ref)

    bsz = uf.shape[0]
    c, nh = GDN_CHUNK, GDN_HEADS
    chains = [(b, d, h) for b in range(bsz) for d in range(2) for h in range(nh)]
    for step in range(GDN_SB):
        zs, vns = [], []
        for b, d, h in chains:
            cc = step if d == 0 else GDN_SB - 1 - step
            wq = (wqf, wqb)[d]
            st = s_ref[(b * 2 + d) * nh + h]
            zs.append(_mm(wq[b, h, 2 * cc * c:(2 * cc + 2) * c, :], st.astype(BF16)))
        for (b, d, h), z in zip(chains, zs):
            cc = step if d == 0 else GDN_SB - 1 - step
            u = (uf, ub)[d]
            vns.append((u[b, h, cc * c:(cc + 1) * c, :] - z[:c]).astype(BF16))
        for (b, d, h), z, vn in zip(chains, zs, vns):
            cc = step if d == 0 else GDN_SB - 1 - step
            rows = slice(cc * c, (cc + 1) * c)
            intra = (inf, inb)[d]
            o_ref = (of_ref, ob_ref)[d]
            o_ref[b, rows, h * GDN_DV:(h + 1) * GDN_DV] = z[c:] + _mm(intra[b, h, rows, :], vn)
        for (b, d, h), vn in zip(chains, vns):
            cc = step if d == 0 else GDN_SB - 1 - step
            rows = slice(cc * c, (cc + 1) * c)
            kd = (kdf, kdb)[d]
            gl = (glf, glb)[d]
            r = (d * nh + h) * GDN_GC + cc % GDN_GC
            sidx = (b * 2 + d) * nh + h
            s_ref[sidx] = s_ref[sidx] * gl[b, cc // GDN_GC, r:r + 1, :] + _tn(kd[b, h, rows, :], vn)


def _gdnscan(u, wq, kd, intra, gl):
    bsz, nu, t, _ = u.shape
    nh = GDN_HEADS
    st = GDN_ST
    nb = t // st
    ngs = st // GDN_GS

    def unit(rows, w, d):
        if d == 0:
            return pl.BlockSpec((bsz, nh, rows, w), lambda i: (0, 0, i, 0))
        return pl.BlockSpec((bsz, nh, rows, w), lambda i: (0, 1, nb - 1 - i, 0))

    glspec = lambda d: pl.BlockSpec((bsz, ngs, nu * GDN_GC, LANES),
                                    (lambda i: (0, i, 0, 0)) if d == 0 else (lambda i: (0, nb - 1 - i, 0, 0)))
    return pl.pallas_call(
        _gdnscan_kernel,
        grid=(nb,),
        in_specs=[unit(st, GDN_DV, 0), unit(st, GDN_DV, 1), unit(2 * st, GDN_DK, 0), unit(2 * st, GDN_DK, 1),
                  unit(st, GDN_DK, 0), unit(st, GDN_DK, 1), unit(st, GDN_CHUNK, 0), unit(st, GDN_CHUNK, 1),
                  glspec(0), glspec(1)],
        out_specs=[pl.BlockSpec((bsz, st, GDN_W), lambda i: (0, i, 0)),
                   pl.BlockSpec((bsz, st, GDN_W), lambda i: (0, nb - 1 - i, 0))],
        out_shape=[jax.ShapeDtypeStruct((bsz, t, GDN_W), F32), jax.ShapeDtypeStruct((bsz, t, GDN_W), F32)],
        scratch_shapes=[pltpu.VMEM((bsz * nu, GDN_DK, GDN_DV), F32)],
        compiler_params=_cparams(("arbitrary",)),
        name="gdnscan",
    )(u, u, wq, wq, kd, kd, intra, intra, gl, gl)


def _outproj_kernel(x_ref, yc_ref, yn_ref, of_ref, ob_ref, z_ref, gnw_ref, wo_ref, g1_ref, nw_ref, sc_ref, sh_ref,
                    wr_ref, br_ref, xo_ref, h_ref, e_ref, g_ref, hist_ref):
    o = of_ref[0] + ob_ref[0]
    z = z_ref[0]
    parts = [yc_ref[0], yn_ref[0]]
    for h in range(GDN_HEADS):
        sl = slice(h * GDN_DV, (h + 1) * GDN_DV)
        oh = o[:, sl]
        oh = oh * lax.rsqrt(jnp.mean(oh * oh, axis=-1, keepdims=True) + EPS) * gnw_ref[...]
        parts.append(oh * _silu(z[:, sl]))
    mixed = jnp.dot(jnp.concatenate(parts, axis=-1).astype(BF16), wo_ref[0], preferred_element_type=F32)
    xn = x_ref[0] + g1_ref[0] * mixed
    xo_ref[0] = xn
    hf = _modnorm(xn, nw_ref[...], sc_ref[0], sh_ref[0])
    h_ref[0] = hf
    hf_hi = hf.astype(BF16)
    hf_lo = (hf - hf_hi.astype(F32)).astype(BF16)
    logits = (_mm(hf_hi, wr_ref[0]) + (_mm(hf_hi, wr_ref[1]) + _mm(hf_lo, wr_ref[0]))) + br_ref[...]
    lane = lax.broadcasted_iota(jnp.int32, logits.shape, 1)
    gl = jnp.where(lane < N_GROUPS, logits, NEG)
    gm = jnp.max(gl, axis=-1, keepdims=True)
    den = jnp.sum(jnp.exp(gl - gm), axis=-1, keepdims=True)
    grp = jnp.min(jnp.where(gl == gm, lane, LANES), axis=-1, keepdims=True)
    pg_top = 1.0 / den
    ex = lane - N_GROUPS
    in_grp = (ex >= grp * EXPERTS_PER_GROUP) & (ex < (grp + 1) * EXPERTS_PER_GROUP)
    el = jnp.where(in_grp, logits, NEG)
    m1 = jnp.max(el, axis=-1, keepdims=True)
    i1 = jnp.min(jnp.where(el == m1, lane, LANES), axis=-1, keepdims=True)
    el2 = jnp.where(lane == i1, NEG, el)
    m2 = jnp.max(el2, axis=-1, keepdims=True)
    i2 = jnp.min(jnp.where(el2 == m2, lane, LANES), axis=-1, keepdims=True)
    e2 = jnp.exp(m2 - m1)
    w1 = pg_top / (1.0 + e2)
    w2 = pg_top * e2 / (1.0 + e2)
    g_ref[0] = jnp.where(lane == 0, w1, jnp.where(lane == 1, w2, 0.0))
    oh1 = (lane == i1 - N_GROUPS).astype(F32)
    oh2 = (lane == i2 - N_GROUPS).astype(F32)
    both = oh1 + oh2
    tm = logits.shape[0]
    earlier = (lax.broadcasted_iota(jnp.int32, (tm, tm), 0) > lax.broadcasted_iota(jnp.int32, (tm, tm), 1))
    cnt = jnp.dot(earlier.astype(BF16), both.astype(BF16), preferred_element_type=F32)
    r1 = jnp.sum(cnt * oh1, axis=-1, keepdims=True).astype(jnp.int32)
    r2 = jnp.sum(cnt * oh2, axis=-1, keepdims=True).astype(jnp.int32)
    e_ref[0] = jnp.where(lane == 0, i1 - N_GROUPS, jnp.where(lane == 1, i2 - N_GROUPS,
                         jnp.where(lane == 2, r1, jnp.where(lane == 3, r2, 0))))
    hist_ref[0, 0] = jnp.broadcast_to(jnp.sum(both, axis=0, keepdims=True), (SUBLANES, LANES))


def _outproj(x, yc, yn, of, ob, proj, gnw, wo, g1, nw, sc, sh, wr, br, layer):
    bsz, t, d = x.shape
    tm = min(ROW_TILE, t)
    tok = lambda w: pl.BlockSpec((1, tm, w), lambda b, i: (b, i, 0))
    perb = pl.BlockSpec((1, 1, d), lambda b, i: (b, 0, 0))
    full = lambda shape: pl.BlockSpec(shape, lambda b, i: (0, 0))
    return pl.pallas_call(
        _outproj_kernel,
        grid=(bsz, t // tm),
        in_specs=[tok(d), tok(CONV_W), tok(NA_W), tok(GDN_W), tok(GDN_W),
                  pl.BlockSpec((1, tm, GDN_W), lambda b, i: (b, i, COL_GZ // GDN_W)),
                  full((1, GDN_DV)), pl.BlockSpec((1, d, d), lambda b, i: (layer, 0, 0)), perb, full((1, d)), perb, perb,
                  pl.BlockSpec((2, d, LANES), lambda b, i: (0, 0, 0)), full((1, LANES))],
        out_specs=[tok(d), tok(d), tok(LANES), tok(LANES),
                   pl.BlockSpec((1, 1, SUBLANES, LANES), lambda b, i: (b, i, 0, 0))],
        out_shape=[jax.ShapeDtypeStruct((bsz, t, d), F32), jax.ShapeDtypeStruct((bsz, t, d), F32),
                   jax.ShapeDtypeStruct((bsz, t, LANES), jnp.int32), jax.ShapeDtypeStruct((bsz, t, LANES), F32),
                   jax.ShapeDtypeStruct((bsz, t // tm, SUBLANES, LANES), F32)],
        compiler_params=_cparams(("parallel", "parallel")),
        name="outproj",
    )(x, yc, yn, of, ob, proj, gnw, wo, g1, nw, sc, sh, wr, br)


MOE_TM = 256


def _dispatch_kernel(lb_ref, nu_ref, pos_ref, h_ref, xs_out, zbuf, sem, zsem):
    i = pl.program_id(0)
    tm = h_ref.shape[0]
    bm = zbuf.shape[0]
    nblk = xs_out.shape[0] // bm

    def zero_copy(blk):
        return pltpu.make_async_copy(zbuf, xs_out.at[pl.ds(pl.multiple_of(blk * bm, bm), bm), :], zsem)

    @pl.when(i == 0)
    def _():
        zbuf[...] = jnp.zeros_like(zbuf)
        for e in range(N_EXPERTS):
            @pl.when(lb_ref[e] >= 0)
            def _():
                zero_copy(lb_ref[e]).start()

        def start_trailing(j, carry):
            zero_copy(j).start()
            return carry

        def wait_one(j, carry):
            zero_copy(0).wait()
            return carry

        lax.fori_loop(nu_ref[0], nblk, start_trailing, 0)
        lax.fori_loop(nu_ref[0], nblk, wait_one, 0)
        for e in range(N_EXPERTS):
            @pl.when(lb_ref[e] >= 0)
            def _():
                zero_copy(0).wait()

    def row_copy(r, k):
        dst = pos_ref[0, 0, k * tm + r]
        return pltpu.make_async_copy(h_ref.at[pl.ds(r, 1), :], xs_out.at[pl.ds(dst, 1), :], sem)

    def body(r, carry):
        for k in range(TOP_K):
            row_copy(r, k).start()
        return carry

    lax.fori_loop(0, tm, body, 0, unroll=8)
    for k in range(TOP_K):
        pltpu.make_async_copy(h_ref, xs_out.at[pl.ds(0, tm), :], sem).wait()


def _dispatch(hf, pos3, last_blk, n_used, padded_rows):
    n, d = hf.shape
    tm = MOE_TM
    nt = n // tm
    grid_spec = pltpu.PrefetchScalarGridSpec(
        num_scalar_prefetch=2,
        grid=(nt,),
        in_specs=[pl.BlockSpec((1, 1, TOP_K * tm), lambda i, lb, nu: (i, 0, 0), memory_space=pltpu.SMEM),
                  pl.BlockSpec((tm, d), lambda i, lb, nu: (i, 0))],
        out_specs=pl.BlockSpec(memory_space=pl.ANY),
        scratch_shapes=[pltpu.VMEM((MOE_BM, d), F32), pltpu.SemaphoreType.DMA(()), pltpu.SemaphoreType.DMA(())])
    return pl.pallas_call(
        _dispatch_kernel,
        grid_spec=grid_spec,
        out_shape=jax.ShapeDtypeStruct((padded_rows, d), F32),
        compiler_params=_cparams(("arbitrary",)),
        name="dispatch",
    )(last_blk, n_used, pos3, hf)


def _experts_kernel(be_ref, nu_ref, x_ref, w1_ref, w3_ref, w2_ref, o_ref, w1b, w3b, w2b):
    i = pl.program_id(0)
    used = nu_ref[0]

    @pl.when((i == 0) | (be_ref[i] != be_ref[jnp.maximum(i - 1, 0)]))
    def _():
        w1b[...] = w1_ref[0, 0].astype(BF16)
        w3b[...] = w3_ref[0, 0].astype(BF16)
        w2b[...] = w2_ref[0, 0].astype(BF16)

    @pl.when(i < used)
    def _():
        xb = x_ref[...].astype(BF16)
        h1 = jnp.dot(xb, w1b[...], preferred_element_type=F32)
        h3 = jnp.dot(xb, w3b[...], preferred_element_type=F32)
        act = (_silu(h1) * h3).astype(BF16)
        o_ref[...] = jnp.dot(act, w2b[...], preferred_element_type=F32)

    @pl.when(i >= used)
    def _():
        o_ref[...] = jnp.zeros_like(o_ref)


def _experts(xs, block_e, n_used, w1, w3, w2, layer):
    padded_rows, d = xs.shape
    bm = MOE_BM
    nblk = padded_rows // bm
    de = w1.shape[-1]
    grid_spec = pltpu.PrefetchScalarGridSpec(
        num_scalar_prefetch=2,
        grid=(nblk,),
        in_specs=[pl.BlockSpec((bm, d), lambda i, be, nu: (jnp.maximum(jnp.minimum(i, nu[0] - 1), 0), 0)),
                  pl.BlockSpec((1, 1, d, de), lambda i, be, nu: (layer, be[i], 0, 0)),
                  pl.BlockSpec((1, 1, d, de), lambda i, be, nu: (layer, be[i], 0, 0)),
                  pl.BlockSpec((1, 1, de, d), lambda i, be, nu: (layer, be[i], 0, 0))],
        out_specs=pl.BlockSpec((bm, d), lambda i, be, nu: (i, 0)),
        scratch_shapes=[pltpu.VMEM((d, de), BF16), pltpu.VMEM((d, de), BF16), pltpu.VMEM((de, d), BF16)])
    return pl.pallas_call(
        _experts_kernel,
        grid_spec=grid_spec,
        out_shape=jax.ShapeDtypeStruct((padded_rows, d), F32),
        compiler_params=_cparams(("arbitrary",)),
        name="experts",
    )(block_e, n_used, xs, w1, w3, w2)


def _combine_kernel(pc_ref, pn_ref, x_ref, gt_ref, g2_ref, fw_ref, ys_hbm, o_ref, ybuf, sem, *, final):
    i = pl.program_id(0)
    nt = pl.num_programs(0)
    tm = x_ref.shape[0]

    def start_gather(idx_ref, slot):
        def body(r, carry):
            pltpu.make_async_copy(ys_hbm.at[pl.ds(idx_ref[0, 0, r], 1), :], ybuf.at[slot, pl.ds(r, 1), :],
                                  sem.at[slot]).start()
            return carry
        lax.fori_loop(0, TOP_K * tm, body, 0, unroll=8)

    slot = i % 2

    @pl.when(i == 0)
    def _():
        start_gather(pc_ref, 0)

    @pl.when(i + 1 < nt)
    def _():
        start_gather(pn_ref, 1 - slot)

    pltpu.make_async_copy(ys_hbm.at[pl.ds(0, TOP_K * tm), :], ybuf.at[slot], sem.at[slot]).wait()
    gt = gt_ref[...]
    y = gt[:, 0:1] * ybuf[slot, 0:tm, :] + gt[:, 1:2] * ybuf[slot, tm:2 * tm, :]
    xn = x_ref[...] + g2_ref[0] * y
    if final:
        ms = jnp.mean(xn * xn, axis=-1, keepdims=True)
        xn = xn * lax.rsqrt(ms + EPS) * fw_ref[...]
    o_ref[...] = xn


def _combine(x2, gates, g2, fw, ys, pos3, t, final):
    n, d = x2.shape
    tm = MOE_TM
    nt = n // tm
    per_b = t // tm
    return pl.pallas_call(
        functools.partial(_combine_kernel, final=final),
        grid=(nt,),
        in_specs=[pl.BlockSpec((1, 1, TOP_K * tm), lambda i: (i, 0, 0), memory_space=pltpu.SMEM),
                  pl.BlockSpec((1, 1, TOP_K * tm), lambda i: (jnp.minimum(i + 1, nt - 1), 0, 0),
                               memory_space=pltpu.SMEM),
                  pl.BlockSpec((tm, d), lambda i: (i, 0)),
                  pl.BlockSpec((tm, LANES), lambda i: (i, 0)),
                  pl.BlockSpec((1, 1, d), lambda i: (i // per_b, 0, 0)),
                  pl.BlockSpec((1, d), lambda i: (0, 0)),
                  pl.BlockSpec(memory_space=pl.ANY)],
        out_specs=pl.BlockSpec((tm, d), lambda i: (i, 0)),
        out_shape=jax.ShapeDtypeStruct((n, d), F32),
        scratch_shapes=[pltpu.VMEM((2, TOP_K * tm, d), F32), pltpu.SemaphoreType.DMA((2,))],
        compiler_params=_cparams(("arbitrary",)),
        name="combine",
    )(pos3, pos3, x2, gates, g2, fw, ys)


def _moe_plan(eidx, hist, rank_tile):
    n = eidx.shape[0]
    bm = MOE_BM
    hist = hist[:, 0, :N_EXPERTS].astype(jnp.int32)
    sizes = jnp.sum(hist, axis=0)
    base = jnp.cumsum(hist, axis=0) - hist
    padded = (sizes + bm - 1) // bm * bm
    pad_end = jnp.cumsum(padded)
    tbl = (pad_end - padded)[None, :] + base
    e = eidx[:, :TOP_K].reshape(-1, rank_tile, TOP_K)
    onehot = e[..., None] == jnp.arange(N_EXPERTS, dtype=jnp.int32)
    pos = jnp.sum(jnp.where(onehot, tbl[:, None, None, :], 0), axis=-1).reshape(n, TOP_K) + eidx[:, TOP_K:2 * TOP_K]
    padded_rows = (n * TOP_K + N_EXPERTS * (bm - 1) + bm - 1) // bm * bm
    n_blocks = padded_rows // bm
    blk_start = jnp.arange(n_blocks, dtype=jnp.int32) * bm
    block_e = jnp.minimum(jnp.sum(pad_end[None, :] <= blk_start[:, None], axis=1), N_EXPERTS - 1).astype(jnp.int32)
    n_used = (pad_end[-1] // bm).astype(jnp.int32).reshape(1)
    last_blk = jnp.where(padded > 0, pad_end // bm - 1, -1).astype(jnp.int32)
    nt = n // MOE_TM
    pos3 = jnp.transpose(pos.astype(jnp.int32).reshape(nt, MOE_TM, TOP_K), (0, 2, 1)).reshape(nt, 1, TOP_K * MOE_TM)
    return pos3, block_e, n_used, last_blk, padded_rows


def kernel(x, c, norm_mix_w, norm_ffn_w, w_ada, b_ada, w_in, conv_a_w, na_rpb, gdn_conv_w, gdn_a_log, gdn_dt_bias,
           gdn_norm_w, w_out, router_group_w, router_group_b, router_expert_w, router_expert_b, expert_w1,
           expert_w3, expert_w2, final_norm_w):
    bsz, t, d = x.shape
    depth = w_ada.shape[0]
    n = bsz * t
    mod = _ada(c, w_ada, b_ada)
    w_in_pad = jnp.pad(w_in.astype(BF16), ((0, 0), (0, 0), (0, D_IN_PAD - D_IN)))
    wabt = jnp.transpose(w_in[:, :, COL_AB:], (0, 2, 1)).astype(BF16)
    w_out_b = w_out.astype(BF16)
    wr = jnp.pad(jnp.concatenate([router_group_w, router_expert_w], axis=-1),
                 ((0, 0), (0, 0), (0, LANES - N_GROUPS - N_EXPERTS)))
    wr_hi = wr.astype(BF16)
    wr = jnp.stack([wr_hi, (wr - wr_hi.astype(F32)).astype(BF16)], axis=1)
    br = jnp.pad(jnp.concatenate([router_group_b, router_expert_b], axis=-1),
                 ((0, 0), (0, LANES - N_GROUPS - N_EXPERTS)))
    fw = final_norm_w.reshape(1, d)
    na_bias = _na_bias_table(na_rpb)
    for l in range(depth):
        sh1, sc1, g1, sh2, sc2, g2 = [mod[l, :, j * d:(j + 1) * d].reshape(bsz, 1, d) for j in range(6)]
        proj, abt = _inproj(x, norm_mix_w[l].reshape(1, d), sc1, sh1, w_in_pad, wabt, l)
        y_conv = _convmix(proj, conv_a_w[l])
        y_na = _na(proj, na_bias, l)
        qkv = _gdnprep(proj, gdn_conv_w[l])
        o_f, o_b = _gdnscan(*_gdnchunk(qkv, proj, abt, gdn_a_log[l], gdn_dt_bias[l]))
        x, hf, eidx, gates, hist = _outproj(x, y_conv, y_na, o_f, o_b, proj, gdn_norm_w[l].reshape(1, GDN_DV),
                                            w_out_b, g1, norm_ffn_w[l].reshape(1, d), sc2, sh2, wr[l],
                                            br[l].reshape(1, LANES), l)
        pos3, block_e, n_used, last_blk, padded_rows = _moe_plan(eidx.reshape(n, LANES),
                                                                 hist.reshape(-1, SUBLANES, LANES), min(ROW_TILE, t))
        xs = _dispatch(hf.reshape(n, d), pos3, last_blk, n_used, padded_rows)
        ys = _experts(xs, block_e, n_used, expert_w1, expert_w3, expert_w2, l)
        x = _combine(x.reshape(n, d), gates.reshape(n, LANES), g2, fw, ys, pos3, t,
                     final=(l == depth - 1)).reshape(bsz, t, d)
    return x
```

```python
import functools

import jax
import jax.numpy as jnp
from jax import lax
from jax.experimental import pallas as pl
from jax.experimental.pallas import tpu as pltpu

F32 = jnp.float32
BF16 = jnp.bfloat16

EPS = 1e-6
GRID_W = 64
CONV_W = 256
NA_HEADS = 4
NA_DH = 64
NA_W = NA_HEADS * NA_DH
NA_KH = 8
NA_KW = 16
GDN_HEADS = 4
GDN_DK = 128
GDN_DV = 128
GDN_W = GDN_HEADS * GDN_DV
GDN_QKV = 2 * GDN_HEADS * GDN_DK + GDN_W
GDN_CHUNK = 64
N_GROUPS = 4
EXPERTS_PER_GROUP = 8
N_EXPERTS = N_GROUPS * EXPERTS_PER_GROUP
TOP_K = 2

COL_CONV = 0
COL_NA = 3 * CONV_W
COL_GQKV = COL_NA + 3 * NA_W
COL_GZ = COL_GQKV + GDN_QKV
COL_AB = COL_GZ + GDN_W
D_IN = COL_AB + 4 * GDN_HEADS
D_IN_PAD = COL_AB + 128
LANES = 128
SUBLANES = 8

NEG = -1e30
VMEM_LIMIT = 56 * 1024 * 1024

MOE_BM = 512
ROW_TILE = 512


def _cparams(sem):
    return pltpu.CompilerParams(dimension_semantics=sem, vmem_limit_bytes=VMEM_LIMIT)


def _silu(x):
    return x * (1.0 / (1.0 + jnp.exp(-x)))


def _ada_kernel(ct_ref, w_ref, b_ref, o_ref):
    w = w_ref[0]
    nb = ct_ref.shape[1]
    for r in range(nb):
        col = _silu(ct_ref[:, r:r + 1])
        o_ref[0, r:r + 1, :] = jnp.sum(w * col, axis=0, keepdims=True) + b_ref[0]


def _ada(c, w_ada, b_ada):
    depth, d, n6 = w_ada.shape
    bsz = c.shape[0]
    tn = 512
    return pl.pallas_call(
        _ada_kernel,
        grid=(depth, n6 // tn),
        in_specs=[pl.BlockSpec((d, bsz), lambda l, j: (0, 0)),
                  pl.BlockSpec((1, d, tn), lambda l, j: (l, 0, j)),
                  pl.BlockSpec((1, 1, tn), lambda l, j: (l, 0, j))],
        out_specs=pl.BlockSpec((1, bsz, tn), lambda l, j: (l, 0, j)),
        out_shape=jax.ShapeDtypeStruct((depth, bsz, n6), F32),
        compiler_params=_cparams(("parallel", "parallel")),
        name="ada",
    )(c.T, w_ada, b_ada.reshape(depth, 1, n6))


def _modnorm(x, nw, sc, sh):
    ms = jnp.mean(x * x, axis=-1, keepdims=True)
    y = x * lax.rsqrt(ms + EPS)
    return (y * nw) * (1.0 + sc) + sh


def _inproj_kernel(x_ref, nw_ref, sc_ref, sh_ref, w_ref, wabt_ref, o_ref, ot_ref):
    h = _modnorm(x_ref[0], nw_ref[...], sc_ref[0], sh_ref[0]).astype(BF16)
    o_ref[0] = jnp.dot(h, w_ref[0], preferred_element_type=F32)
    ot_ref[0] = lax.dot_general(wabt_ref[0], h, (((1,), (1,)), ((), ())), preferred_element_type=F32)


def _inproj(x, nw, sc, sh, w_pad, wabt, layer):
    bsz, t, d = x.shape
    tm = min(ROW_TILE, t)
    nab = wabt.shape[1]
    return pl.pallas_call(
        _inproj_kernel,
        grid=(bsz, t // tm),
        in_specs=[pl.BlockSpec((1, tm, d), lambda b, i: (b, i, 0)),
                  pl.BlockSpec((1, d), lambda b, i: (0, 0)),
                  pl.BlockSpec((1, 1, d), lambda b, i: (b, 0, 0)),
                  pl.BlockSpec((1, 1, d), lambda b, i: (b, 0, 0)),
                  pl.BlockSpec((1, d, D_IN_PAD), lambda b, i: (layer, 0, 0)),
                  pl.BlockSpec((1, nab, d), lambda b, i: (layer, 0, 0))],
        out_specs=[pl.BlockSpec((1, tm, D_IN_PAD), lambda b, i: (b, i, 0)),
                   pl.BlockSpec((1, nab, tm), lambda b, i: (b, 0, i))],
        out_shape=[jax.ShapeDtypeStruct((bsz, t, D_IN_PAD), F32),
                   jax.ShapeDtypeStruct((bsz, nab, t), F32)],
        compiler_params=_cparams(("parallel", "parallel")),
        name="inproj",
    )(x, nw, sc, sh, w_pad, wabt)


def _dwconv3(u, prev_row, next_row, w_ref):
    tt = u.shape[0]
    row = lax.broadcasted_iota(jnp.int32, u.shape, 0)
    dn = jnp.where(row == 0, prev_row, pltpu.roll(u, 1, axis=0))
    up = jnp.where(row == tt - 1, next_row, pltpu.roll(u, tt - 1, axis=0))
    return w_ref[0:1, :] * dn + w_ref[1:2, :] * u + w_ref[2:3, :] * up


def _halo_specs(tt, t, width, colblk):
    nsub = tt // SUBLANES
    last = t // SUBLANES - 1
    return [pl.BlockSpec((1, tt, width), lambda b, i: (b, i, colblk)),
            pl.BlockSpec((1, SUBLANES, width), lambda b, i: (b, jnp.maximum(i * nsub - 1, 0), colblk)),
            pl.BlockSpec((1, SUBLANES, width), lambda b, i: (b, jnp.minimum((i + 1) * nsub, last), colblk))]


def _convmix_tile(m_ref, p_ref, n_ref, w_ref):
    i = pl.program_id(1)
    nt = pl.num_programs(1)
    m = m_ref[0]
    u = m[:, CONV_W:2 * CONV_W] * m[:, 2 * CONV_W:]
    p = p_ref[0]
    n = n_ref[0]
    pu = p[SUBLANES - 1:SUBLANES, CONV_W:2 * CONV_W] * p[SUBLANES - 1:SUBLANES, 2 * CONV_W:]
    nu = n[0:1, CONV_W:2 * CONV_W] * n[0:1, 2 * CONV_W:]
    pu = jnp.where(i == 0, 0.0, pu)
    nu = jnp.where(i == nt - 1, 0.0, nu)
    return m[:, :CONV_W] * _dwconv3(u, pu, nu, w_ref)


NA_RB = NA_KH // 2
NA_TOK = NA_RB * GRID_W
NA_KEYS = 3 * NA_TOK


def _na_bias_table(rpb):
    col = jnp.arange(GRID_W)
    cstart = jnp.clip(col - NA_KW // 2, 0, GRID_W - NA_KW)
    kc = jnp.arange(GRID_W)
    valid = (kc[None, :] >= cstart[:, None]) & (kc[None, :] < cstart[:, None] + NA_KW)
    dc = kc[None, :] - col[:, None] + (NA_KW - 1)
    onehot = (dc[None] == jnp.arange(2 * NA_KW - 1)[:, None, None]) & valid[None]
    cols = jnp.einsum('lhrd,dck->lhrck', rpb, onehot.astype(F32), precision=lax.Precision.HIGHEST)
    cols = jnp.where(valid, cols, NEG)
    lo = NA_KH - 1 - NA_RB
    blk = jnp.stack([cols[:, :, lo - j:lo - j + 3 * NA_RB] for j in range(NA_RB)], axis=2)
    blk = jnp.transpose(blk, (0, 1, 2, 4, 3, 5))
    return blk.reshape(rpb.shape[0], NA_HEADS, NA_TOK, NA_KEYS)


def _na_kernel(q_ref, kp_ref, kc_ref, kn_ref, vp_ref, vc_ref, vn_ref, bias_ref, o_ref, *, rows):
    i = pl.program_id(1)
    kbuf = jnp.concatenate([kp_ref[0], kc_ref[0], kn_ref[0]], axis=0).astype(BF16)
    vbuf = jnp.concatenate([vp_ref[0], vc_ref[0], vn_ref[0]], axis=0).astype(BF16)
    q = q_ref[0] * (NA_DH ** -0.5)
    head_of_lane = lax.broadcasted_iota(jnp.int32, (1, NA_W), 1) // NA_DH
    qrow = i * NA_RB + lax.broadcasted_iota(jnp.int32, (NA_TOK, NA_KEYS), 0) // GRID_W
    krow = (i - 1) * NA_RB + lax.broadcasted_iota(jnp.int32, (NA_TOK, NA_KEYS), 1) // GRID_W
    rs = jnp.clip(qrow - NA_KH // 2, 0, rows - NA_KH)
    row_mask = jnp.where((krow >= rs) & (krow < rs + NA_KH), 0.0, NEG)
    acc = jnp.zeros((NA_TOK, NA_W), F32)
    for h in range(NA_HEADS):
        mine = head_of_lane == h
        s = _nt(jnp.where(mine, q, 0.0).astype(BF16), kbuf) + (bias_ref[0, h] + row_mask)
        m = jnp.max(s, axis=-1, keepdims=True)
        p = jnp.exp(s - m)
        l = jnp.sum(p, axis=-1, keepdims=True)
        o = jnp.dot((p / l).astype(BF16), vbuf, preferred_element_type=F32)
        acc = acc + jnp.where(mine, o, 0.0)
    o_ref[0] = acc


def _na(proj, bias_tbl, layer):
    bsz, t, _ = proj.shape
    rows = t // GRID_W
    nblk = rows // NA_RB
    qc, kc, vc = COL_NA // NA_W, COL_NA // NA_W + 1, COL_NA // NA_W + 2

    def spec(col, shift):
        return pl.BlockSpec((1, NA_TOK, NA_W), lambda b, i: (b, jnp.clip(i + shift, 0, nblk - 1), col))

    return pl.pallas_call(
        functools.partial(_na_kernel, rows=rows),
        grid=(bsz, nblk),
        in_specs=[spec(qc, 0), spec(kc, -1), spec(kc, 0), spec(kc, 1), spec(vc, -1), spec(vc, 0), spec(vc, 1),
                  pl.BlockSpec((1, NA_HEADS, NA_TOK, NA_KEYS), lambda b, i: (layer, 0, 0, 0))],
        out_specs=pl.BlockSpec((1, NA_TOK, NA_W), lambda b, i: (b, i, 0)),
        out_shape=jax.ShapeDtypeStruct((bsz, t, NA_W), F32),
        compiler_params=_cparams(("parallel", "parallel")),
        name="na",
    )(proj, proj, proj, proj, proj, proj, proj, bias_tbl)


def _gdn_qkv(m_ref, p_ref, n_ref, w_ref):
    i = pl.program_id(1)
    nt = pl.num_programs(1)
    pu = jnp.where(i == 0, 0.0, p_ref[0, SUBLANES - 1:SUBLANES, :])
    nu = jnp.where(i == nt - 1, 0.0, n_ref[0, 0:1, :])
    c = _silu(_dwconv3(m_ref[0], pu, nu, w_ref))
    heads = []
    for hh in range(GDN_QKV // GDN_DK):
        xh = c[:, hh * GDN_DK:(hh + 1) * GDN_DK]
        if hh < 2 * GDN_HEADS:
            xh = xh * lax.rsqrt(jnp.sum(xh * xh, axis=-1, keepdims=True) + EPS)
            if hh < GDN_HEADS:
                xh = xh * (GDN_DK ** -0.5)
        heads.append(xh)
    return heads[:GDN_HEADS], heads[GDN_HEADS:2 * GDN_HEADS], heads[2 * GDN_HEADS:]


GDN_GS = 128
GDN_UNITS = 2 * GDN_HEADS
GDN_GC = GDN_GS // GDN_CHUNK


def _softplus(x):
    return jnp.maximum(x, 0.0) + jnp.log1p(jnp.exp(-jnp.abs(x)))


def _seg_cumsum(x, axis, reverse):
    n = x.shape[axis]
    pos = lax.broadcasted_iota(jnp.int32, x.shape, axis) & (GDN_CHUNK - 1)
    s = 1
    while s < GDN_CHUNK:
        if reverse:
            x = x + jnp.where(pos < GDN_CHUNK - s, pltpu.roll(x, n - s, axis=axis), 0.0)
        else:
            x = x + jnp.where(pos >= s, pltpu.roll(x, s, axis=axis), 0.0)
        s *= 2
    return x


def _nt(a, b):
    return lax.dot_general(a, b, (((1,), (1,)), ((), ())), preferred_element_type=F32)


def _tn(a, b):
    return lax.dot_general(a, b, (((0,), (0,)), ((), ())), preferred_element_type=F32)


def _mm(a, b):
    return jnp.dot(a, b, preferred_element_type=F32)


def _gdnchunk_kernel(m_ref, p_ref, n_ref, cw_ref, ab_ref, abt_ref, alr_ref, dtr_ref, alc_ref, dtc_ref,
                     u_ref, wq_ref, kd_ref, in_ref, gl_ref):
    gs, c, nh = GDN_GS, GDN_CHUNK, GDN_HEADS
    ri = lax.broadcasted_iota(jnp.int32, (gs, gs), 0)
    ci = lax.broadcasted_iota(jnp.int32, (gs, gs), 1)
    same = (ri // c) == (ci // c)
    eye = (ri == ci).astype(F32)
    rowc = lax.broadcasted_iota(jnp.int32, (gs, 1), 0) // c
    ab = ab_ref[0]
    abt = abt_ref[0]
    graw_c = -jnp.exp(alr_ref[...]) * _softplus(ab + dtr_ref[...])
    graw_r = -jnp.exp(alc_ref[...]) * _softplus(abt[0:2 * nh] + dtc_ref[...])
    beta_c = 1.0 / (1.0 + jnp.exp(-ab))
    g_col = [_seg_cumsum(graw_c, 0, False), _seg_cumsum(graw_c, 0, True)]
    g_row = [_seg_cumsum(graw_r, 1, False), _seg_cumsum(graw_r, 1, True)]
    incl = [same & (ri >= ci), same & (ri <= ci)]
    strict = [same & (ri > ci), same & (ri < ci)]

    qs, ks, vs = _gdn_qkv(m_ref, p_ref, n_ref, cw_ref)
    grams = [_nt(jnp.concatenate([q, k], axis=0).astype(BF16), k.astype(BF16)) for q, k in zip(qs, ks)]

    units = [(d, h) for d in range(2) for h in range(nh)]
    gcs, bcs, intras, xs, ps = [], [], [], [], []
    for d, h in units:
        col = d * nh + h
        gc = g_col[d][:, col:col + 1]
        gr = g_row[d][col:col + 1, :]
        bc = beta_c[:, 2 * nh + col:2 * nh + col + 1]
        e_incl = jnp.exp(jnp.where(incl[d], gc - gr, NEG))
        a = grams[h][gs:] * bc * jnp.where(strict[d], e_incl, 0.0)
        gcs.append(gc), bcs.append(bc)
        intras.append(grams[h][:gs] * e_incl)
        xs.append(eye - a), ps.append(a)
    for _ in range(5):
        pbs = [p.astype(BF16) for p in ps]
        ps = [_mm(pb, pb) for pb in pbs]
        xs = [x + _mm(x.astype(BF16), p.astype(BF16)) for x, p in zip(xs, ps)]
    egs = [jnp.exp(gc) for gc in gcs]
    sols = [_mm(x.astype(BF16),
                jnp.concatenate([vs[h] * bc, ks[h] * (bc * eg)], axis=1).astype(BF16))
            for (d, h), x, bc, eg in zip(units, xs, bcs, egs)]
    for (d, h), gc, eg, sol, intra in zip(units, gcs, egs, sols, intras):
        col = d * nh + h
        glast_col = jnp.zeros_like(gc)
        for n in range(GDN_GC):
            r = n * c if d == 1 else (n + 1) * c - 1
            glast = gc[r:r + 1, :]
            glast_col = jnp.where(rowc == n, glast, glast_col)
            gl_ref[0, 0, col * GDN_GC + n:col * GDN_GC + n + 1, :] = jnp.broadcast_to(jnp.exp(glast), (1, LANES))
        u_ref[0, col] = sol[:, :GDN_DV]
        w = sol[:, GDN_DV:].astype(BF16)
        qd = (qs[h] * eg).astype(BF16)
        for n in range(GDN_GC):
            wq_ref[0, col, 2 * n * c:(2 * n + 1) * c, :] = w[n * c:(n + 1) * c]
            wq_ref[0, col, (2 * n + 1) * c:(2 * n + 2) * c, :] = qd[n * c:(n + 1) * c]
        kd_ref[0, col] = (ks[h] * jnp.exp(glast_col - gc)).astype(BF16)
        in_ref[0, col] = jnp.concatenate([intra[n * c:(n + 1) * c, n * c:(n + 1) * c] for n in range(GDN_GC)],
                                         axis=0).astype(BF16)


def _gdnchunk(proj, abt, conv_w, a_log, dt_bias):
    bsz, t, _ = proj.shape
    gs, nu = GDN_GS, GDN_UNITS
    ng = t // gs
    nab = abt.shape[1]
    pad = lambda r: jnp.pad(r.reshape(1, -1), ((0, 0), (0, LANES - r.size)))
    alr, dtr = pad(a_log), pad(dt_bias)
    alc, dtc = a_log.reshape(-1, 1), dt_bias.reshape(-1, 1)
    small = lambda shape: pl.BlockSpec(shape, lambda b, i: (0, 0))
    unit = lambda rows, w: pl.BlockSpec((1, nu, rows, w), lambda b, i: (b, 0, i, 0))
    return pl.pallas_call(
        _gdnchunk_kernel,
        grid=(bsz, ng),
        in_specs=_halo_specs(gs, t, GDN_QKV, COL_GQKV // GDN_QKV) + [
            small((3, GDN_QKV)),
            pl.BlockSpec((1, gs, LANES), lambda b, i: (b, i, COL_AB // LANES)),
            pl.BlockSpec((1, nab, gs), lambda b, i: (b, 0, i)),
            small((1, LANES)), small((1, LANES)), small((nu, 1)), small((nu, 1))],
        out_specs=[unit(gs, GDN_DV), unit(2 * gs, GDN_DK), unit(gs, GDN_DK), unit(gs, GDN_CHUNK),
                   pl.BlockSpec((1, 1, nu * GDN_GC, LANES), lambda b, i: (b, i, 0, 0))],
        out_shape=[jax.ShapeDtypeStruct((bsz, nu, t, GDN_DV), F32),
                   jax.ShapeDtypeStruct((bsz, nu, 2 * t, GDN_DK), BF16),
                   jax.ShapeDtypeStruct((bsz, nu, t, GDN_DK), BF16),
                   jax.ShapeDtypeStruct((bsz, nu, t, GDN_CHUNK), BF16),
                   jax.ShapeDtypeStruct((bsz, ng, nu * GDN_GC, LANES), F32)],
        compiler_params=_cparams(("parallel", "parallel")),
        name="gdnchunk",
    )(proj, proj, proj, conv_w, proj, abt, alr, dtr, alc, dtc)


GDN_SB = 4
GDN_ST = GDN_SB * GDN_CHUNK


def _gdnscan_kernel(uf, ub, wqf, wqb, kdf, kdb, inf, inb, glf, glb, of_ref, ob_ref, s_ref):
    i = pl.program_id(0)

    @pl.when(i == 0)
    def _():
        s_ref[...] = jnp.zeros_like(s_ref)

    bsz = uf.shape[0]
    c, nh = GDN_CHUNK, GDN_HEADS
    chains = [(b, d, h) for b in range(bsz) for d in range(2) for h in range(nh)]
    for step in range(GDN_SB):
        zs, vns = [], []
        for b, d, h in chains:
            cc = step if d == 0 else GDN_SB - 1 - step
            wq = (wqf, wqb)[d]
            st = s_ref[(b * 2 + d) * nh + h]
            zs.append(_mm(wq[b, h, 2 * cc * c:(2 * cc + 2) * c, :], st.astype(BF16)))
        for (b, d, h), z in zip(chains, zs):
            cc = step if d == 0 else GDN_SB - 1 - step
            u = (uf, ub)[d]
            vns.append((u[b, h, cc * c:(cc + 1) * c, :] - z[:c]).astype(BF16))
        for (b, d, h), z, vn in zip(chains, zs, vns):
            cc = step if d == 0 else GDN_SB - 1 - step
            rows = slice(cc * c, (cc + 1) * c)
            intra = (inf, inb)[d]
            o_ref = (of_ref, ob_ref)[d]
            o_ref[b, rows, h * GDN_DV:(h + 1) * GDN_DV] = z[c:] + _mm(intra[b, h, rows, :], vn)
        for (b, d, h), vn in zip(chains, vns):
            cc = step if d == 0 else GDN_SB - 1 - step
            rows = slice(cc * c, (cc + 1) * c)
            kd = (kdf, kdb)[d]
            gl = (glf, glb)[d]
            r = (d * nh + h) * GDN_GC + cc % GDN_GC
            sidx = (b * 2 + d) * nh + h
            s_ref[sidx] = s_ref[sidx] * gl[b, cc // GDN_GC, r:r + 1, :] + _tn(kd[b, h, rows, :], vn)


def _gdnscan(u, wq, kd, intra, gl):
    bsz, nu, t, _ = u.shape
    nh = GDN_HEADS
    st = GDN_ST
    nb = t // st
    ngs = st // GDN_GS

    def unit(rows, w, d):
        if d == 0:
            return pl.BlockSpec((bsz, nh, rows, w), lambda i: (0, 0, i, 0))
        return pl.BlockSpec((bsz, nh, rows, w), lambda i: (0, 1, nb - 1 - i, 0))

    glspec = lambda d: pl.BlockSpec((bsz, ngs, nu * GDN_GC, LANES),
                                    (lambda i: (0, i, 0, 0)) if d == 0 else (lambda i: (0, nb - 1 - i, 0, 0)))
    return pl.pallas_call(
        _gdnscan_kernel,
        grid=(nb,),
        in_specs=[unit(st, GDN_DV, 0), unit(st, GDN_DV, 1), unit(2 * st, GDN_DK, 0), unit(2 * st, GDN_DK, 1),
                  unit(st, GDN_DK, 0), unit(st, GDN_DK, 1), unit(st, GDN_CHUNK, 0), unit(st, GDN_CHUNK, 1),
                  glspec(0), glspec(1)],
        out_specs=[pl.BlockSpec((bsz, st, GDN_W), lambda i: (0, i, 0)),
                   pl.BlockSpec((bsz, st, GDN_W), lambda i: (0, nb - 1 - i, 0))],
        out_shape=[jax.ShapeDtypeStruct((bsz, t, GDN_W), F32), jax.ShapeDtypeStruct((bsz, t, GDN_W), F32)],
        scratch_shapes=[pltpu.VMEM((bsz * nu, GDN_DK, GDN_DV), F32)],
        compiler_params=_cparams(("arbitrary",)),
        name="gdnscan",
    )(u, u, wq, wq, kd, kd, intra, intra, gl, gl)


def _outproj_kernel(x_ref, cm_ref, cp_ref, cn_ref, cw_ref, yn_ref, of_ref, ob_ref, z_ref, gnw_ref, wo_ref, g1_ref,
                    nw_ref, sc_ref, sh_ref, wr_ref, br_ref, xo_ref, h_ref, e_ref, g_ref, hist_ref):
    o = of_ref[0] + ob_ref[0]
    z = z_ref[0]
    parts = [_convmix_tile(cm_ref, cp_ref, cn_ref, cw_ref), yn_ref[0]]
    for h in range(GDN_HEADS):
        sl = slice(h * GDN_DV, (h + 1) * GDN_DV)
        oh = o[:, sl]
        oh = oh * lax.rsqrt(jnp.mean(oh * oh, axis=-1, keepdims=True) + EPS) * gnw_ref[...]
        parts.append(oh * _silu(z[:, sl]))
    mixed = jnp.dot(jnp.concatenate(parts, axis=-1).astype(BF16), wo_ref[0], preferred_element_type=F32)
    xn = x_ref[0] + g1_ref[0] * mixed
    xo_ref[0] = xn
    hf = _modnorm(xn, nw_ref[...], sc_ref[0], sh_ref[0])
    h_ref[0] = hf
    hf_hi = hf.astype(BF16)
    hf_lo = (hf - hf_hi.astype(F32)).astype(BF16)
    logits = (_mm(hf_hi, wr_ref[0]) + (_mm(hf_hi, wr_ref[1]) + _mm(hf_lo, wr_ref[0]))) + br_ref[...]
    lane = lax.broadcasted_iota(jnp.int32, logits.shape, 1)
    gl = jnp.where(lane < N_GROUPS, logits, NEG)
    gm = jnp.max(gl, axis=-1, keepdims=True)
    den = jnp.sum(jnp.exp(gl - gm), axis=-1, keepdims=True)
    grp = jnp.min(jnp.where(gl == gm, lane, LANES), axis=-1, keepdims=True)
    pg_top = 1.0 / den
    ex = lane - N_GROUPS
    in_grp = (ex >= grp * EXPERTS_PER_GROUP) & (ex < (grp + 1) * EXPERTS_PER_GROUP)
    el = jnp.where(in_grp, logits, NEG)
    m1 = jnp.max(el, axis=-1, keepdims=True)
    i1 = jnp.min(jnp.where(el == m1, lane, LANES), axis=-1, keepdims=True)
    el2 = jnp.where(lane == i1, NEG, el)
    m2 = jnp.max(el2, axis=-1, keepdims=True)
    i2 = jnp.min(jnp.where(el2 == m2, lane, LANES), axis=-1, keepdims=True)
    e2 = jnp.exp(m2 - m1)
    w1 = pg_top / (1.0 + e2)
    w2 = pg_top * e2 / (1.0 + e2)
    g_ref[0] = jnp.where(lane == 0, w1, jnp.where(lane == 1, w2, 0.0))
    oh1 = (lane == i1 - N_GROUPS).astype(F32)
    oh2 = (lane == i2 - N_GROUPS).astype(F32)
    both = oh1 + oh2
    tm = logits.shape[0]
    earlier = (lax.broadcasted_iota(jnp.int32, (tm, tm), 0) > lax.broadcasted_iota(jnp.int32, (tm, tm), 1))
    cnt = jnp.dot(earlier.astype(BF16), both.astype(BF16), preferred_element_type=F32)
    r1 = jnp.sum(cnt * oh1, axis=-1, keepdims=True).astype(jnp.int32)
    r2 = jnp.sum(cnt * oh2, axis=-1, keepdims=True).astype(jnp.int32)
    e_ref[0] = jnp.where(lane == 0, i1 - N_GROUPS, jnp.where(lane == 1, i2 - N_GROUPS,
                         jnp.where(lane == 2, r1, jnp.where(lane == 3, r2, 0))))
    hist_ref[0, 0] = jnp.broadcast_to(jnp.sum(both, axis=0, keepdims=True), (SUBLANES, LANES))


def _outproj(x, conv_w, yn, of, ob, proj, gnw, wo, g1, nw, sc, sh, wr, br, layer):
    bsz, t, d = x.shape
    tm = min(ROW_TILE, t)
    tok = lambda w: pl.BlockSpec((1, tm, w), lambda b, i: (b, i, 0))
    perb = pl.BlockSpec((1, 1, d), lambda b, i: (b, 0, 0))
    full = lambda shape: pl.BlockSpec(shape, lambda b, i: (0, 0))
    return pl.pallas_call(
        _outproj_kernel,
        grid=(bsz, t // tm),
        in_specs=[tok(d)] + _halo_specs(tm, t, 3 * CONV_W, COL_CONV // (3 * CONV_W)) + [
                  full((3, CONV_W)), tok(NA_W), tok(GDN_W), tok(GDN_W),
                  pl.BlockSpec((1, tm, GDN_W), lambda b, i: (b, i, COL_GZ // GDN_W)),
                  full((1, GDN_DV)), pl.BlockSpec((1, d, d), lambda b, i: (layer, 0, 0)), perb, full((1, d)), perb, perb,
                  pl.BlockSpec((2, d, LANES), lambda b, i: (0, 0, 0)), full((1, LANES))],
        out_specs=[tok(d), tok(d), tok(LANES), tok(LANES),
                   pl.BlockSpec((1, 1, SUBLANES, LANES), lambda b, i: (b, i, 0, 0))],
        out_shape=[jax.ShapeDtypeStruct((bsz, t, d), F32), jax.ShapeDtypeStruct((bsz, t, d), F32),
                   jax.ShapeDtypeStruct((bsz, t, LANES), jnp.int32), jax.ShapeDtypeStruct((bsz, t, LANES), F32),
                   jax.ShapeDtypeStruct((bsz, t // tm, SUBLANES, LANES), F32)],
        compiler_params=_cparams(("parallel", "parallel")),
        name="outproj",
    )(x, proj, proj, proj, conv_w, yn, of, ob, proj, gnw, wo, g1, nw, sc, sh, wr, br)


MOE_TM = 256


def _dispatch_kernel(lb_ref, nu_ref, pos_ref, h_ref, xs_out, zbuf, sem, zsem):
    i = pl.program_id(0)
    tm = h_ref.shape[0]
    bm = zbuf.shape[0]
    nblk = xs_out.shape[0] // bm

    def zero_copy(blk):
        return pltpu.make_async_copy(zbuf, xs_out.at[pl.ds(pl.multiple_of(blk * bm, bm), bm), :], zsem)

    @pl.when(i == 0)
    def _():
        zbuf[...] = jnp.zeros_like(zbuf)
        for e in range(N_EXPERTS):
            @pl.when(lb_ref[e] >= 0)
            def _():
                zero_copy(lb_ref[e]).start()

        def start_trailing(j, carry):
            zero_copy(j).start()
            return carry

        def wait_one(j, carry):
            zero_copy(0).wait()
            return carry

        lax.fori_loop(nu_ref[0], nblk, start_trailing, 0)
        lax.fori_loop(nu_ref[0], nblk, wait_one, 0)
        for e in range(N_EXPERTS):
            @pl.when(lb_ref[e] >= 0)
            def _():
                zero_copy(0).wait()

    def row_copy(r, k):
        dst = pos_ref[0, 0, k * tm + r]
        return pltpu.make_async_copy(h_ref.at[pl.ds(r, 1), :], xs_out.at[pl.ds(dst, 1), :], sem)

    def body(r, carry):
        for k in range(TOP_K):
            row_copy(r, k).start(priority=k % 2)
        return carry

    lax.fori_loop(0, tm, body, 0, unroll=8)
    for k in range(TOP_K):
        pltpu.make_async_copy(h_ref, xs_out.at[pl.ds(0, tm), :], sem).wait()


def _dispatch(hf, pos3, last_blk, n_used, padded_rows):
    n, d = hf.shape
    tm = MOE_TM
    nt = n // tm
    grid_spec = pltpu.PrefetchScalarGridSpec(
        num_scalar_prefetch=2,
        grid=(nt,),
        in_specs=[pl.BlockSpec((1, 1, TOP_K * tm), lambda i, lb, nu: (i, 0, 0), memory_space=pltpu.SMEM),
                  pl.BlockSpec((tm, d), lambda i, lb, nu: (i, 0))],
        out_specs=pl.BlockSpec(memory_space=pl.ANY),
        scratch_shapes=[pltpu.VMEM((MOE_BM, d), F32), pltpu.SemaphoreType.DMA(()), pltpu.SemaphoreType.DMA(())])
    return pl.pallas_call(
        _dispatch_kernel,
        grid_spec=grid_spec,
        out_shape=jax.ShapeDtypeStruct((padded_rows, d), F32),
        compiler_params=_cparams(("arbitrary",)),
        name="dispatch",
    )(last_blk, n_used, pos3, hf)


def _experts_kernel(be_ref, nu_ref, x_ref, w1_ref, w3_ref, w2_ref, o_ref, w1b, w3b, w2b):
    i = pl.program_id(0)
    used = nu_ref[0]

    @pl.when((i == 0) | (be_ref[i] != be_ref[jnp.maximum(i - 1, 0)]))
    def _():
        w1b[...] = w1_ref[0, 0].astype(BF16)
        w3b[...] = w3_ref[0, 0].astype(BF16)
        w2b[...] = w2_ref[0, 0].astype(BF16)

    @pl.when(i < used)
    def _():
        xb = x_ref[...].astype(BF16)
        h1 = jnp.dot(xb, w1b[...], preferred_element_type=F32)
        h3 = jnp.dot(xb, w3b[...], preferred_element_type=F32)
        act = (_silu(h1) * h3).astype(BF16)
        o_ref[...] = jnp.dot(act, w2b[...], preferred_element_type=F32)

    @pl.when(i >= used)
    def _():
        o_ref[...] = jnp.zeros_like(o_ref)


def _experts(xs, block_e, n_used, w1, w3, w2, layer):
    padded_rows, d = xs.shape
    bm = MOE_BM
    nblk = padded_rows // bm
    de = w1.shape[-1]
    grid_spec = pltpu.PrefetchScalarGridSpec(
        num_scalar_prefetch=2,
        grid=(nblk,),
        in_specs=[pl.BlockSpec((bm, d), lambda i, be, nu: (jnp.maximum(jnp.minimum(i, nu[0] - 1), 0), 0)),
                  pl.BlockSpec((1, 1, d, de), lambda i, be, nu: (layer, be[i], 0, 0)),
                  pl.BlockSpec((1, 1, d, de), lambda i, be, nu: (layer, be[i], 0, 0)),
                  pl.BlockSpec((1, 1, de, d), lambda i, be, nu: (layer, be[i], 0, 0))],
        out_specs=pl.BlockSpec((bm, d), lambda i, be, nu: (i, 0)),
        scratch_shapes=[pltpu.VMEM((d, de), BF16), pltpu.VMEM((d, de), BF16), pltpu.VMEM((de, d), BF16)])
    return pl.pallas_call(
        _experts_kernel,
        grid_spec=grid_spec,
        out_shape=jax.ShapeDtypeStruct((padded_rows, d), F32),
        compiler_params=_cparams(("arbitrary",)),
        name="experts",
    )(block_e, n_used, xs, w1, w3, w2)


def _combine_kernel(pc_ref, pn_ref, x_ref, gt_ref, g2_ref, fw_ref, ys_hbm, o_ref, ybuf, sem, *, final):
    i = pl.program_id(0)
    nt = pl.num_programs(0)
    tm = x_ref.shape[0]

    def start_gather(idx_ref, slot):
        def body(j, carry):
            for par in range(2):
                r = 2 * j + par
                pltpu.make_async_copy(ys_hbm.at[pl.ds(idx_ref[0, 0, r], 1), :], ybuf.at[slot, pl.ds(r, 1), :],
                                      sem.at[slot]).start(priority=par)
            return carry
        lax.fori_loop(0, TOP_K * tm // 2, body, 0, unroll=4)

    slot = i % 2

    @pl.when(i == 0)
    def _():
        start_gather(pc_ref, 0)

    @pl.when(i + 1 < nt)
    def _():
        start_gather(pn_ref, 1 - slot)

    pltpu.make_async_copy(ys_hbm.at[pl.ds(0, TOP_K * tm), :], ybuf.at[slot], sem.at[slot]).wait()
    gt = gt_ref[...]
    y = gt[:, 0:1] * ybuf[slot, 0:tm, :] + gt[:, 1:2] * ybuf[slot, tm:2 * tm, :]
    xn = x_ref[...] + g2_ref[0] * y
    if final:
        ms = jnp.mean(xn * xn, axis=-1, keepdims=True)
        xn = xn * lax.rsqrt(ms + EPS) * fw_ref[...]
    o_ref[...] = xn


def _combine(x2, gates, g2, fw, ys, pos3, t, final):
    n, d = x2.shape
    tm = MOE_TM
    nt = n // tm
    per_b = t // tm
    return pl.pallas_call(
        functools.partial(_combine_kernel, final=final),
        grid=(nt,),
        in_specs=[pl.BlockSpec((1, 1, TOP_K * tm), lambda i: (i, 0, 0), memory_space=pltpu.SMEM),
                  pl.BlockSpec((1, 1, TOP_K * tm), lambda i: (jnp.minimum(i + 1, nt - 1), 0, 0),
                               memory_space=pltpu.SMEM),
                  pl.BlockSpec((tm, d), lambda i: (i, 0)),
                  pl.BlockSpec((tm, LANES), lambda i: (i, 0)),
                  pl.BlockSpec((1, 1, d), lambda i: (i // per_b, 0, 0)),
                  pl.BlockSpec((1, d), lambda i: (0, 0)),
                  pl.BlockSpec(memory_space=pl.ANY)],
        out_specs=pl.BlockSpec((tm, d), lambda i: (i, 0)),
        out_shape=jax.ShapeDtypeStruct((n, d), F32),
        scratch_shapes=[pltpu.VMEM((2, TOP_K * tm, d), F32), pltpu.SemaphoreType.DMA((2,))],
        compiler_params=_cparams(("arbitrary",)),
        name="combine",
    )(pos3, pos3, x2, gates, g2, fw, ys)


def _moe_plan(eidx, hist, rank_tile):
    n = eidx.shape[0]
    bm = MOE_BM
    hist = hist[:, 0, :N_EXPERTS].astype(jnp.int32)
    sizes = jnp.sum(hist, axis=0)
    base = jnp.cumsum(hist, axis=0) - hist
    padded = (sizes + bm - 1) // bm * bm
    pad_end = jnp.cumsum(padded)
    tbl = (pad_end - padded)[None, :] + base
    e = eidx[:, :TOP_K].reshape(-1, rank_tile, TOP_K)
    onehot = e[..., None] == jnp.arange(N_EXPERTS, dtype=jnp.int32)
    pos = jnp.sum(jnp.where(onehot, tbl[:, None, None, :], 0), axis=-1).reshape(n, TOP_K) + eidx[:, TOP_K:2 * TOP_K]
    padded_rows = (n * TOP_K + N_EXPERTS * (bm - 1) + bm - 1) // bm * bm
    n_blocks = padded_rows // bm
    blk_start = jnp.arange(n_blocks, dtype=jnp.int32) * bm
    block_e = jnp.minimum(jnp.sum(pad_end[None, :] <= blk_start[:, None], axis=1), N_EXPERTS - 1).astype(jnp.int32)
    n_used = (pad_end[-1] // bm).astype(jnp.int32).reshape(1)
    last_blk = jnp.where(padded > 0, pad_end // bm - 1, -1).astype(jnp.int32)
    nt = n // MOE_TM
    pos3 = jnp.transpose(pos.astype(jnp.int32).reshape(nt, MOE_TM, TOP_K), (0, 2, 1)).reshape(nt, 1, TOP_K * MOE_TM)
    return pos3, block_e, n_used, last_blk, padded_rows


def kernel(x, c, norm_mix_w, norm_ffn_w, w_ada, b_ada, w_in, conv_a_w, na_rpb, gdn_conv_w, gdn_a_log, gdn_dt_bias,
           gdn_norm_w, w_out, router_group_w, router_group_b, router_expert_w, router_expert_b, expert_w1,
           expert_w3, expert_w2, final_norm_w):
    bsz, t, d = x.shape
    depth = w_ada.shape[0]
    n = bsz * t
    mod = _ada(c, w_ada, b_ada)
    w_in_pad = jnp.pad(w_in.astype(BF16), ((0, 0), (0, 0), (0, D_IN_PAD - D_IN)))
    wabt = jnp.transpose(w_in[:, :, COL_AB:], (0, 2, 1)).astype(BF16)
    w_out_b = w_out.astype(BF16)
    wr = jnp.pad(jnp.concatenate([router_group_w, router_expert_w], axis=-1),
                 ((0, 0), (0, 0), (0, LANES - N_GROUPS - N_EXPERTS)))
    wr_hi = wr.astype(BF16)
    wr = jnp.stack([wr_hi, (wr - wr_hi.astype(F32)).astype(BF16)], axis=1)
    br = jnp.pad(jnp.concatenate([router_group_b, router_expert_b], axis=-1),
                 ((0, 0), (0, LANES - N_GROUPS - N_EXPERTS)))
    fw = final_norm_w.reshape(1, d)
    na_bias = _na_bias_table(na_rpb)
    for l in range(depth):
        sh1, sc1, g1, sh2, sc2, g2 = [mod[l, :, j * d:(j + 1) * d].reshape(bsz, 1, d) for j in range(6)]
        proj, abt = _inproj(x, norm_mix_w[l].reshape(1, d), sc1, sh1, w_in_pad, wabt, l)
        y_na = _na(proj, na_bias, l)
        o_f, o_b = _gdnscan(*_gdnchunk(proj, abt, gdn_conv_w[l], gdn_a_log[l], gdn_dt_bias[l]))
        x, hf, eidx, gates, hist = _outproj(x, conv_a_w[l], y_na, o_f, o_b, proj, gdn_norm_w[l].reshape(1, GDN_DV),
                                            w_out_b, g1, norm_ffn_w[l].reshape(1, d), sc2, sh2, wr[l],
                                            br[l].reshape(1, LANES), l)
        pos3, block_e, n_used, last_blk, padded_rows = _moe_plan(eidx.reshape(n, LANES),
                                                                 hist.reshape(-1, SUBLANES, LANES), min(ROW_TILE, t))
        xs = _dispatch(hf.reshape(n, d), pos3, last_blk, n_used, padded_rows)
        ys = _experts(xs, block_e, n_used, expert_w1, expert_w3, expert_w2, l)
        x = _combine(x.reshape(n, d), gates.reshape(n, LANES), g2, fw, ys, pos3, t,
                     final=(l == depth - 1)).reshape(bsz, t, d)
    return x
```

```python
import jax
import jax.numpy as jnp
from jax import lax
from jax.experimental import pallas as pl
from jax.experimental.pallas import tpu as pltpu

F32 = jnp.float32
BF16 = jnp.bfloat16

EPS = 1e-6
GRID_W = 64
CONV_W = 256
NA_HEADS = 4
NA_DH = 64
NA_W = NA_HEADS * NA_DH
NA_KH = 8
NA_KW = 16
GDN_HEADS = 4
GDN_DK = 128
GDN_DV = 128
GDN_W = GDN_HEADS * GDN_DV
GDN_QKV = 2 * GDN_HEADS * GDN_DK + GDN_W
GDN_CHUNK = 64
N_GROUPS = 4
EXPERTS_PER_GROUP = 8
N_EXPERTS = N_GROUPS * EXPERTS_PER_GROUP
TOP_K = 2

COL_CONV = 0
COL_NA = 3 * CONV_W
COL_GQKV = COL_NA + 3 * NA_W
COL_GZ = COL_GQKV + GDN_QKV
COL_AB = COL_GZ + GDN_W
D_IN = COL_AB + 4 * GDN_HEADS
D_IN_PAD = COL_AB + 128
LANES = 128
SUBLANES = 8

NEG = -1e30
VMEM_LIMIT = 56 * 1024 * 1024

MOE_BM = 512
ROW_TILE = 512


def _cparams(sem):
    return pltpu.CompilerParams(dimension_semantics=sem, vmem_limit_bytes=VMEM_LIMIT)


def _silu(x):
    return x * (1.0 / (1.0 + jnp.exp(-x)))


def _nt(a, b):
    return lax.dot_general(a, b, (((1,), (1,)), ((), ())), preferred_element_type=F32)


def _tn(a, b):
    return lax.dot_general(a, b, (((0,), (0,)), ((), ())), preferred_element_type=F32)


def _mm(a, b):
    return jnp.dot(a, b, preferred_element_type=F32)


def _ada_kernel(ct_ref, w_ref, b_ref, o_ref):
    w = w_ref[0]
    nb = ct_ref.shape[1]
    for r in range(nb):
        col = _silu(ct_ref[:, r:r + 1])
        o_ref[0, r:r + 1, :] = jnp.sum(w * col, axis=0, keepdims=True) + b_ref[0]


def _ada(c, w_ada, b_ada):
    depth, d, n6 = w_ada.shape
    bsz = c.shape[0]
    tn = 512
    return pl.pallas_call(
        _ada_kernel,
        grid=(depth, n6 // tn),
        in_specs=[pl.BlockSpec((d, bsz), lambda l, j: (0, 0)),
                  pl.BlockSpec((1, d, tn), lambda l, j: (l, 0, j)),
                  pl.BlockSpec((1, 1, tn), lambda l, j: (l, 0, j))],
        out_specs=pl.BlockSpec((1, bsz, tn), lambda l, j: (l, 0, j)),
        out_shape=jax.ShapeDtypeStruct((depth, bsz, n6), F32),
        compiler_params=_cparams(("parallel", "parallel")),
        name="ada",
    )(c.T, w_ada, b_ada.reshape(depth, 1, n6))


def _modnorm(x, nw, sc, sh):
    ms = jnp.mean(x * x, axis=-1, keepdims=True)
    y = x * lax.rsqrt(ms + EPS)
    return (y * nw) * (1.0 + sc) + sh


def _project(h, w_ref, wabt_ref, o_ref, ot_ref):
    o_ref[0] = _mm(h, w_ref[0])
    ot_ref[0] = _nt(wabt_ref[0], h)


def _inproj_kernel(x_ref, nw_ref, sc_ref, sh_ref, w_ref, wabt_ref, o_ref, ot_ref):
    h = _modnorm(x_ref[0], nw_ref[...], sc_ref[0], sh_ref[0]).astype(BF16)
    _project(h, w_ref, wabt_ref, o_ref, ot_ref)


def _inproj(x, nw, sc, sh, w_pad, wabt, layer):
    bsz, t, d = x.shape
    tm = min(ROW_TILE, t)
    nab = wabt.shape[1]
    return pl.pallas_call(
        _inproj_kernel,
        grid=(bsz, t // tm),
        in_specs=[pl.BlockSpec((1, tm, d), lambda b, i: (b, i, 0)),
                  pl.BlockSpec((1, d), lambda b, i: (0, 0)),
                  pl.BlockSpec((1, 1, d), lambda b, i: (b, 0, 0)),
                  pl.BlockSpec((1, 1, d), lambda b, i: (b, 0, 0)),
                  pl.BlockSpec((1, d, D_IN_PAD), lambda b, i: (layer, 0, 0)),
                  pl.BlockSpec((1, nab, d), lambda b, i: (layer, 0, 0))],
        out_specs=[pl.BlockSpec((1, tm, D_IN_PAD), lambda b, i: (b, i, 0)),
                   pl.BlockSpec((1, nab, tm), lambda b, i: (b, 0, i))],
        out_shape=[jax.ShapeDtypeStruct((bsz, t, D_IN_PAD), F32),
                   jax.ShapeDtypeStruct((bsz, nab, t), F32)],
        compiler_params=_cparams(("parallel", "parallel")),
        name="inproj",
    )(x, nw, sc, sh, w_pad, wabt)


def _dwconv3(u, prev_row, next_row, w_ref):
    tt = u.shape[0]
    row = lax.broadcasted_iota(jnp.int32, u.shape, 0)
    dn = jnp.where(row == 0, prev_row, pltpu.roll(u, 1, axis=0))
    up = jnp.where(row == tt - 1, next_row, pltpu.roll(u, tt - 1, axis=0))
    return w_ref[0:1, :] * dn + w_ref[1:2, :] * u + w_ref[2:3, :] * up


def _halo_specs(tt, t, width, colblk):
    nsub = tt // SUBLANES
    last = t // SUBLANES - 1
    return [pl.BlockSpec((1, tt, width), lambda b, i: (b, i, colblk)),
            pl.BlockSpec((1, SUBLANES, width), lambda b, i: (b, jnp.maximum(i * nsub - 1, 0), colblk)),
            pl.BlockSpec((1, SUBLANES, width), lambda b, i: (b, jnp.minimum((i + 1) * nsub, last), colblk))]


def _convmix_tile(m_ref, p_ref, n_ref, w_ref):
    i = pl.program_id(1)
    nt = pl.num_programs(1)
    m = m_ref[0]
    u = m[:, CONV_W:2 * CONV_W] * m[:, 2 * CONV_W:]
    p = p_ref[0]
    n = n_ref[0]
    pu = p[SUBLANES - 1:SUBLANES, CONV_W:2 * CONV_W] * p[SUBLANES - 1:SUBLANES, 2 * CONV_W:]
    nu = n[0:1, CONV_W:2 * CONV_W] * n[0:1, 2 * CONV_W:]
    pu = jnp.where(i == 0, 0.0, pu)
    nu = jnp.where(i == nt - 1, 0.0, nu)
    return m[:, :CONV_W] * _dwconv3(u, pu, nu, w_ref)


NA_RB = NA_KH // 2
NA_TOK = NA_RB * GRID_W
NA_KEYS = 3 * NA_TOK


def _na_bias_table(rpb):
    col = jnp.arange(GRID_W)
    cstart = jnp.clip(col - NA_KW // 2, 0, GRID_W - NA_KW)
    kc = jnp.arange(GRID_W)
    valid = (kc[None, :] >= cstart[:, None]) & (kc[None, :] < cstart[:, None] + NA_KW)
    dc = kc[None, :] - col[:, None] + (NA_KW - 1)
    onehot = (dc[None] == jnp.arange(2 * NA_KW - 1)[:, None, None]) & valid[None]
    cols = jnp.einsum('lhrd,dck->lhrck', rpb, onehot.astype(F32), precision=lax.Precision.HIGHEST)
    cols = jnp.where(valid, cols, NEG)
    lo = NA_KH - 1 - NA_RB
    blk = jnp.stack([cols[:, :, lo - j:lo - j + 3 * NA_RB] for j in range(NA_RB)], axis=2)
    blk = jnp.transpose(blk, (0, 1, 2, 4, 3, 5))
    return blk.reshape(rpb.shape[0], NA_HEADS, NA_TOK, NA_KEYS)


def _na_kernel(q_ref, kp_ref, kc_ref, kn_ref, vp_ref, vc_ref, vn_ref, bias_ref, o_ref, *, rows):
    i = pl.program_id(1)
    kbuf = jnp.concatenate([kp_ref[0], kc_ref[0], kn_ref[0]], axis=0).astype(BF16)
    vbuf = jnp.concatenate([vp_ref[0], vc_ref[0], vn_ref[0]], axis=0).astype(BF16)
    q = q_ref[0] * (NA_DH ** -0.5)
    head_of_lane = lax.broadcasted_iota(jnp.int32, (1, NA_W), 1) // NA_DH
    qrow = i * NA_RB + lax.broadcasted_iota(jnp.int32, (NA_TOK, NA_KEYS), 0) // GRID_W
    krow = (i - 1) * NA_RB + lax.broadcasted_iota(jnp.int32, (NA_TOK, NA_KEYS), 1) // GRID_W
    rs = jnp.clip(qrow - NA_KH // 2, 0, rows - NA_KH)
    row_mask = jnp.where((krow >= rs) & (krow < rs + NA_KH), 0.0, NEG)
    acc = jnp.zeros((NA_TOK, NA_W), F32)
    for h in range(NA_HEADS):
        mine = head_of_lane == h
        s = _nt(jnp.where(mine, q, 0.0).astype(BF16), kbuf) + (bias_ref[0, h] + row_mask)
        m = jnp.max(s, axis=-1, keepdims=True)
        p = jnp.exp(s - m)
        l = jnp.sum(p, axis=-1, keepdims=True)
        o = _mm((p / l).astype(BF16), vbuf)
        acc = acc + jnp.where(mine, o, 0.0)
    o_ref[0] = acc


def _na(proj, bias_tbl, layer):
    bsz, t, _ = proj.shape
    rows = t // GRID_W
    nblk = rows // NA_RB
    qc, kc, vc = COL_NA // NA_W, COL_NA // NA_W + 1, COL_NA // NA_W + 2

    def spec(col, shift):
        return pl.BlockSpec((1, NA_TOK, NA_W), lambda b, i: (b, jnp.clip(i + shift, 0, nblk - 1), col))

    def na_kernel(*refs):
        _na_kernel(*refs, rows=rows)

    return pl.pallas_call(
        na_kernel,
        grid=(bsz, nblk),
        in_specs=[spec(qc, 0), spec(kc, -1), spec(kc, 0), spec(kc, 1), spec(vc, -1), spec(vc, 0), spec(vc, 1),
                  pl.BlockSpec((1, NA_HEADS, NA_TOK, NA_KEYS), lambda b, i: (layer, 0, 0, 0))],
        out_specs=pl.BlockSpec((1, NA_TOK, NA_W), lambda b, i: (b, i, 0)),
        out_shape=jax.ShapeDtypeStruct((bsz, t, NA_W), F32),
        compiler_params=_cparams(("parallel", "parallel")),
        name="na",
    )(proj, proj, proj, proj, proj, proj, proj, bias_tbl)


def _gdn_qkv(m_ref, p_ref, n_ref, w_ref):
    i = pl.program_id(1)
    nt = pl.num_programs(1)
    pu = jnp.where(i == 0, 0.0, p_ref[0, SUBLANES - 1:SUBLANES, :])
    nu = jnp.where(i == nt - 1, 0.0, n_ref[0, 0:1, :])
    c = _silu(_dwconv3(m_ref[0], pu, nu, w_ref))
    heads = []
    for hh in range(GDN_QKV // GDN_DK):
        xh = c[:, hh * GDN_DK:(hh + 1) * GDN_DK]
        if hh < 2 * GDN_HEADS:
            xh = xh * lax.rsqrt(jnp.sum(xh * xh, axis=-1, keepdims=True) + EPS)
            if hh < GDN_HEADS:
                xh = xh * (GDN_DK ** -0.5)
        heads.append(xh)
    return heads[:GDN_HEADS], heads[GDN_HEADS:2 * GDN_HEADS], heads[2 * GDN_HEADS:]


GDN_GS = 128
GDN_UNITS = 2 * GDN_HEADS
GDN_GC = GDN_GS // GDN_CHUNK


def _softplus(x):
    return jnp.maximum(x, 0.0) + jnp.log1p(jnp.exp(-jnp.abs(x)))


def _seg_cumsum(x, axis, reverse):
    n = x.shape[axis]
    pos = lax.broadcasted_iota(jnp.int32, x.shape, axis) & (GDN_CHUNK - 1)
    s = 1
    while s < GDN_CHUNK:
        if reverse:
            x = x + jnp.where(pos < GDN_CHUNK - s, pltpu.roll(x, n - s, axis=axis), 0.0)
        else:
            x = x + jnp.where(pos >= s, pltpu.roll(x, s, axis=axis), 0.0)
        s *= 2
    return x


def _gdnchunk_kernel(m_ref, p_ref, n_ref, cw_ref, ab_ref, abt_ref, alr_ref, dtr_ref, alc_ref, dtc_ref,
                     u_ref, wq_ref, kd_ref, in_ref, gl_ref):
    gs, c, nh = GDN_GS, GDN_CHUNK, GDN_HEADS
    ri = lax.broadcasted_iota(jnp.int32, (gs, gs), 0)
    ci = lax.broadcasted_iota(jnp.int32, (gs, gs), 1)
    same = (ri // c) == (ci // c)
    eye = (ri == ci).astype(F32)
    rowc = lax.broadcasted_iota(jnp.int32, (gs, 1), 0) // c
    ab = ab_ref[0]
    abt = abt_ref[0]
    graw_c = -jnp.exp(alr_ref[...]) * _softplus(ab + dtr_ref[...])
    graw_r = -jnp.exp(alc_ref[...]) * _softplus(abt[0:2 * nh] + dtc_ref[...])
    beta_c = 1.0 / (1.0 + jnp.exp(-ab))
    g_col = [_seg_cumsum(graw_c, 0, False), _seg_cumsum(graw_c, 0, True)]
    g_row = [_seg_cumsum(graw_r, 1, False), _seg_cumsum(graw_r, 1, True)]
    incl = [same & (ri >= ci), same & (ri <= ci)]
    strict = [same & (ri > ci), same & (ri < ci)]

    qs, ks, vs = _gdn_qkv(m_ref, p_ref, n_ref, cw_ref)
    grams = [_nt(jnp.concatenate([q, k], axis=0).astype(BF16), k.astype(BF16)) for q, k in zip(qs, ks)]

    units = [(d, h) for d in range(2) for h in range(nh)]
    gcs, bcs, intras, xs, ps = [], [], [], [], []
    for d, h in units:
        col = d * nh + h
        gc = g_col[d][:, col:col + 1]
        gr = g_row[d][col:col + 1, :]
        bc = beta_c[:, 2 * nh + col:2 * nh + col + 1]
        e_incl = jnp.exp(jnp.where(incl[d], gc - gr, NEG))
        a = grams[h][gs:] * bc * jnp.where(strict[d], e_incl, 0.0)
        gcs.append(gc), bcs.append(bc)
        intras.append(grams[h][:gs] * e_incl)
        xs.append(eye - a), ps.append(a)
    for _ in range(5):
        pbs = [p.astype(BF16) for p in ps]
        ps = [_mm(pb, pb) for pb in pbs]
        xs = [x + _mm(x.astype(BF16), p.astype(BF16)) for x, p in zip(xs, ps)]
    egs = [jnp.exp(gc) for gc in gcs]
    sols = [_mm(x.astype(BF16),
                jnp.concatenate([vs[h] * bc, ks[h] * (bc * eg)], axis=1).astype(BF16))
            for (d, h), x, bc, eg in zip(units, xs, bcs, egs)]
    for (d, h), gc, eg, sol, intra in zip(units, gcs, egs, sols, intras):
        col = d * nh + h
        glast_col = jnp.zeros_like(gc)
        for n in range(GDN_GC):
            r = n * c if d == 1 else (n + 1) * c - 1
            glast = gc[r:r + 1, :]
            glast_col = jnp.where(rowc == n, glast, glast_col)
            gl_ref[0, 0, col * GDN_GC + n:col * GDN_GC + n + 1, :] = jnp.broadcast_to(jnp.exp(glast), (1, LANES))
        u_ref[0, col] = sol[:, :GDN_DV]
        w = sol[:, GDN_DV:].astype(BF16)
        qd = (qs[h] * eg).astype(BF16)
        for n in range(GDN_GC):
            wq_ref[0, col, 2 * n * c:(2 * n + 1) * c, :] = w[n * c:(n + 1) * c]
            wq_ref[0, col, (2 * n + 1) * c:(2 * n + 2) * c, :] = qd[n * c:(n + 1) * c]
        kd_ref[0, col] = (ks[h] * jnp.exp(glast_col - gc)).astype(BF16)
        in_ref[0, col] = jnp.concatenate([intra[n * c:(n + 1) * c, n * c:(n + 1) * c] for n in range(GDN_GC)],
                                         axis=0).astype(BF16)


def _gdnchunk(proj, abt, conv_w, a_log, dt_bias):
    bsz, t, _ = proj.shape
    gs, nu = GDN_GS, GDN_UNITS
    ng = t // gs
    nab = abt.shape[1]
    pad = lambda r: jnp.pad(r.reshape(1, -1), ((0, 0), (0, LANES - r.size)))
    alr, dtr = pad(a_log), pad(dt_bias)
    alc, dtc = a_log.reshape(-1, 1), dt_bias.reshape(-1, 1)
    small = lambda shape: pl.BlockSpec(shape, lambda b, i: (0, 0))
    unit = lambda rows, w: pl.BlockSpec((1, nu, rows, w), lambda b, i: (b, 0, i, 0))
    return pl.pallas_call(
        _gdnchunk_kernel,
        grid=(bsz, ng),
        in_specs=_halo_specs(gs, t, GDN_QKV, COL_GQKV // GDN_QKV) + [
            small((3, GDN_QKV)),
            pl.BlockSpec((1, gs, LANES), lambda b, i: (b, i, COL_AB // LANES)),
            pl.BlockSpec((1, nab, gs), lambda b, i: (b, 0, i)),
            small((1, LANES)), small((1, LANES)), small((nu, 1)), small((nu, 1))],
        out_specs=[unit(gs, GDN_DV), unit(2 * gs, GDN_DK), unit(gs, GDN_DK), unit(gs, GDN_CHUNK),
                   pl.BlockSpec((1, 1, nu * GDN_GC, LANES), lambda b, i: (b, i, 0, 0))],
        out_shape=[jax.ShapeDtypeStruct((bsz, nu, t, GDN_DV), F32),
                   jax.ShapeDtypeStruct((bsz, nu, 2 * t, GDN_DK), BF16),
                   jax.ShapeDtypeStruct((bsz, nu, t, GDN_DK), BF16),
                   jax.ShapeDtypeStruct((bsz, nu, t, GDN_CHUNK), BF16),
                   jax.ShapeDtypeStruct((bsz, ng, nu * GDN_GC, LANES), F32)],
        compiler_params=_cparams(("parallel", "parallel")),
        name="gdnchunk",
    )(proj, proj, proj, conv_w, proj, abt, alr, dtr, alc, dtc)


GDN_SB = 4
GDN_ST = GDN_SB * GDN_CHUNK


def _gdnscan_kernel(uf, ub, wqf, wqb, kdf, kdb, inf, inb, glf, glb, of_ref, ob_ref, s_ref):
    i = pl.program_id(0)

    @pl.when(i == 0)
    def _():
        s_ref[...] = jnp.zeros_like(s_ref)

    bsz = uf.shape[0]
    c, nh = GDN_CHUNK, GDN_HEADS
    chains = [(b, d, h) for b in range(bsz) for d in range(2) for h in range(nh)]
    for step in range(GDN_SB):
        zs, vns = [], []
        for b, d, h in chains:
            cc = step if d == 0 else GDN_SB - 1 - step
            wq = (wqf, wqb)[d]
            st = s_ref[(b * 2 + d) * nh + h]
            zs.append(_mm(wq[b, h, 2 * cc * c:(2 * cc + 2) * c, :], st.astype(BF16)))
        for (b, d, h), z in zip(chains, zs):
            cc = step if d == 0 else GDN_SB - 1 - step
            u = (uf, ub)[d]
            vns.append((u[b, h, cc * c:(cc + 1) * c, :] - z[:c]).astype(BF16))
        for (b, d, h), z, vn in zip(chains, zs, vns):
            cc = step if d == 0 else GDN_SB - 1 - step
            rows = slice(cc * c, (cc + 1) * c)
            intra = (inf, inb)[d]
            o_ref = (of_ref, ob_ref)[d]
            o_ref[b, rows, h * GDN_DV:(h + 1) * GDN_DV] = z[c:] + _mm(intra[b, h, rows, :], vn)
        for (b, d, h), vn in zip(chains, vns):
            cc = step if d == 0 else GDN_SB - 1 - step
            rows = slice(cc * c, (cc + 1) * c)
            kd = (kdf, kdb)[d]
            gl = (glf, glb)[d]
            r = (d * nh + h) * GDN_GC + cc % GDN_GC
            sidx = (b * 2 + d) * nh + h
            s_ref[sidx] = s_ref[sidx] * gl[b, cc // GDN_GC, r:r + 1, :] + _tn(kd[b, h, rows, :], vn)


def _gdnscan(u, wq, kd, intra, gl):
    bsz, nu, t, _ = u.shape
    nh = GDN_HEADS
    st = GDN_ST
    nb = t // st
    ngs = st // GDN_GS

    def unit(rows, w, d):
        if d == 0:
            return pl.BlockSpec((bsz, nh, rows, w), lambda i: (0, 0, i, 0))
        return pl.BlockSpec((bsz, nh, rows, w), lambda i: (0, 1, nb - 1 - i, 0))

    glspec = lambda d: pl.BlockSpec((bsz, ngs, nu * GDN_GC, LANES),
                                    (lambda i: (0, i, 0, 0)) if d == 0 else (lambda i: (0, nb - 1 - i, 0, 0)))
    return pl.pallas_call(
        _gdnscan_kernel,
        grid=(nb,),
        in_specs=[unit(st, GDN_DV, 0), unit(st, GDN_DV, 1), unit(2 * st, GDN_DK, 0), unit(2 * st, GDN_DK, 1),
                  unit(st, GDN_DK, 0), unit(st, GDN_DK, 1), unit(st, GDN_CHUNK, 0), unit(st, GDN_CHUNK, 1),
                  glspec(0), glspec(1)],
        out_specs=[pl.BlockSpec((bsz, st, GDN_W), lambda i: (0, i, 0)),
                   pl.BlockSpec((bsz, st, GDN_W), lambda i: (0, nb - 1 - i, 0))],
        out_shape=[jax.ShapeDtypeStruct((bsz, t, GDN_W), F32), jax.ShapeDtypeStruct((bsz, t, GDN_W), F32)],
        scratch_shapes=[pltpu.VMEM((bsz * nu, GDN_DK, GDN_DV), F32)],
        compiler_params=_cparams(("arbitrary",)),
        name="gdnscan",
    )(u, u, wq, wq, kd, kd, intra, intra, gl, gl)


def _outproj_kernel(x_ref, cm_ref, cp_ref, cn_ref, cw_ref, yn_ref, of_ref, ob_ref, z_ref, gnw_ref, wo_ref, g1_ref,
                    nw_ref, sc_ref, sh_ref, wr_ref, br_ref, xo_ref, h_ref, e_ref, g_ref, hist_ref):
    o = of_ref[0] + ob_ref[0]
    z = z_ref[0]
    parts = [_convmix_tile(cm_ref, cp_ref, cn_ref, cw_ref), yn_ref[0]]
    for h in range(GDN_HEADS):
        sl = slice(h * GDN_DV, (h + 1) * GDN_DV)
        oh = o[:, sl]
        oh = oh * lax.rsqrt(jnp.mean(oh * oh, axis=-1, keepdims=True) + EPS) * gnw_ref[...]
        parts.append(oh * _silu(z[:, sl]))
    mixed = _mm(jnp.concatenate(parts, axis=-1).astype(BF16), wo_ref[0])
    xn = x_ref[0] + g1_ref[0] * mixed
    xo_ref[0] = xn
    hf = _modnorm(xn, nw_ref[...], sc_ref[0], sh_ref[0])
    h_ref[0] = hf
    hf_hi = hf.astype(BF16)
    hf_lo = (hf - hf_hi.astype(F32)).astype(BF16)
    logits = (_mm(hf_hi, wr_ref[0]) + (_mm(hf_hi, wr_ref[1]) + _mm(hf_lo, wr_ref[0]))) + br_ref[...]
    lane = lax.broadcasted_iota(jnp.int32, logits.shape, 1)
    gl = jnp.where(lane < N_GROUPS, logits, NEG)
    gm = jnp.max(gl, axis=-1, keepdims=True)
    den = jnp.sum(jnp.exp(gl - gm), axis=-1, keepdims=True)
    grp = jnp.min(jnp.where(gl == gm, lane, LANES), axis=-1, keepdims=True)
    pg_top = 1.0 / den
    ex = lane - N_GROUPS
    in_grp = (ex >= grp * EXPERTS_PER_GROUP) & (ex < (grp + 1) * EXPERTS_PER_GROUP)
    el = jnp.where(in_grp, logits, NEG)
    m1 = jnp.max(el, axis=-1, keepdims=True)
    i1 = jnp.min(jnp.where(el == m1, lane, LANES), axis=-1, keepdims=True)
    el2 = jnp.where(lane == i1, NEG, el)
    m2 = jnp.max(el2, axis=-1, keepdims=True)
    i2 = jnp.min(jnp.where(el2 == m2, lane, LANES), axis=-1, keepdims=True)
    e2 = jnp.exp(m2 - m1)
    w1 = pg_top / (1.0 + e2)
    w2 = pg_top * e2 / (1.0 + e2)
    g_ref[0] = jnp.where(lane == 0, w1, jnp.where(lane == 1, w2, 0.0))
    oh1 = (lane == i1 - N_GROUPS).astype(F32)
    oh2 = (lane == i2 - N_GROUPS).astype(F32)
    both = oh1 + oh2
    tm = logits.shape[0]
    earlier = (lax.broadcasted_iota(jnp.int32, (tm, tm), 0) > lax.broadcasted_iota(jnp.int32, (tm, tm), 1))
    cnt = _mm(earlier.astype(BF16), both.astype(BF16))
    r1 = jnp.sum(cnt * oh1, axis=-1, keepdims=True).astype(jnp.int32)
    r2 = jnp.sum(cnt * oh2, axis=-1, keepdims=True).astype(jnp.int32)
    e_ref[0] = jnp.where(lane == 0, i1 - N_GROUPS, jnp.where(lane == 1, i2 - N_GROUPS,
                         jnp.where(lane == 2, r1, jnp.where(lane == 3, r2, 0))))
    hist_ref[0, 0] = jnp.broadcast_to(jnp.sum(both, axis=0, keepdims=True), (SUBLANES, LANES))


def _outproj(x, conv_w, yn, of, ob, proj, gnw, wo, g1, nw, sc, sh, wr, br, layer):
    bsz, t, d = x.shape
    tm = min(ROW_TILE, t)
    tok = lambda w: pl.BlockSpec((1, tm, w), lambda b, i: (b, i, 0))
    perb = pl.BlockSpec((1, 1, d), lambda b, i: (b, 0, 0))
    full = lambda shape: pl.BlockSpec(shape, lambda b, i: (0, 0))
    return pl.pallas_call(
        _outproj_kernel,
        grid=(bsz, t // tm),
        in_specs=[tok(d)] + _halo_specs(tm, t, 3 * CONV_W, COL_CONV // (3 * CONV_W)) + [
                  full((3, CONV_W)), tok(NA_W), tok(GDN_W), tok(GDN_W),
                  pl.BlockSpec((1, tm, GDN_W), lambda b, i: (b, i, COL_GZ // GDN_W)),
                  full((1, GDN_DV)), pl.BlockSpec((1, d, d), lambda b, i: (layer, 0, 0)), perb, full((1, d)), perb, perb,
                  pl.BlockSpec((2, d, LANES), lambda b, i: (0, 0, 0)), full((1, LANES))],
        out_specs=[tok(d), tok(d), tok(LANES), tok(LANES),
                   pl.BlockSpec((1, 1, SUBLANES, LANES), lambda b, i: (b, i, 0, 0))],
        out_shape=[jax.ShapeDtypeStruct((bsz, t, d), F32), jax.ShapeDtypeStruct((bsz, t, d), F32),
                   jax.ShapeDtypeStruct((bsz, t, LANES), jnp.int32), jax.ShapeDtypeStruct((bsz, t, LANES), F32),
                   jax.ShapeDtypeStruct((bsz, t // tm, SUBLANES, LANES), F32)],
        compiler_params=_cparams(("parallel", "parallel")),
        name="outproj",
    )(x, proj, proj, proj, conv_w, yn, of, ob, proj, gnw, wo, g1, nw, sc, sh, wr, br)


MOE_TM = 256


def _dispatch_kernel(lb_ref, nu_ref, pos_ref, h_ref, xs_out, zbuf, sem, zsem):
    i = pl.program_id(0)
    tm = h_ref.shape[0]
    bm = zbuf.shape[0]
    nblk = xs_out.shape[0] // bm

    def zero_copy(blk):
        return pltpu.make_async_copy(zbuf, xs_out.at[pl.ds(pl.multiple_of(blk * bm, bm), bm), :], zsem)

    @pl.when(i == 0)
    def _():
        zbuf[...] = jnp.zeros_like(zbuf)
        for e in range(N_EXPERTS):
            @pl.when(lb_ref[e] >= 0)
            def _():
                zero_copy(lb_ref[e]).start()

        def start_trailing(j, carry):
            zero_copy(j).start()
            return carry

        def wait_one(j, carry):
            zero_copy(0).wait()
            return carry

        lax.fori_loop(nu_ref[0], nblk, start_trailing, 0)
        lax.fori_loop(nu_ref[0], nblk, wait_one, 0)
        for e in range(N_EXPERTS):
            @pl.when(lb_ref[e] >= 0)
            def _():
                zero_copy(0).wait()

    def row_copy(r, k):
        dst = pos_ref[0, 0, k * tm + r]
        return pltpu.make_async_copy(h_ref.at[pl.ds(r, 1), :], xs_out.at[pl.ds(dst, 1), :], sem)

    def body(r, carry):
        for k in range(TOP_K):
            row_copy(r, k).start()
        return carry

    lax.fori_loop(0, tm, body, 0, unroll=8)
    for k in range(TOP_K):
        pltpu.make_async_copy(h_ref, xs_out.at[pl.ds(0, tm), :], sem).wait()


def _dispatch(hf, pos3, last_blk, n_used, padded_rows):
    n, d = hf.shape
    tm = MOE_TM
    nt = n // tm
    grid_spec = pltpu.PrefetchScalarGridSpec(
        num_scalar_prefetch=2,
        grid=(nt,),
        in_specs=[pl.BlockSpec((1, 1, TOP_K * tm), lambda i, lb, nu: (i, 0, 0), memory_space=pltpu.SMEM),
                  pl.BlockSpec((tm, d), lambda i, lb, nu: (i, 0))],
        out_specs=pl.BlockSpec(memory_space=pl.ANY),
        scratch_shapes=[pltpu.VMEM((MOE_BM, d), F32), pltpu.SemaphoreType.DMA(()), pltpu.SemaphoreType.DMA(())])
    return pl.pallas_call(
        _dispatch_kernel,
        grid_spec=grid_spec,
        out_shape=jax.ShapeDtypeStruct((padded_rows, d), F32),
        compiler_params=_cparams(("arbitrary",)),
        name="dispatch",
    )(last_blk, n_used, pos3, hf)


def _experts_kernel(be_ref, nu_ref, x_ref, w1_ref, w3_ref, w2_ref, o_ref, w1b, w3b, w2b):
    i = pl.program_id(0)
    used = nu_ref[0]

    @pl.when((i == 0) | (be_ref[i] != be_ref[jnp.maximum(i - 1, 0)]))
    def _():
        w1b[...] = w1_ref[0, 0].astype(BF16)
        w3b[...] = w3_ref[0, 0].astype(BF16)
        w2b[...] = w2_ref[0, 0].astype(BF16)

    @pl.when(i < used)
    def _():
        xb = x_ref[...].astype(BF16)
        act = (_silu(_mm(xb, w1b[...])) * _mm(xb, w3b[...])).astype(BF16)
        o_ref[...] = _mm(act, w2b[...])

    @pl.when(i >= used)
    def _():
        o_ref[...] = jnp.zeros_like(o_ref)


def _experts(xs, block_e, n_used, w1, w3, w2, layer):
    padded_rows, d = xs.shape
    bm = MOE_BM
    nblk = padded_rows // bm
    de = w1.shape[-1]
    grid_spec = pltpu.PrefetchScalarGridSpec(
        num_scalar_prefetch=2,
        grid=(nblk,),
        in_specs=[pl.BlockSpec((bm, d), lambda i, be, nu: (jnp.maximum(jnp.minimum(i, nu[0] - 1), 0), 0)),
                  pl.BlockSpec((1, 1, d, de), lambda i, be, nu: (layer, be[i], 0, 0)),
                  pl.BlockSpec((1, 1, d, de), lambda i, be, nu: (layer, be[i], 0, 0)),
                  pl.BlockSpec((1, 1, de, d), lambda i, be, nu: (layer, be[i], 0, 0))],
        out_specs=pl.BlockSpec((bm, d), lambda i, be, nu: (i, 0)),
        scratch_shapes=[pltpu.VMEM((d, de), BF16), pltpu.VMEM((d, de), BF16), pltpu.VMEM((de, d), BF16)])
    return pl.pallas_call(
        _experts_kernel,
        grid_spec=grid_spec,
        out_shape=jax.ShapeDtypeStruct((padded_rows, d), F32),
        compiler_params=_cparams(("arbitrary",)),
        name="experts",
    )(block_e, n_used, xs, w1, w3, w2)


def _combine_rows(pc_ref, pn_ref, x_ref, gt_ref, g2_ref, ys_hbm, ybuf, sem):
    i = pl.program_id(0)
    nt = pl.num_programs(0)
    tm = x_ref.shape[0]

    def start_gather(idx_ref, slot):
        def body(r, carry):
            pltpu.make_async_copy(ys_hbm.at[pl.ds(idx_ref[0, 0, r], 1), :], ybuf.at[slot, pl.ds(r, 1), :],
                                  sem.at[slot]).start()
            return carry
        lax.fori_loop(0, TOP_K * tm, body, 0, unroll=8)

    slot = i % 2

    @pl.when(i == 0)
    def _():
        start_gather(pc_ref, 0)

    @pl.when(i + 1 < nt)
    def _():
        start_gather(pn_ref, 1 - slot)

    pltpu.make_async_copy(ys_hbm.at[pl.ds(0, TOP_K * tm), :], ybuf.at[slot], sem.at[slot]).wait()
    gt = gt_ref[...]
    y = gt[:, 0:1] * ybuf[slot, 0:tm, :] + gt[:, 1:2] * ybuf[slot, tm:2 * tm, :]
    return x_ref[...] + g2_ref[0] * y


def _combine_final_kernel(pc_ref, pn_ref, x_ref, gt_ref, g2_ref, fw_ref, ys_hbm, o_ref, ybuf, sem):
    xn = _combine_rows(pc_ref, pn_ref, x_ref, gt_ref, g2_ref, ys_hbm, ybuf, sem)
    ms = jnp.mean(xn * xn, axis=-1, keepdims=True)
    o_ref[...] = xn * lax.rsqrt(ms + EPS) * fw_ref[...]


def _combine_inproj_kernel(pc_ref, pn_ref, x_ref, gt_ref, g2_ref, nw_ref, sc_ref, sh_ref, w_ref, wabt_ref, ys_hbm,
                           xo_ref, o_ref, ot_ref, ybuf, sem):
    xn = _combine_rows(pc_ref, pn_ref, x_ref, gt_ref, g2_ref, ys_hbm, ybuf, sem)
    xo_ref[...] = xn
    h = _modnorm(xn, nw_ref[...], sc_ref[0], sh_ref[0]).astype(BF16)
    _project(h, w_ref, wabt_ref, o_ref, ot_ref)


def _combine_specs(n, d, t):
    tm = MOE_TM
    nt = n // tm
    per_b = t // tm
    specs = [pl.BlockSpec((1, 1, TOP_K * tm), lambda i: (i, 0, 0), memory_space=pltpu.SMEM),
             pl.BlockSpec((1, 1, TOP_K * tm), lambda i: (jnp.minimum(i + 1, nt - 1), 0, 0), memory_space=pltpu.SMEM),
             pl.BlockSpec((tm, d), lambda i: (i, 0)),
             pl.BlockSpec((tm, LANES), lambda i: (i, 0)),
             pl.BlockSpec((1, 1, d), lambda i: (i // per_b, 0, 0))]
    scratch = [pltpu.VMEM((2, TOP_K * tm, d), F32), pltpu.SemaphoreType.DMA((2,))]
    return tm, nt, per_b, specs, scratch


def _combine_final(x2, gates, g2, fw, ys, pos3, t):
    n, d = x2.shape
    tm, nt, per_b, specs, scratch = _combine_specs(n, d, t)
    return pl.pallas_call(
        _combine_final_kernel,
        grid=(nt,),
        in_specs=specs + [pl.BlockSpec((1, d), lambda i: (0, 0)), pl.BlockSpec(memory_space=pl.ANY)],
        out_specs=pl.BlockSpec((tm, d), lambda i: (i, 0)),
        out_shape=jax.ShapeDtypeStruct((n, d), F32),
        scratch_shapes=scratch,
        compiler_params=_cparams(("arbitrary",)),
        name="combine",
    )(pos3, pos3, x2, gates, g2, fw, ys)


def _combine_inproj(x2, gates, g2, ys, pos3, t, nw, sc, sh, w_pad, wabt, layer):
    n, d = x2.shape
    bsz = n // t
    nab = wabt.shape[1]
    tm, nt, per_b, specs, scratch = _combine_specs(n, d, t)
    perb = pl.BlockSpec((1, 1, d), lambda i: (i // per_b, 0, 0))
    return pl.pallas_call(
        _combine_inproj_kernel,
        grid=(nt,),
        in_specs=specs + [pl.BlockSpec((1, d), lambda i: (0, 0)), perb, perb,
                          pl.BlockSpec((1, d, D_IN_PAD), lambda i: (layer, 0, 0)),
                          pl.BlockSpec((1, nab, d), lambda i: (layer, 0, 0)),
                          pl.BlockSpec(memory_space=pl.ANY)],
        out_specs=[pl.BlockSpec((tm, d), lambda i: (i, 0)),
                   pl.BlockSpec((1, tm, D_IN_PAD), lambda i: (i // per_b, i % per_b, 0)),
                   pl.BlockSpec((1, nab, tm), lambda i: (i // per_b, 0, i % per_b))],
        out_shape=[jax.ShapeDtypeStruct((n, d), F32), jax.ShapeDtypeStruct((bsz, t, D_IN_PAD), F32),
                   jax.ShapeDtypeStruct((bsz, nab, t), F32)],
        scratch_shapes=scratch,
        compiler_params=_cparams(("arbitrary",)),
        name="combine_inproj",
    )(pos3, pos3, x2, gates, g2, nw, sc, sh, w_pad, wabt, ys)


def _moe_plan(eidx, hist, rank_tile):
    n = eidx.shape[0]
    bm = MOE_BM
    hist = hist[:, 0, :N_EXPERTS].astype(jnp.int32)
    sizes = jnp.sum(hist, axis=0)
    base = jnp.cumsum(hist, axis=0) - hist
    padded = (sizes + bm - 1) // bm * bm
    pad_end = jnp.cumsum(padded)
    tbl = (pad_end - padded)[None, :] + base
    e = eidx[:, :TOP_K].reshape(-1, rank_tile, TOP_K)
    onehot = e[..., None] == jnp.arange(N_EXPERTS, dtype=jnp.int32)
    pos = jnp.sum(jnp.where(onehot, tbl[:, None, None, :], 0), axis=-1).reshape(n, TOP_K) + eidx[:, TOP_K:2 * TOP_K]
    padded_rows = (n * TOP_K + N_EXPERTS * (bm - 1) + bm - 1) // bm * bm
    n_blocks = padded_rows // bm
    blk_start = jnp.arange(n_blocks, dtype=jnp.int32) * bm
    block_e = jnp.minimum(jnp.sum(pad_end[None, :] <= blk_start[:, None], axis=1), N_EXPERTS - 1).astype(jnp.int32)
    n_used = (pad_end[-1] // bm).astype(jnp.int32).reshape(1)
    last_blk = jnp.where(padded > 0, pad_end // bm - 1, -1).astype(jnp.int32)
    nt = n // MOE_TM
    pos3 = jnp.transpose(pos.astype(jnp.int32).reshape(nt, MOE_TM, TOP_K), (0, 2, 1)).reshape(nt, 1, TOP_K * MOE_TM)
    return pos3, block_e, n_used, last_blk, padded_rows


def kernel(x, c, norm_mix_w, norm_ffn_w, w_ada, b_ada, w_in, conv_a_w, na_rpb, gdn_conv_w, gdn_a_log, gdn_dt_bias,
           gdn_norm_w, w_out, router_group_w, router_group_b, router_expert_w, router_expert_b, expert_w1,
           expert_w3, expert_w2, final_norm_w):
    bsz, t, d = x.shape
    depth = w_ada.shape[0]
    n = bsz * t
    mod = _ada(c, w_ada, b_ada)
    w_in_pad = jnp.pad(w_in.astype(BF16), ((0, 0), (0, 0), (0, D_IN_PAD - D_IN)))
    wabt = jnp.transpose(w_in[:, :, COL_AB:], (0, 2, 1)).astype(BF16)
    w_out_b = w_out.astype(BF16)
    wr = jnp.pad(jnp.concatenate([router_group_w, router_expert_w], axis=-1),
                 ((0, 0), (0, 0), (0, LANES - N_GROUPS - N_EXPERTS)))
    wr_hi = wr.astype(BF16)
    wr = jnp.stack([wr_hi, (wr - wr_hi.astype(F32)).astype(BF16)], axis=1)
    br = jnp.pad(jnp.concatenate([router_group_b, router_expert_b], axis=-1),
                 ((0, 0), (0, LANES - N_GROUPS - N_EXPERTS)))
    fw = final_norm_w.reshape(1, d)
    na_bias = _na_bias_table(na_rpb)
    mods = [[mod[l, :, j * d:(j + 1) * d].reshape(bsz, 1, d) for j in range(6)] for l in range(depth)]
    proj, abt = _inproj(x, norm_mix_w[0].reshape(1, d), mods[0][1], mods[0][0], w_in_pad, wabt, 0)
    for l in range(depth):
        sh1, sc1, g1, sh2, sc2, g2 = mods[l]
        y_na = _na(proj, na_bias, l)
        o_f, o_b = _gdnscan(*_gdnchunk(proj, abt, gdn_conv_w[l], gdn_a_log[l], gdn_dt_bias[l]))
        x, hf, eidx, gates, hist = _outproj(x, conv_a_w[l], y_na, o_f, o_b, proj, gdn_norm_w[l].reshape(1, GDN_DV),
                                            w_out_b, g1, norm_ffn_w[l].reshape(1, d), sc2, sh2, wr[l],
                                            br[l].reshape(1, LANES), l)
        pos3, block_e, n_used, last_blk, padded_rows = _moe_plan(eidx.reshape(n, LANES),
                                                                 hist.reshape(-1, SUBLANES, LANES), min(ROW_TILE, t))
        xs = _dispatch(hf.reshape(n, d), pos3, last_blk, n_used, padded_rows)
        ys = _experts(xs, block_e, n_used, expert_w1, expert_w3, expert_w2, l)
        x2, gt = x.reshape(n, d), gates.reshape(n, LANES)
        if l + 1 < depth:
            x2, proj, abt = _combine_inproj(x2, gt, g2, ys, pos3, t, norm_mix_w[l + 1].reshape(1, d), mods[l + 1][1],
                                            mods[l + 1][0], w_in_pad, wabt, l + 1)
            x = x2.reshape(bsz, t, d)
        else:
            x = _combine_final(x2, gt, g2, fw, ys, pos3, t).reshape(bsz, t, d)
    return x
```

```python
import jax
import jax.numpy as jnp
from jax import lax
from jax.experimental import pallas as pl
from jax.experimental.pallas import tpu as pltpu

F32 = jnp.float32
BF16 = jnp.bfloat16

EPS = 1e-6
GRID_W = 64
CONV_W = 256
NA_HEADS = 4
NA_DH = 64
NA_W = NA_HEADS * NA_DH
NA_KH = 8
NA_KW = 16
GDN_HEADS = 4
GDN_DK = 128
GDN_DV = 128
GDN_W = GDN_HEADS * GDN_DV
GDN_QKV = 2 * GDN_HEADS * GDN_DK + GDN_W
GDN_CHUNK = 64
N_GROUPS = 4
EXPERTS_PER_GROUP = 8
N_EXPERTS = N_GROUPS * EXPERTS_PER_GROUP
TOP_K = 2

COL_CONV = 0
COL_NA = 3 * CONV_W
COL_GQKV = COL_NA + 3 * NA_W
COL_GZ = COL_GQKV + GDN_QKV
COL_AB = COL_GZ + GDN_W
D_IN = COL_AB + 4 * GDN_HEADS
D_IN_PAD = COL_AB + 128
LANES = 128
SUBLANES = 8

NEG = -1e30
VMEM_LIMIT = 56 * 1024 * 1024

MOE_BM = 512
ROW_TILE = 512


def _cparams(sem):
    return pltpu.CompilerParams(dimension_semantics=sem, vmem_limit_bytes=VMEM_LIMIT)


def _silu(x):
    return x * (1.0 / (1.0 + jnp.exp(-x)))


def _nt(a, b):
    return lax.dot_general(a, b, (((1,), (1,)), ((), ())), preferred_element_type=F32)


def _tn(a, b):
    return lax.dot_general(a, b, (((0,), (0,)), ((), ())), preferred_element_type=F32)


def _mm(a, b):
    return jnp.dot(a, b, preferred_element_type=F32)


def _ada_kernel(ct_ref, w_ref, b_ref, o_ref):
    w = w_ref[0]
    nb = ct_ref.shape[1]
    for r in range(nb):
        col = _silu(ct_ref[:, r:r + 1])
        o_ref[0, r:r + 1, :] = jnp.sum(w * col, axis=0, keepdims=True) + b_ref[0]


def _ada(c, w_ada, b_ada):
    depth, d, n6 = w_ada.shape
    bsz = c.shape[0]
    tn = 512
    return pl.pallas_call(
        _ada_kernel,
        grid=(depth, n6 // tn),
        in_specs=[pl.BlockSpec((d, bsz), lambda l, j: (0, 0)),
                  pl.BlockSpec((1, d, tn), lambda l, j: (l, 0, j)),
                  pl.BlockSpec((1, 1, tn), lambda l, j: (l, 0, j))],
        out_specs=pl.BlockSpec((1, bsz, tn), lambda l, j: (l, 0, j)),
        out_shape=jax.ShapeDtypeStruct((depth, bsz, n6), F32),
        compiler_params=_cparams(("parallel", "parallel")),
        name="ada",
    )(c.T, w_ada, b_ada.reshape(depth, 1, n6))


def _modnorm(x, nw, sc, sh):
    ms = jnp.mean(x * x, axis=-1, keepdims=True)
    y = x * lax.rsqrt(ms + EPS)
    return (y * nw) * (1.0 + sc) + sh


def _project(h, w_ref, wabt_ref, o_ref, ot_ref):
    o_ref[0] = _mm(h, w_ref[0])
    ot_ref[0] = _nt(wabt_ref[0], h)


def _inproj_kernel(x_ref, nw_ref, sc_ref, sh_ref, w_ref, wabt_ref, o_ref, ot_ref):
    h = _modnorm(x_ref[0], nw_ref[...], sc_ref[0], sh_ref[0]).astype(BF16)
    _project(h, w_ref, wabt_ref, o_ref, ot_ref)


def _inproj(x, nw, sc, sh, w_pad, wabt, layer):
    bsz, t, d = x.shape
    tm = min(ROW_TILE, t)
    nab = wabt.shape[1]
    return pl.pallas_call(
        _inproj_kernel,
        grid=(bsz, t // tm),
        in_specs=[pl.BlockSpec((1, tm, d), lambda b, i: (b, i, 0)),
                  pl.BlockSpec((1, d), lambda b, i: (0, 0)),
                  pl.BlockSpec((1, 1, d), lambda b, i: (b, 0, 0)),
                  pl.BlockSpec((1, 1, d), lambda b, i: (b, 0, 0)),
                  pl.BlockSpec((1, d, D_IN_PAD), lambda b, i: (layer, 0, 0)),
                  pl.BlockSpec((1, nab, d), lambda b, i: (layer, 0, 0))],
        out_specs=[pl.BlockSpec((1, tm, D_IN_PAD), lambda b, i: (b, i, 0)),
                   pl.BlockSpec((1, nab, tm), lambda b, i: (b, 0, i))],
        out_shape=[jax.ShapeDtypeStruct((bsz, t, D_IN_PAD), F32),
                   jax.ShapeDtypeStruct((bsz, nab, t), F32)],
        compiler_params=_cparams(("parallel", "parallel")),
        name="inproj",
    )(x, nw, sc, sh, w_pad, wabt)


def _dwconv3(u, prev_row, next_row, w_ref):
    tt = u.shape[0]
    row = lax.broadcasted_iota(jnp.int32, u.shape, 0)
    dn = jnp.where(row == 0, prev_row, pltpu.roll(u, 1, axis=0))
    up = jnp.where(row == tt - 1, next_row, pltpu.roll(u, tt - 1, axis=0))
    return w_ref[0:1, :] * dn + w_ref[1:2, :] * u + w_ref[2:3, :] * up


def _halo_specs(tt, t, width, colblk):
    nsub = tt // SUBLANES
    last = t // SUBLANES - 1
    return [pl.BlockSpec((1, tt, width), lambda b, i: (b, i, colblk)),
            pl.BlockSpec((1, SUBLANES, width), lambda b, i: (b, jnp.maximum(i * nsub - 1, 0), colblk)),
            pl.BlockSpec((1, SUBLANES, width), lambda b, i: (b, jnp.minimum((i + 1) * nsub, last), colblk))]


def _convmix_tile(m_ref, p_ref, n_ref, w_ref):
    i = pl.program_id(1)
    nt = pl.num_programs(1)
    m = m_ref[0]
    u = m[:, CONV_W:2 * CONV_W] * m[:, 2 * CONV_W:]
    p = p_ref[0]
    n = n_ref[0]
    pu = p[SUBLANES - 1:SUBLANES, CONV_W:2 * CONV_W] * p[SUBLANES - 1:SUBLANES, 2 * CONV_W:]
    nu = n[0:1, CONV_W:2 * CONV_W] * n[0:1, 2 * CONV_W:]
    pu = jnp.where(i == 0, 0.0, pu)
    nu = jnp.where(i == nt - 1, 0.0, nu)
    return m[:, :CONV_W] * _dwconv3(u, pu, nu, w_ref)


NA_RB = NA_KH // 2
NA_TOK = NA_RB * GRID_W
NA_KEYS = 3 * NA_TOK


def _na_bias_table(rpb):
    col = jnp.arange(GRID_W)
    cstart = jnp.clip(col - NA_KW // 2, 0, GRID_W - NA_KW)
    kc = jnp.arange(GRID_W)
    valid = (kc[None, :] >= cstart[:, None]) & (kc[None, :] < cstart[:, None] + NA_KW)
    dc = kc[None, :] - col[:, None] + (NA_KW - 1)
    onehot = (dc[None] == jnp.arange(2 * NA_KW - 1)[:, None, None]) & valid[None]
    cols = jnp.einsum('lhrd,dck->lhrck', rpb, onehot.astype(F32), precision=lax.Precision.HIGHEST)
    cols = jnp.where(valid, cols, NEG)
    lo = NA_KH - 1 - NA_RB
    blk = jnp.stack([cols[:, :, lo - j:lo - j + 3 * NA_RB] for j in range(NA_RB)], axis=2)
    blk = jnp.transpose(blk, (0, 1, 2, 4, 3, 5))
    return blk.reshape(rpb.shape[0], NA_HEADS, NA_TOK, NA_KEYS)


def _na_kernel(q_ref, kp_ref, kc_ref, kn_ref, vp_ref, vc_ref, vn_ref, bias_ref, o_ref, *, rows):
    i = pl.program_id(1)
    kbuf = jnp.concatenate([kp_ref[0], kc_ref[0], kn_ref[0]], axis=0).astype(BF16)
    vbuf = jnp.concatenate([vp_ref[0], vc_ref[0], vn_ref[0]], axis=0).astype(BF16)
    q = q_ref[0] * (NA_DH ** -0.5)
    head_of_lane = lax.broadcasted_iota(jnp.int32, (1, NA_W), 1) // NA_DH
    qrow = i * NA_RB + lax.broadcasted_iota(jnp.int32, (NA_TOK, NA_KEYS), 0) // GRID_W
    krow = (i - 1) * NA_RB + lax.broadcasted_iota(jnp.int32, (NA_TOK, NA_KEYS), 1) // GRID_W
    rs = jnp.clip(qrow - NA_KH // 2, 0, rows - NA_KH)
    row_mask = jnp.where((krow >= rs) & (krow < rs + NA_KH), 0.0, NEG)
    acc = jnp.zeros((NA_TOK, NA_W), F32)
    for h in range(NA_HEADS):
        mine = head_of_lane == h
        s = _nt(jnp.where(mine, q, 0.0).astype(BF16), kbuf) + (bias_ref[0, h] + row_mask)
        m = jnp.max(s, axis=-1, keepdims=True)
        p = jnp.exp(s - m)
        l = jnp.sum(p, axis=-1, keepdims=True)
        o = _mm((p / l).astype(BF16), vbuf)
        acc = acc + jnp.where(mine, o, 0.0)
    o_ref[0] = acc


def _na(proj, bias_tbl, layer):
    bsz, t, _ = proj.shape
    rows = t // GRID_W
    nblk = rows // NA_RB
    qc, kc, vc = COL_NA // NA_W, COL_NA // NA_W + 1, COL_NA // NA_W + 2

    def spec(col, shift):
        return pl.BlockSpec((1, NA_TOK, NA_W), lambda b, i: (b, jnp.clip(i + shift, 0, nblk - 1), col))

    def na_kernel(*refs):
        _na_kernel(*refs, rows=rows)

    return pl.pallas_call(
        na_kernel,
        grid=(bsz, nblk),
        in_specs=[spec(qc, 0), spec(kc, -1), spec(kc, 0), spec(kc, 1), spec(vc, -1), spec(vc, 0), spec(vc, 1),
                  pl.BlockSpec((1, NA_HEADS, NA_TOK, NA_KEYS), lambda b, i: (layer, 0, 0, 0))],
        out_specs=pl.BlockSpec((1, NA_TOK, NA_W), lambda b, i: (b, i, 0)),
        out_shape=jax.ShapeDtypeStruct((bsz, t, NA_W), F32),
        compiler_params=_cparams(("parallel", "parallel")),
        name="na",
    )(proj, proj, proj, proj, proj, proj, proj, bias_tbl)


def _gdn_qkv(m_ref, p_ref, n_ref, w_ref):
    i = pl.program_id(1)
    nt = pl.num_programs(1)
    pu = jnp.where(i == 0, 0.0, p_ref[0, SUBLANES - 1:SUBLANES, :])
    nu = jnp.where(i == nt - 1, 0.0, n_ref[0, 0:1, :])
    c = _silu(_dwconv3(m_ref[0], pu, nu, w_ref))
    heads = []
    for hh in range(GDN_QKV // GDN_DK):
        xh = c[:, hh * GDN_DK:(hh + 1) * GDN_DK]
        if hh < 2 * GDN_HEADS:
            xh = xh * lax.rsqrt(jnp.sum(xh * xh, axis=-1, keepdims=True) + EPS)
            if hh < GDN_HEADS:
                xh = xh * (GDN_DK ** -0.5)
        heads.append(xh)
    return heads[:GDN_HEADS], heads[GDN_HEADS:2 * GDN_HEADS], heads[2 * GDN_HEADS:]


GDN_GS = 128
GDN_UNITS = 2 * GDN_HEADS
GDN_GC = GDN_GS // GDN_CHUNK


def _softplus(x):
    return jnp.maximum(x, 0.0) + jnp.log1p(jnp.exp(-jnp.abs(x)))


def _seg_cumsum(x, axis, reverse):
    n = x.shape[axis]
    pos = lax.broadcasted_iota(jnp.int32, x.shape, axis) & (GDN_CHUNK - 1)
    s = 1
    while s < GDN_CHUNK:
        if reverse:
            x = x + jnp.where(pos < GDN_CHUNK - s, pltpu.roll(x, n - s, axis=axis), 0.0)
        else:
            x = x + jnp.where(pos >= s, pltpu.roll(x, s, axis=axis), 0.0)
        s *= 2
    return x


def _gdnchunk_kernel(m_ref, p_ref, n_ref, cw_ref, ab_ref, abt_ref, alr_ref, dtr_ref, alc_ref, dtc_ref,
                     u_ref, wq_ref, kd_ref, in_ref, gl_ref):
    gs, c, nh = GDN_GS, GDN_CHUNK, GDN_HEADS
    ri = lax.broadcasted_iota(jnp.int32, (gs, gs), 0)
    ci = lax.broadcasted_iota(jnp.int32, (gs, gs), 1)
    same = (ri // c) == (ci // c)
    eye = (ri == ci).astype(F32)
    rowc = lax.broadcasted_iota(jnp.int32, (gs, 1), 0) // c
    ab = ab_ref[0]
    abt = abt_ref[0]
    graw_c = -jnp.exp(alr_ref[...]) * _softplus(ab + dtr_ref[...])
    graw_r = -jnp.exp(alc_ref[...]) * _softplus(abt[0:2 * nh] + dtc_ref[...])
    beta_c = 1.0 / (1.0 + jnp.exp(-ab))
    g_col = [_seg_cumsum(graw_c, 0, False), _seg_cumsum(graw_c, 0, True)]
    g_row = [_seg_cumsum(graw_r, 1, False), _seg_cumsum(graw_r, 1, True)]
    incl = [same & (ri >= ci), same & (ri <= ci)]
    strict = [same & (ri > ci), same & (ri < ci)]

    qs, ks, vs = _gdn_qkv(m_ref, p_ref, n_ref, cw_ref)
    grams = [_nt(jnp.concatenate([q, k], axis=0).astype(BF16), k.astype(BF16)) for q, k in zip(qs, ks)]

    units = [(d, h) for d in range(2) for h in range(nh)]
    gcs, bcs, intras, xs, ps = [], [], [], [], []
    for d, h in units:
        col = d * nh + h
        gc = g_col[d][:, col:col + 1]
        gr = g_row[d][col:col + 1, :]
        bc = beta_c[:, 2 * nh + col:2 * nh + col + 1]
        e_incl = jnp.exp(jnp.where(incl[d], gc - gr, NEG))
        a = grams[h][gs:] * bc * jnp.where(strict[d], e_incl, 0.0)
        gcs.append(gc), bcs.append(bc)
        intras.append(grams[h][:gs] * e_incl)
        xs.append(eye - a), ps.append(a)
    for _ in range(5):
        pbs = [p.astype(BF16) for p in ps]
        ps = [_mm(pb, pb) for pb in pbs]
        xs = [x + _mm(x.astype(BF16), p.astype(BF16)) for x, p in zip(xs, ps)]
    egs = [jnp.exp(gc) for gc in gcs]
    sols = [_mm(x.astype(BF16),
                jnp.concatenate([vs[h] * bc, ks[h] * (bc * eg)], axis=1).astype(BF16))
            for (d, h), x, bc, eg in zip(units, xs, bcs, egs)]
    for (d, h), gc, eg, sol, intra in zip(units, gcs, egs, sols, intras):
        col = d * nh + h
        glast_col = jnp.zeros_like(gc)
        for n in range(GDN_GC):
            r = n * c if d == 1 else (n + 1) * c - 1
            glast = gc[r:r + 1, :]
            glast_col = jnp.where(rowc == n, glast, glast_col)
            gl_ref[0, 0, col * GDN_GC + n:col * GDN_GC + n + 1, :] = jnp.broadcast_to(jnp.exp(glast), (1, LANES))
        u_ref[0, col] = sol[:, :GDN_DV]
        w = sol[:, GDN_DV:].astype(BF16)
        qd = (qs[h] * eg).astype(BF16)
        for n in range(GDN_GC):
            wq_ref[0, col, 2 * n * c:(2 * n + 1) * c, :] = w[n * c:(n + 1) * c]
            wq_ref[0, col, (2 * n + 1) * c:(2 * n + 2) * c, :] = qd[n * c:(n + 1) * c]
        kd_ref[0, col] = (ks[h] * jnp.exp(glast_col - gc)).astype(BF16)
        in_ref[0, col] = jnp.concatenate([intra[n * c:(n + 1) * c, n * c:(n + 1) * c] for n in range(GDN_GC)],
                                         axis=0).astype(BF16)


def _gdnchunk(proj, abt, conv_w, a_log, dt_bias):
    bsz, t, _ = proj.shape
    gs, nu = GDN_GS, GDN_UNITS
    ng = t // gs
    nab = abt.shape[1]
    pad = lambda r: jnp.pad(r.reshape(1, -1), ((0, 0), (0, LANES - r.size)))
    alr, dtr = pad(a_log), pad(dt_bias)
    alc, dtc = a_log.reshape(-1, 1), dt_bias.reshape(-1, 1)
    small = lambda shape: pl.BlockSpec(shape, lambda b, i: (0, 0))
    unit = lambda rows, w: pl.BlockSpec((1, nu, rows, w), lambda b, i: (b, 0, i, 0))
    return pl.pallas_call(
        _gdnchunk_kernel,
        grid=(bsz, ng),
        in_specs=_halo_specs(gs, t, GDN_QKV, COL_GQKV // GDN_QKV) + [
            small((3, GDN_QKV)),
            pl.BlockSpec((1, gs, LANES), lambda b, i: (b, i, COL_AB // LANES)),
            pl.BlockSpec((1, nab, gs), lambda b, i: (b, 0, i)),
            small((1, LANES)), small((1, LANES)), small((nu, 1)), small((nu, 1))],
        out_specs=[unit(gs, GDN_DV), unit(2 * gs, GDN_DK), unit(gs, GDN_DK), unit(gs, GDN_CHUNK),
                   pl.BlockSpec((1, 1, nu * GDN_GC, LANES), lambda b, i: (b, i, 0, 0))],
        out_shape=[jax.ShapeDtypeStruct((bsz, nu, t, GDN_DV), F32),
                   jax.ShapeDtypeStruct((bsz, nu, 2 * t, GDN_DK), BF16),
                   jax.ShapeDtypeStruct((bsz, nu, t, GDN_DK), BF16),
                   jax.ShapeDtypeStruct((bsz, nu, t, GDN_CHUNK), BF16),
                   jax.ShapeDtypeStruct((bsz, ng, nu * GDN_GC, LANES), F32)],
        compiler_params=_cparams(("parallel", "parallel")),
        name="gdnchunk",
    )(proj, proj, proj, conv_w, proj, abt, alr, dtr, alc, dtc)


GDN_SB = 4
GDN_ST = GDN_SB * GDN_CHUNK


def _gdnscan_kernel(uf, ub, wqf, wqb, kdf, kdb, inf, inb, glf, glb, of_ref, ob_ref, s_ref):
    i = pl.program_id(0)

    @pl.when(i == 0)
    def _():
        s_ref[...] = jnp.zeros_like(s_ref)

    bsz = uf.shape[0]
    c, nh = GDN_CHUNK, GDN_HEADS
    chains = [(b, d, h) for b in range(bsz) for d in range(2) for h in range(nh)]
    for step in range(GDN_SB):
        zs, vns = [], []
        for b, d, h in chains:
            cc = step if d == 0 else GDN_SB - 1 - step
            wq = (wqf, wqb)[d]
            st = s_ref[(b * 2 + d) * nh + h]
            zs.append(_mm(wq[b, h, 2 * cc * c:(2 * cc + 2) * c, :], st.astype(BF16)))
        for (b, d, h), z in zip(chains, zs):
            cc = step if d == 0 else GDN_SB - 1 - step
            u = (uf, ub)[d]
            vns.append((u[b, h, cc * c:(cc + 1) * c, :] - z[:c]).astype(BF16))
        for (b, d, h), z, vn in zip(chains, zs, vns):
            cc = step if d == 0 else GDN_SB - 1 - step
            rows = slice(cc * c, (cc + 1) * c)
            intra = (inf, inb)[d]
            o_ref = (of_ref, ob_ref)[d]
            o_ref[b, rows, h * GDN_DV:(h + 1) * GDN_DV] = z[c:] + _mm(intra[b, h, rows, :], vn)
        for (b, d, h), vn in zip(chains, vns):
            cc = step if d == 0 else GDN_SB - 1 - step
            rows = slice(cc * c, (cc + 1) * c)
            kd = (kdf, kdb)[d]
            gl = (glf, glb)[d]
            r = (d * nh + h) * GDN_GC + cc % GDN_GC
            sidx = (b * 2 + d) * nh + h
            s_ref[sidx] = s_ref[sidx] * gl[b, cc // GDN_GC, r:r + 1, :] + _tn(kd[b, h, rows, :], vn)


def _gdnscan(u, wq, kd, intra, gl):
    bsz, nu, t, _ = u.shape
    nh = GDN_HEADS
    st = GDN_ST
    nb = t // st
    ngs = st // GDN_GS

    def unit(rows, w, d):
        if d == 0:
            return pl.BlockSpec((bsz, nh, rows, w), lambda i: (0, 0, i, 0))
        return pl.BlockSpec((bsz, nh, rows, w), lambda i: (0, 1, nb - 1 - i, 0))

    glspec = lambda d: pl.BlockSpec((bsz, ngs, nu * GDN_GC, LANES),
                                    (lambda i: (0, i, 0, 0)) if d == 0 else (lambda i: (0, nb - 1 - i, 0, 0)))
    return pl.pallas_call(
        _gdnscan_kernel,
        grid=(nb,),
        in_specs=[unit(st, GDN_DV, 0), unit(st, GDN_DV, 1), unit(2 * st, GDN_DK, 0), unit(2 * st, GDN_DK, 1),
                  unit(st, GDN_DK, 0), unit(st, GDN_DK, 1), unit(st, GDN_CHUNK, 0), unit(st, GDN_CHUNK, 1),
                  glspec(0), glspec(1)],
        out_specs=[pl.BlockSpec((bsz, st, GDN_W), lambda i: (0, i, 0)),
                   pl.BlockSpec((bsz, st, GDN_W), lambda i: (0, nb - 1 - i, 0))],
        out_shape=[jax.ShapeDtypeStruct((bsz, t, GDN_W), F32), jax.ShapeDtypeStruct((bsz, t, GDN_W), F32)],
        scratch_shapes=[pltpu.VMEM((bsz * nu, GDN_DK, GDN_DV), F32)],
        compiler_params=_cparams(("arbitrary",)),
        name="gdnscan",
    )(u, u, wq, wq, kd, kd, intra, intra, gl, gl)


def _outproj_kernel(x_ref, cm_ref, cp_ref, cn_ref, cw_ref, yn_ref, of_ref, ob_ref, z_ref, gnw_ref, wo_ref, g1_ref,
                    nw_ref, sc_ref, sh_ref, wr_ref, br_ref, xo_ref, h_ref, e_ref, g_ref, hist_ref):
    o = of_ref[0] + ob_ref[0]
    z = z_ref[0]
    parts = [_convmix_tile(cm_ref, cp_ref, cn_ref, cw_ref), yn_ref[0]]
    for h in range(GDN_HEADS):
        sl = slice(h * GDN_DV, (h + 1) * GDN_DV)
        oh = o[:, sl]
        oh = oh * lax.rsqrt(jnp.mean(oh * oh, axis=-1, keepdims=True) + EPS) * gnw_ref[...]
        parts.append(oh * _silu(z[:, sl]))
    mixed = _mm(jnp.concatenate(parts, axis=-1).astype(BF16), wo_ref[0])
    xn = x_ref[0] + g1_ref[0] * mixed
    xo_ref[0] = xn
    hf = _modnorm(xn, nw_ref[...], sc_ref[0], sh_ref[0])
    h_ref[0] = hf
    hf_hi = hf.astype(BF16)
    hf_lo = (hf - hf_hi.astype(F32)).astype(BF16)
    logits = (_mm(hf_hi, wr_ref[0]) + (_mm(hf_hi, wr_ref[1]) + _mm(hf_lo, wr_ref[0]))) + br_ref[...]
    lane = lax.broadcasted_iota(jnp.int32, logits.shape, 1)
    gl = jnp.where(lane < N_GROUPS, logits, NEG)
    gm = jnp.max(gl, axis=-1, keepdims=True)
    den = jnp.sum(jnp.exp(gl - gm), axis=-1, keepdims=True)
    grp = jnp.min(jnp.where(gl == gm, lane, LANES), axis=-1, keepdims=True)
    pg_top = 1.0 / den
    ex = lane - N_GROUPS
    in_grp = (ex >= grp * EXPERTS_PER_GROUP) & (ex < (grp + 1) * EXPERTS_PER_GROUP)
    el = jnp.where(in_grp, logits, NEG)
    m1 = jnp.max(el, axis=-1, keepdims=True)
    i1 = jnp.min(jnp.where(el == m1, lane, LANES), axis=-1, keepdims=True)
    el2 = jnp.where(lane == i1, NEG, el)
    m2 = jnp.max(el2, axis=-1, keepdims=True)
    i2 = jnp.min(jnp.where(el2 == m2, lane, LANES), axis=-1, keepdims=True)
    e2 = jnp.exp(m2 - m1)
    w1 = pg_top / (1.0 + e2)
    w2 = pg_top * e2 / (1.0 + e2)
    g_ref[0] = jnp.where(lane == 0, w1, jnp.where(lane == 1, w2, 0.0))
    oh1 = (lane == i1 - N_GROUPS).astype(F32)
    oh2 = (lane == i2 - N_GROUPS).astype(F32)
    both = oh1 + oh2
    tm = logits.shape[0]
    earlier = (lax.broadcasted_iota(jnp.int32, (tm, tm), 0) > lax.broadcasted_iota(jnp.int32, (tm, tm), 1))
    cnt = _mm(earlier.astype(BF16), both.astype(BF16))
    r1 = jnp.sum(cnt * oh1, axis=-1, keepdims=True).astype(jnp.int32)
    r2 = jnp.sum(cnt * oh2, axis=-1, keepdims=True).astype(jnp.int32)
    e_ref[0] = jnp.where(lane == 0, i1 - N_GROUPS, jnp.where(lane == 1, i2 - N_GROUPS,
                         jnp.where(lane == 2, r1, jnp.where(lane == 3, r2, 0))))
    hist_ref[0, 0] = jnp.broadcast_to(jnp.sum(both, axis=0, keepdims=True), (SUBLANES, LANES))


def _outproj(x, conv_w, yn, of, ob, proj, gnw, wo, g1, nw, sc, sh, wr, br, layer):
    bsz, t, d = x.shape
    tm = min(ROW_TILE, t)
    tok = lambda w: pl.BlockSpec((1, tm, w), lambda b, i: (b, i, 0))
    perb = pl.BlockSpec((1, 1, d), lambda b, i: (b, 0, 0))
    full = lambda shape: pl.BlockSpec(shape, lambda b, i: (0, 0))
    return pl.pallas_call(
        _outproj_kernel,
        grid=(bsz, t // tm),
        in_specs=[tok(d)] + _halo_specs(tm, t, 3 * CONV_W, COL_CONV // (3 * CONV_W)) + [
                  full((3, CONV_W)), tok(NA_W), tok(GDN_W), tok(GDN_W),
                  pl.BlockSpec((1, tm, GDN_W), lambda b, i: (b, i, COL_GZ // GDN_W)),
                  full((1, GDN_DV)), pl.BlockSpec((1, d, d), lambda b, i: (layer, 0, 0)), perb, full((1, d)), perb, perb,
                  pl.BlockSpec((2, d, LANES), lambda b, i: (0, 0, 0)), full((1, LANES))],
        out_specs=[tok(d), tok(d), tok(LANES), tok(LANES),
                   pl.BlockSpec((1, 1, SUBLANES, LANES), lambda b, i: (b, i, 0, 0))],
        out_shape=[jax.ShapeDtypeStruct((bsz, t, d), F32), jax.ShapeDtypeStruct((bsz, t, d), F32),
                   jax.ShapeDtypeStruct((bsz, t, LANES), jnp.int32), jax.ShapeDtypeStruct((bsz, t, LANES), F32),
                   jax.ShapeDtypeStruct((bsz, t // tm, SUBLANES, LANES), F32)],
        compiler_params=_cparams(("parallel", "parallel")),
        name="outproj",
    )(x, proj, proj, proj, conv_w, yn, of, ob, proj, gnw, wo, g1, nw, sc, sh, wr, br)


MOE_TM = 256


PIECE = SUBLANES
SORT_ROWS = TOP_K * ROW_TILE + N_EXPERTS * PIECE
MAX_PIECES = SORT_ROWS // PIECE


def _dispatch_kernel(lb_ref, nu_ref, np_ref, dst_ref, e_ref, a8_ref, h_ref, xs_out, zbuf, sbuf, sem, zsem):
    i = pl.program_id(0)
    nt = pl.num_programs(0)
    tm = h_ref.shape[0]
    bm = zbuf.shape[0]
    nblk = xs_out.shape[0] // bm
    slot = i % 2

    def zero_copy(blk):
        return pltpu.make_async_copy(zbuf, xs_out.at[pl.ds(pl.multiple_of(blk * bm, bm), bm), :], zsem)

    @pl.when(i == 0)
    def _():
        zbuf[...] = jnp.zeros_like(zbuf)
        for e in range(N_EXPERTS):
            @pl.when(lb_ref[e] >= 0)
            def _():
                zero_copy(lb_ref[e]).start()

        def start_trailing(j, carry):
            zero_copy(j).start()
            return carry

        def wait_one(j, carry):
            zero_copy(0).wait()
            return carry

        lax.fori_loop(nu_ref[0], nblk, start_trailing, 0)
        lax.fori_loop(nu_ref[0], nblk, wait_one, 0)
        for e in range(N_EXPERTS):
            @pl.when(lb_ref[e] >= 0)
            def _():
                zero_copy(0).wait()

    e = e_ref[...]
    lane = lax.broadcasted_iota(jnp.int32, e.shape, 1)
    a8 = a8_ref[0, 0:1, :]
    rows = []
    eye = lax.broadcasted_iota(jnp.int32, (tm, tm), 0) == lax.broadcasted_iota(jnp.int32, (tm, tm), 1)
    for k in range(TOP_K):
        col = (jnp.sum(jnp.where(lane == e[:, k:k + 1], a8, 0.0), axis=-1, keepdims=True)
               + e[:, TOP_K + k:TOP_K + k + 1].astype(F32))
        rows.append(jnp.sum(jnp.where(eye, col, 0.0), axis=0, keepdims=True))
    j = lax.broadcasted_iota(jnp.int32, (SORT_ROWS, tm), 0).astype(F32)
    perm = ((j == rows[0]) | (j == rows[1])).astype(BF16)
    sbuf[slot] = _mm(perm, h_ref[...].astype(BF16))

    def piece_copy(g, s):
        src = pl.multiple_of(g * PIECE, PIECE)
        dst = pl.multiple_of(dst_ref[0, 0, g] * PIECE, PIECE)
        return pltpu.make_async_copy(sbuf.at[s, pl.ds(src, PIECE), :], xs_out.at[pl.ds(dst, PIECE), :], sem.at[s])

    def start_piece(g, carry):
        piece_copy(g, slot).start()
        return carry

    def wait_slot(s):
        def wait_piece(g, carry):
            pltpu.make_async_copy(sbuf.at[s, pl.ds(0, PIECE), :], xs_out.at[pl.ds(0, PIECE), :], sem.at[s]).wait()
            return carry
        return wait_piece

    lax.fori_loop(0, np_ref[i], start_piece, 0)

    @pl.when(i > 0)
    def _():
        lax.fori_loop(0, np_ref[jnp.maximum(i - 1, 0)], wait_slot(1 - slot), 0)

    @pl.when(i == nt - 1)
    def _():
        lax.fori_loop(0, np_ref[i], wait_slot(slot), 0)


def _dispatch(hf, eidx, a8, dst8, npieces, last_blk, n_used, padded_rows):
    n, d = hf.shape
    tm = ROW_TILE
    nt = n // tm
    grid_spec = pltpu.PrefetchScalarGridSpec(
        num_scalar_prefetch=3,
        grid=(nt,),
        in_specs=[pl.BlockSpec((1, 1, MAX_PIECES), lambda i, lb, nu, npc: (i, 0, 0), memory_space=pltpu.SMEM),
                  pl.BlockSpec((tm, LANES), lambda i, lb, nu, npc: (i, 0)),
                  pl.BlockSpec((1, SUBLANES, LANES), lambda i, lb, nu, npc: (i, 0, 0)),
                  pl.BlockSpec((tm, d), lambda i, lb, nu, npc: (i, 0))],
        out_specs=pl.BlockSpec(memory_space=pl.ANY),
        scratch_shapes=[pltpu.VMEM((MOE_BM, d), F32), pltpu.VMEM((2, SORT_ROWS, d), F32),
                        pltpu.SemaphoreType.DMA((2,)), pltpu.SemaphoreType.DMA(())])
    return pl.pallas_call(
        _dispatch_kernel,
        grid_spec=grid_spec,
        out_shape=jax.ShapeDtypeStruct((padded_rows, d), F32),
        compiler_params=_cparams(("arbitrary",)),
        name="dispatch",
    )(last_blk, n_used, npieces, dst8, eidx, a8, hf)


def _experts_kernel(be_ref, nu_ref, x_ref, w1_ref, w3_ref, w2_ref, o_ref, w1b, w3b, w2b):
    i = pl.program_id(0)
    used = nu_ref[0]

    @pl.when((i == 0) | (be_ref[i] != be_ref[jnp.maximum(i - 1, 0)]))
    def _():
        w1b[...] = w1_ref[0, 0].astype(BF16)
        w3b[...] = w3_ref[0, 0].astype(BF16)
        w2b[...] = w2_ref[0, 0].astype(BF16)

    @pl.when(i < used)
    def _():
        xb = x_ref[...].astype(BF16)
        act = (_silu(_mm(xb, w1b[...])) * _mm(xb, w3b[...])).astype(BF16)
        o_ref[...] = _mm(act, w2b[...])

    @pl.when(i >= used)
    def _():
        o_ref[...] = jnp.zeros_like(o_ref)


def _experts(xs, block_e, n_used, w1, w3, w2, layer):
    padded_rows, d = xs.shape
    bm = MOE_BM
    nblk = padded_rows // bm
    de = w1.shape[-1]
    grid_spec = pltpu.PrefetchScalarGridSpec(
        num_scalar_prefetch=2,
        grid=(nblk,),
        in_specs=[pl.BlockSpec((bm, d), lambda i, be, nu: (jnp.maximum(jnp.minimum(i, nu[0] - 1), 0), 0)),
                  pl.BlockSpec((1, 1, d, de), lambda i, be, nu: (layer, be[i], 0, 0)),
                  pl.BlockSpec((1, 1, d, de), lambda i, be, nu: (layer, be[i], 0, 0)),
                  pl.BlockSpec((1, 1, de, d), lambda i, be, nu: (layer, be[i], 0, 0))],
        out_specs=pl.BlockSpec((bm, d), lambda i, be, nu: (i, 0)),
        scratch_shapes=[pltpu.VMEM((d, de), BF16), pltpu.VMEM((d, de), BF16), pltpu.VMEM((de, d), BF16)])
    return pl.pallas_call(
        _experts_kernel,
        grid_spec=grid_spec,
        out_shape=jax.ShapeDtypeStruct((padded_rows, d), F32),
        compiler_params=_cparams(("arbitrary",)),
        name="experts",
    )(block_e, n_used, xs, w1, w3, w2)


def _gather_rows(idx_ref, ys_hbm, ybuf, sem, slot, lo, hi):
    def body(r, carry):
        pltpu.make_async_copy(ys_hbm.at[pl.ds(idx_ref[0, 0, r], 1), :], ybuf.at[slot, pl.ds(r, 1), :],
                              sem.at[slot]).start()
        return carry
    lax.fori_loop(lo, hi, body, 0, unroll=8)


def _combine_rows(pc_ref, x_ref, gt_ref, g2_ref, ys_hbm, ybuf, sem):
    i = pl.program_id(0)
    tm = x_ref.shape[0]
    slot = i % 2

    @pl.when(i == 0)
    def _():
        _gather_rows(pc_ref, ys_hbm, ybuf, sem, 0, 0, TOP_K * tm)

    pltpu.make_async_copy(ys_hbm.at[pl.ds(0, TOP_K * tm), :], ybuf.at[slot], sem.at[slot]).wait()
    gt = gt_ref[...]
    y = gt[:, 0:1] * ybuf[slot, 0:tm, :] + gt[:, 1:2] * ybuf[slot, tm:2 * tm, :]
    return x_ref[...] + g2_ref[0] * y


def _combine_final_kernel(pc_ref, pn_ref, x_ref, gt_ref, g2_ref, fw_ref, ys_hbm, o_ref, ybuf, sem):
    i = pl.program_id(0)
    tm = x_ref.shape[0]

    @pl.when(i + 1 < pl.num_programs(0))
    def _():
        _gather_rows(pn_ref, ys_hbm, ybuf, sem, 1 - i % 2, 0, TOP_K * tm)

    xn = _combine_rows(pc_ref, x_ref, gt_ref, g2_ref, ys_hbm, ybuf, sem)
    ms = jnp.mean(xn * xn, axis=-1, keepdims=True)
    o_ref[...] = xn * lax.rsqrt(ms + EPS) * fw_ref[...]


PROJ_CHUNK = 512


def _combine_inproj_kernel(pc_ref, pn_ref, x_ref, gt_ref, g2_ref, nw_ref, sc_ref, sh_ref, w_ref, wabt_ref, ys_hbm,
                           xo_ref, o_ref, ot_ref, ybuf, sem):
    i = pl.program_id(0)
    tm = x_ref.shape[0]
    xn = _combine_rows(pc_ref, x_ref, gt_ref, g2_ref, ys_hbm, ybuf, sem)
    xo_ref[...] = xn
    h = _modnorm(xn, nw_ref[...], sc_ref[0], sh_ref[0]).astype(BF16)
    ot_ref[0] = _nt(wabt_ref[0], h)
    ncol = o_ref.shape[2]
    starts = list(range(0, ncol, PROJ_CHUNK))
    per = TOP_K * tm // len(starts)
    nxt = 1 - i % 2
    for j, c0 in enumerate(starts):
        for r in range(j * per, TOP_K * tm if j == len(starts) - 1 else (j + 1) * per):
            pltpu.make_async_copy(ys_hbm.at[pl.ds(pn_ref[0, 0, r], 1), :], ybuf.at[nxt, pl.ds(r, 1), :],
                                  sem.at[nxt]).start()
        c1 = min(c0 + PROJ_CHUNK, ncol)
        o_ref[0, :, c0:c1] = _mm(h, w_ref[0, :, c0:c1])

    @pl.when(i + 1 == pl.num_programs(0))
    def _():
        pltpu.make_async_copy(ys_hbm.at[pl.ds(0, TOP_K * tm), :], ybuf.at[nxt], sem.at[nxt]).wait()


def _combine_specs(n, d, t):
    tm = MOE_TM
    nt = n // tm
    per_b = t // tm
    specs = [pl.BlockSpec((1, 1, TOP_K * tm), lambda i: (i, 0, 0), memory_space=pltpu.SMEM),
             pl.BlockSpec((1, 1, TOP_K * tm), lambda i: (jnp.minimum(i + 1, nt - 1), 0, 0), memory_space=pltpu.SMEM),
             pl.BlockSpec((tm, d), lambda i: (i, 0)),
             pl.BlockSpec((tm, LANES), lambda i: (i, 0)),
             pl.BlockSpec((1, 1, d), lambda i: (i // per_b, 0, 0))]
    scratch = [pltpu.VMEM((2, TOP_K * tm, d), F32), pltpu.SemaphoreType.DMA((2,))]
    return tm, nt, per_b, specs, scratch


def _combine_final(x2, gates, g2, fw, ys, pos3, t):
    n, d = x2.shape
    tm, nt, per_b, specs, scratch = _combine_specs(n, d, t)
    return pl.pallas_call(
        _combine_final_kernel,
        grid=(nt,),
        in_specs=specs + [pl.BlockSpec((1, d), lambda i: (0, 0)), pl.BlockSpec(memory_space=pl.ANY)],
        out_specs=pl.BlockSpec((tm, d), lambda i: (i, 0)),
        out_shape=jax.ShapeDtypeStruct((n, d), F32),
        scratch_shapes=scratch,
        compiler_params=_cparams(("arbitrary",)),
        name="combine",
    )(pos3, pos3, x2, gates, g2, fw, ys)


def _combine_inproj(x2, gates, g2, ys, pos3, t, nw, sc, sh, w_pad, wabt, layer):
    n, d = x2.shape
    bsz = n // t
    nab = wabt.shape[1]
    tm, nt, per_b, specs, scratch = _combine_specs(n, d, t)
    perb = pl.BlockSpec((1, 1, d), lambda i: (i // per_b, 0, 0))
    return pl.pallas_call(
        _combine_inproj_kernel,
        grid=(nt,),
        in_specs=specs + [pl.BlockSpec((1, d), lambda i: (0, 0)), perb, perb,
                          pl.BlockSpec((1, d, D_IN_PAD), lambda i: (layer, 0, 0)),
                          pl.BlockSpec((1, nab, d), lambda i: (layer, 0, 0)),
                          pl.BlockSpec(memory_space=pl.ANY)],
        out_specs=[pl.BlockSpec((tm, d), lambda i: (i, 0)),
                   pl.BlockSpec((1, tm, D_IN_PAD), lambda i: (i // per_b, i % per_b, 0)),
                   pl.BlockSpec((1, nab, tm), lambda i: (i // per_b, 0, i % per_b))],
        out_shape=[jax.ShapeDtypeStruct((n, d), F32), jax.ShapeDtypeStruct((bsz, t, D_IN_PAD), F32),
                   jax.ShapeDtypeStruct((bsz, nab, t), F32)],
        scratch_shapes=scratch,
        compiler_params=_cparams(("arbitrary",)),
        name="combine_inproj",
    )(pos3, pos3, x2, gates, g2, nw, sc, sh, w_pad, wabt, ys)


def _moe_plan(eidx, hist):
    n = eidx.shape[0]
    bm = MOE_BM
    ntile = n // ROW_TILE
    experts = jnp.arange(N_EXPERTS, dtype=jnp.int32)
    hist = hist[:, 0, :N_EXPERTS].astype(jnp.int32)
    cnt = (hist + PIECE - 1) // PIECE * PIECE
    seg_start = jnp.cumsum(cnt, axis=1) - cnt
    pieces = cnt // PIECE
    piece_end = jnp.cumsum(pieces, axis=1)
    npieces = piece_end[:, -1].astype(jnp.int32)
    sizes = jnp.sum(cnt, axis=0)
    base = jnp.cumsum(cnt, axis=0) - cnt
    padded = (sizes + bm - 1) // bm * bm
    pad_end = jnp.cumsum(padded)
    tbl = (pad_end - padded)[None, :] + base
    e = eidx[:, :TOP_K].reshape(ntile, ROW_TILE, TOP_K)
    onehot = e[..., None] == experts
    pos = jnp.sum(jnp.where(onehot, tbl[:, None, None, :], 0), axis=-1).reshape(n, TOP_K) + eidx[:, TOP_K:2 * TOP_K]
    g = jnp.arange(MAX_PIECES, dtype=jnp.int32)
    e_of_g = jnp.minimum(jnp.sum(piece_end[:, None, :] <= g[None, :, None], axis=-1), N_EXPERTS - 1)
    sel = e_of_g[..., None] == experts
    first = jnp.sum(jnp.where(sel, (piece_end - pieces)[:, None, :], 0), axis=-1)
    dst8 = jnp.sum(jnp.where(sel, tbl[:, None, :], 0), axis=-1) // PIECE + (g[None, :] - first)
    dst8 = jnp.where(g[None, :] < npieces[:, None], dst8, 0).astype(jnp.int32).reshape(ntile, 1, MAX_PIECES)
    seg8 = jnp.broadcast_to(jnp.pad(seg_start.astype(F32), ((0, 0), (0, LANES - N_EXPERTS)))[:, None, :],
                            (ntile, SUBLANES, LANES))
    padded_rows = (n * TOP_K + ntile * N_EXPERTS * (PIECE - 1) + N_EXPERTS * (bm - 1) + bm - 1) // bm * bm
    n_blocks = padded_rows // bm
    blk_start = jnp.arange(n_blocks, dtype=jnp.int32) * bm
    block_e = jnp.minimum(jnp.sum(pad_end[None, :] <= blk_start[:, None], axis=1), N_EXPERTS - 1).astype(jnp.int32)
    n_used = (pad_end[-1] // bm).astype(jnp.int32).reshape(1)
    last_blk = jnp.where(padded > 0, pad_end // bm - 1, -1).astype(jnp.int32)
    nt = n // MOE_TM
    pos3 = jnp.transpose(pos.astype(jnp.int32).reshape(nt, MOE_TM, TOP_K), (0, 2, 1)).reshape(nt, 1, TOP_K * MOE_TM)
    return pos3, seg8, dst8, npieces, block_e, n_used, last_blk, padded_rows


def kernel(x, c, norm_mix_w, norm_ffn_w, w_ada, b_ada, w_in, conv_a_w, na_rpb, gdn_conv_w, gdn_a_log, gdn_dt_bias,
           gdn_norm_w, w_out, router_group_w, router_group_b, router_expert_w, router_expert_b, expert_w1,
           expert_w3, expert_w2, final_norm_w):
    bsz, t, d = x.shape
    depth = w_ada.shape[0]
    n = bsz * t
    assert t % ROW_TILE == 0 and (t // GRID_W) % NA_KH == 0, "sequence length must be a multiple of 512"
    mod = _ada(c, w_ada, b_ada)
    w_in_pad = jnp.pad(w_in.astype(BF16), ((0, 0), (0, 0), (0, D_IN_PAD - D_IN)))
    wabt = jnp.transpose(w_in[:, :, COL_AB:], (0, 2, 1)).astype(BF16)
    w_out_b = w_out.astype(BF16)
    wr = jnp.pad(jnp.concatenate([router_group_w, router_expert_w], axis=-1),
                 ((0, 0), (0, 0), (0, LANES - N_GROUPS - N_EXPERTS)))
    wr_hi = wr.astype(BF16)
    wr = jnp.stack([wr_hi, (wr - wr_hi.astype(F32)).astype(BF16)], axis=1)
    br = jnp.pad(jnp.concatenate([router_group_b, router_expert_b], axis=-1),
                 ((0, 0), (0, LANES - N_GROUPS - N_EXPERTS)))
    fw = final_norm_w.reshape(1, d)
    na_bias = _na_bias_table(na_rpb)
    mods = [[mod[l, :, j * d:(j + 1) * d].reshape(bsz, 1, d) for j in range(6)] for l in range(depth)]
    proj, abt = _inproj(x, norm_mix_w[0].reshape(1, d), mods[0][1], mods[0][0], w_in_pad, wabt, 0)
    for l in range(depth):
        sh1, sc1, g1, sh2, sc2, g2 = mods[l]
        y_na = _na(proj, na_bias, l)
        o_f, o_b = _gdnscan(*_gdnchunk(proj, abt, gdn_conv_w[l], gdn_a_log[l], gdn_dt_bias[l]))
        x, hf, eidx, gates, hist = _outproj(x, conv_a_w[l], y_na, o_f, o_b, proj, gdn_norm_w[l].reshape(1, GDN_DV),
                                            w_out_b, g1, norm_ffn_w[l].reshape(1, d), sc2, sh2, wr[l],
                                            br[l].reshape(1, LANES), l)
        eidx = eidx.reshape(n, LANES)
        pos3, seg8, dst8, npieces, block_e, n_used, last_blk, padded_rows = _moe_plan(
            eidx, hist.reshape(-1, SUBLANES, LANES))
        xs = _dispatch(hf.reshape(n, d), eidx, seg8, dst8, npieces, last_blk, n_used, padded_rows)
        ys = _experts(xs, block_e, n_used, expert_w1, expert_w3, expert_w2, l)
        x2, gt = x.reshape(n, d), gates.reshape(n, LANES)
        if l + 1 < depth:
            x2, proj, abt = _combine_inproj(x2, gt, g2, ys, pos3, t, norm_mix_w[l + 1].reshape(1, d), mods[l + 1][1],
                                            mods[l + 1][0], w_in_pad, wabt, l + 1)
            x = x2.reshape(bsz, t, d)
        else:
            x = _combine_final(x2, gt, g2, fw, ys, pos3, t).reshape(bsz, t, d)
    return x
```

```python
import jax
import jax.numpy as jnp
from jax import lax
from jax.experimental import pallas as pl
from jax.experimental.pallas import tpu as pltpu

F32 = jnp.float32
BF16 = jnp.bfloat16

EPS = 1e-6
GRID_W = 64
CONV_W = 256
NA_HEADS = 4
NA_DH = 64
NA_W = NA_HEADS * NA_DH
NA_KH = 8
NA_KW = 16
GDN_HEADS = 4
GDN_DK = 128
GDN_DV = 128
GDN_W = GDN_HEADS * GDN_DV
GDN_QKV = 2 * GDN_HEADS * GDN_DK + GDN_W
GDN_CHUNK = 64
N_GROUPS = 4
EXPERTS_PER_GROUP = 8
N_EXPERTS = N_GROUPS * EXPERTS_PER_GROUP
TOP_K = 2

COL_CONV = 0
COL_NA = 3 * CONV_W
COL_GQKV = COL_NA + 3 * NA_W
COL_GZ = COL_GQKV + GDN_QKV
COL_AB = COL_GZ + GDN_W
D_IN = COL_AB + 4 * GDN_HEADS
D_IN_PAD = COL_AB + 128
LANES = 128
SUBLANES = 8

NEG = -1e30
VMEM_LIMIT = 56 * 1024 * 1024

MOE_BM = 512
ROW_TILE = 512


def _cparams(sem):
    return pltpu.CompilerParams(dimension_semantics=sem, vmem_limit_bytes=VMEM_LIMIT)


def _silu(x):
    return x * (1.0 / (1.0 + jnp.exp(-x)))


def _nt(a, b):
    return lax.dot_general(a, b, (((1,), (1,)), ((), ())), preferred_element_type=F32)


def _tn(a, b):
    return lax.dot_general(a, b, (((0,), (0,)), ((), ())), preferred_element_type=F32)


def _mm(a, b):
    return jnp.dot(a, b, preferred_element_type=F32)


def _ada_kernel(ct_ref, w_ref, b_ref, o_ref):
    w = w_ref[0]
    nb = ct_ref.shape[1]
    for r in range(nb):
        col = _silu(ct_ref[:, r:r + 1])
        o_ref[0, r:r + 1, :] = jnp.sum(w * col, axis=0, keepdims=True) + b_ref[0]


def _ada(c, w_ada, b_ada):
    depth, d, n6 = w_ada.shape
    bsz = c.shape[0]
    tn = 512
    return pl.pallas_call(
        _ada_kernel,
        grid=(depth, n6 // tn),
        in_specs=[pl.BlockSpec((d, bsz), lambda l, j: (0, 0)),
                  pl.BlockSpec((1, d, tn), lambda l, j: (l, 0, j)),
                  pl.BlockSpec((1, 1, tn), lambda l, j: (l, 0, j))],
        out_specs=pl.BlockSpec((1, bsz, tn), lambda l, j: (l, 0, j)),
        out_shape=jax.ShapeDtypeStruct((depth, bsz, n6), F32),
        compiler_params=_cparams(("parallel", "parallel")),
        name="ada",
    )(c.T, w_ada, b_ada.reshape(depth, 1, n6))


def _modnorm(x, nw, sc, sh):
    ms = jnp.mean(x * x, axis=-1, keepdims=True)
    y = x * lax.rsqrt(ms + EPS)
    return (y * nw) * (1.0 + sc) + sh


def _project(h, w_ref, wabt_ref, o_ref, ot_ref):
    o_ref[0] = _mm(h, w_ref[0])
    ot_ref[0] = _nt(wabt_ref[0], h)


def _inproj_kernel(x_ref, nw_ref, sc_ref, sh_ref, w_ref, wabt_ref, o_ref, ot_ref):
    h = _modnorm(x_ref[0], nw_ref[...], sc_ref[0], sh_ref[0]).astype(BF16)
    _project(h, w_ref, wabt_ref, o_ref, ot_ref)


def _inproj(x, nw, sc, sh, w_pad, wabt, layer):
    bsz, t, d = x.shape
    tm = min(ROW_TILE, t)
    nab = wabt.shape[1]
    return pl.pallas_call(
        _inproj_kernel,
        grid=(bsz, t // tm),
        in_specs=[pl.BlockSpec((1, tm, d), lambda b, i: (b, i, 0)),
                  pl.BlockSpec((1, d), lambda b, i: (0, 0)),
                  pl.BlockSpec((1, 1, d), lambda b, i: (b, 0, 0)),
                  pl.BlockSpec((1, 1, d), lambda b, i: (b, 0, 0)),
                  pl.BlockSpec((1, d, D_IN_PAD), lambda b, i: (layer, 0, 0)),
                  pl.BlockSpec((1, nab, d), lambda b, i: (layer, 0, 0))],
        out_specs=[pl.BlockSpec((1, tm, D_IN_PAD), lambda b, i: (b, i, 0)),
                   pl.BlockSpec((1, nab, tm), lambda b, i: (b, 0, i))],
        out_shape=[jax.ShapeDtypeStruct((bsz, t, D_IN_PAD), F32),
                   jax.ShapeDtypeStruct((bsz, nab, t), F32)],
        compiler_params=_cparams(("parallel", "parallel")),
        name="inproj",
    )(x, nw, sc, sh, w_pad, wabt)


def _dwconv3(u, prev_row, next_row, w_ref):
    tt = u.shape[0]
    row = lax.broadcasted_iota(jnp.int32, u.shape, 0)
    dn = jnp.where(row == 0, prev_row, pltpu.roll(u, 1, axis=0))
    up = jnp.where(row == tt - 1, next_row, pltpu.roll(u, tt - 1, axis=0))
    return w_ref[0:1, :] * dn + w_ref[1:2, :] * u + w_ref[2:3, :] * up


def _halo_specs(tt, t, width, colblk):
    nsub = tt // SUBLANES
    last = t // SUBLANES - 1
    return [pl.BlockSpec((1, tt, width), lambda b, i: (b, i, colblk)),
            pl.BlockSpec((1, SUBLANES, width), lambda b, i: (b, jnp.maximum(i * nsub - 1, 0), colblk)),
            pl.BlockSpec((1, SUBLANES, width), lambda b, i: (b, jnp.minimum((i + 1) * nsub, last), colblk))]


def _convmix_tile(m_ref, p_ref, n_ref, w_ref):
    i = pl.program_id(1)
    nt = pl.num_programs(1)
    m = m_ref[0]
    u = m[:, CONV_W:2 * CONV_W] * m[:, 2 * CONV_W:]
    p = p_ref[0]
    n = n_ref[0]
    pu = p[SUBLANES - 1:SUBLANES, CONV_W:2 * CONV_W] * p[SUBLANES - 1:SUBLANES, 2 * CONV_W:]
    nu = n[0:1, CONV_W:2 * CONV_W] * n[0:1, 2 * CONV_W:]
    pu = jnp.where(i == 0, 0.0, pu)
    nu = jnp.where(i == nt - 1, 0.0, nu)
    return m[:, :CONV_W] * _dwconv3(u, pu, nu, w_ref)


NA_RB = NA_KH // 2
NA_TOK = NA_RB * GRID_W
NA_KEYS = 3 * NA_TOK


def _na_bias_table(rpb):
    col = jnp.arange(GRID_W)
    cstart = jnp.clip(col - NA_KW // 2, 0, GRID_W - NA_KW)
    kc = jnp.arange(GRID_W)
    valid = (kc[None, :] >= cstart[:, None]) & (kc[None, :] < cstart[:, None] + NA_KW)
    dc = kc[None, :] - col[:, None] + (NA_KW - 1)
    onehot = (dc[None] == jnp.arange(2 * NA_KW - 1)[:, None, None]) & valid[None]
    cols = jnp.einsum('lhrd,dck->lhrck', rpb, onehot.astype(F32), precision=lax.Precision.HIGHEST)
    cols = jnp.where(valid, cols, NEG)
    lo = NA_KH - 1 - NA_RB
    blk = jnp.stack([cols[:, :, lo - j:lo - j + 3 * NA_RB] for j in range(NA_RB)], axis=2)
    blk = jnp.transpose(blk, (0, 1, 2, 4, 3, 5))
    return blk.reshape(rpb.shape[0], NA_HEADS, NA_TOK, NA_KEYS)


def _na_kernel(q_ref, kp_ref, kc_ref, kn_ref, vp_ref, vc_ref, vn_ref, bias_ref, o_ref, *, rows):
    i = pl.program_id(1)
    kbuf = jnp.concatenate([kp_ref[0], kc_ref[0], kn_ref[0]], axis=0).astype(BF16)
    vbuf = jnp.concatenate([vp_ref[0], vc_ref[0], vn_ref[0]], axis=0).astype(BF16)
    q = q_ref[0] * (NA_DH ** -0.5)
    head_of_lane = lax.broadcasted_iota(jnp.int32, (1, NA_W), 1) // NA_DH
    qrow = i * NA_RB + lax.broadcasted_iota(jnp.int32, (NA_TOK, NA_KEYS), 0) // GRID_W
    krow = (i - 1) * NA_RB + lax.broadcasted_iota(jnp.int32, (NA_TOK, NA_KEYS), 1) // GRID_W
    rs = jnp.clip(qrow - NA_KH // 2, 0, rows - NA_KH)
    row_mask = jnp.where((krow >= rs) & (krow < rs + NA_KH), 0.0, NEG)
    acc = jnp.zeros((NA_TOK, NA_W), F32)
    for h in range(NA_HEADS):
        mine = head_of_lane == h
        s = _nt(jnp.where(mine, q, 0.0).astype(BF16), kbuf) + (bias_ref[0, h] + row_mask)
        m = jnp.max(s, axis=-1, keepdims=True)
        p = jnp.exp(s - m)
        l = jnp.sum(p, axis=-1, keepdims=True)
        o = _mm((p / l).astype(BF16), vbuf)
        acc = acc + jnp.where(mine, o, 0.0)
    o_ref[0] = acc


def _na(proj, bias_tbl, layer):
    bsz, t, _ = proj.shape
    rows = t // GRID_W
    nblk = rows // NA_RB
    qc, kc, vc = COL_NA // NA_W, COL_NA // NA_W + 1, COL_NA // NA_W + 2

    def spec(col, shift):
        return pl.BlockSpec((1, NA_TOK, NA_W), lambda b, i: (b, jnp.clip(i + shift, 0, nblk - 1), col))

    def na_kernel(*refs):
        _na_kernel(*refs, rows=rows)

    return pl.pallas_call(
        na_kernel,
        grid=(bsz, nblk),
        in_specs=[spec(qc, 0), spec(kc, -1), spec(kc, 0), spec(kc, 1), spec(vc, -1), spec(vc, 0), spec(vc, 1),
                  pl.BlockSpec((1, NA_HEADS, NA_TOK, NA_KEYS), lambda b, i: (layer, 0, 0, 0))],
        out_specs=pl.BlockSpec((1, NA_TOK, NA_W), lambda b, i: (b, i, 0)),
        out_shape=jax.ShapeDtypeStruct((bsz, t, NA_W), F32),
        compiler_params=_cparams(("parallel", "parallel")),
        name="na",
    )(proj, proj, proj, proj, proj, proj, proj, bias_tbl)


def _gdn_qkv(m_ref, p_ref, n_ref, w_ref):
    i = pl.program_id(1)
    nt = pl.num_programs(1)
    pu = jnp.where(i == 0, 0.0, p_ref[0, SUBLANES - 1:SUBLANES, :])
    nu = jnp.where(i == nt - 1, 0.0, n_ref[0, 0:1, :])
    c = _silu(_dwconv3(m_ref[0], pu, nu, w_ref))
    heads = []
    for hh in range(GDN_QKV // GDN_DK):
        xh = c[:, hh * GDN_DK:(hh + 1) * GDN_DK]
        if hh < 2 * GDN_HEADS:
            xh = xh * lax.rsqrt(jnp.sum(xh * xh, axis=-1, keepdims=True) + EPS)
            if hh < GDN_HEADS:
                xh = xh * (GDN_DK ** -0.5)
        heads.append(xh)
    return heads[:GDN_HEADS], heads[GDN_HEADS:2 * GDN_HEADS], heads[2 * GDN_HEADS:]


GDN_GS = 128
GDN_UNITS = 2 * GDN_HEADS
GDN_GC = GDN_GS // GDN_CHUNK
GDN_NG = 2
GDN_TS = GDN_NG * GDN_GS


def _softplus(x):
    return jnp.maximum(x, 0.0) + jnp.log1p(jnp.exp(-jnp.abs(x)))


def _seg_cumsum(x, axis, reverse):
    n = x.shape[axis]
    pos = lax.broadcasted_iota(jnp.int32, x.shape, axis) & (GDN_CHUNK - 1)
    s = 1
    while s < GDN_CHUNK:
        if reverse:
            x = x + jnp.where(pos < GDN_CHUNK - s, pltpu.roll(x, n - s, axis=axis), 0.0)
        else:
            x = x + jnp.where(pos >= s, pltpu.roll(x, s, axis=axis), 0.0)
        s *= 2
    return x


def _gdnchunk_kernel(m_ref, p_ref, n_ref, cw_ref, ab_ref, abt_ref, alr_ref, dtr_ref, alc_ref, dtc_ref,
                     u_ref, wq_ref, kd_ref, in_ref, gl_ref):
    gs, c, nh = GDN_GS, GDN_CHUNK, GDN_HEADS
    ri = lax.broadcasted_iota(jnp.int32, (gs, gs), 0)
    ci = lax.broadcasted_iota(jnp.int32, (gs, gs), 1)
    same = (ri // c) == (ci // c)
    eye = (ri == ci).astype(F32)
    rowc = lax.broadcasted_iota(jnp.int32, (gs, 1), 0) // c
    ab = ab_ref[0]
    abt = abt_ref[0]
    graw_c = -jnp.exp(alr_ref[...]) * _softplus(ab + dtr_ref[...])
    graw_r = -jnp.exp(alc_ref[...]) * _softplus(abt[0:2 * nh] + dtc_ref[...])
    beta_c = 1.0 / (1.0 + jnp.exp(-ab))
    g_col = [_seg_cumsum(graw_c, 0, False), _seg_cumsum(graw_c, 0, True)]
    g_row = [_seg_cumsum(graw_r, 1, False), _seg_cumsum(graw_r, 1, True)]
    incl = [same & (ri >= ci), same & (ri <= ci)]
    strict = [same & (ri > ci), same & (ri < ci)]

    qs, ks, vs = _gdn_qkv(m_ref, p_ref, n_ref, cw_ref)
    groups = [slice(g * gs, (g + 1) * gs) for g in range(GDN_NG)]
    grams = [[_nt(jnp.concatenate([q[rg], k[rg]], axis=0).astype(BF16), k[rg].astype(BF16))
              for q, k in zip(qs, ks)] for rg in groups]

    units = [(g, d, h) for g in range(GDN_NG) for d in range(2) for h in range(nh)]
    gcs, bcs, intras, xs, ps = [], [], [], [], []
    for g, d, h in units:
        col = d * nh + h
        rg = groups[g]
        gc = g_col[d][rg, col:col + 1]
        gr = g_row[d][col:col + 1, rg]
        bc = beta_c[rg, 2 * nh + col:2 * nh + col + 1]
        e_incl = jnp.exp(jnp.where(incl[d], gc - gr, NEG))
        a = grams[g][h][gs:] * bc * jnp.where(strict[d], e_incl, 0.0)
        gcs.append(gc), bcs.append(bc)
        intras.append(grams[g][h][:gs] * e_incl)
        xs.append(eye - a), ps.append(a)
    for _ in range(5):
        pbs = [p.astype(BF16) for p in ps]
        ps = [_mm(pb, pb) for pb in pbs]
        xs = [x + _mm(x.astype(BF16), p.astype(BF16)) for x, p in zip(xs, ps)]
    egs = [jnp.exp(gc) for gc in gcs]
    sols = [_mm(x.astype(BF16),
                jnp.concatenate([vs[h][groups[g]] * bc, ks[h][groups[g]] * (bc * eg)], axis=1).astype(BF16))
            for (g, d, h), x, bc, eg in zip(units, xs, bcs, egs)]
    for (g, d, h), gc, eg, sol, intra in zip(units, gcs, egs, sols, intras):
        col = d * nh + h
        rg = groups[g]
        glast_col = jnp.zeros_like(gc)
        for n in range(GDN_GC):
            r = n * c if d == 1 else (n + 1) * c - 1
            glast = gc[r:r + 1, :]
            glast_col = jnp.where(rowc == n, glast, glast_col)
            gl_ref[0, g, col * GDN_GC + n:col * GDN_GC + n + 1, :] = jnp.broadcast_to(jnp.exp(glast), (1, LANES))
        u_ref[0, col, rg, :] = sol[:, :GDN_DV]
        w = sol[:, GDN_DV:].astype(BF16)
        qd = (qs[h][rg] * eg).astype(BF16)
        for n in range(GDN_GC):
            base = 2 * (g * gs + n * c)
            wq_ref[0, col, base:base + c, :] = w[n * c:(n + 1) * c]
            wq_ref[0, col, base + c:base + 2 * c, :] = qd[n * c:(n + 1) * c]
        kd_ref[0, col, rg, :] = (ks[h][rg] * jnp.exp(glast_col - gc)).astype(BF16)
        in_ref[0, col, rg, :] = jnp.concatenate([intra[n * c:(n + 1) * c, n * c:(n + 1) * c] for n in range(GDN_GC)],
                                                axis=0).astype(BF16)


def _gdnchunk(proj, abt, conv_w, a_log, dt_bias):
    bsz, t, _ = proj.shape
    ts, nu = GDN_TS, GDN_UNITS
    nab = abt.shape[1]
    pad = lambda r: jnp.pad(r.reshape(1, -1), ((0, 0), (0, LANES - r.size)))
    alr, dtr = pad(a_log), pad(dt_bias)
    alc, dtc = a_log.reshape(-1, 1), dt_bias.reshape(-1, 1)
    small = lambda shape: pl.BlockSpec(shape, lambda b, i: (0, 0))
    unit = lambda rows, w: pl.BlockSpec((1, nu, rows, w), lambda b, i: (b, 0, i, 0))
    return pl.pallas_call(
        _gdnchunk_kernel,
        grid=(bsz, t // ts),
        in_specs=_halo_specs(ts, t, GDN_QKV, COL_GQKV // GDN_QKV) + [
            small((3, GDN_QKV)),
            pl.BlockSpec((1, ts, LANES), lambda b, i: (b, i, COL_AB // LANES)),
            pl.BlockSpec((1, nab, ts), lambda b, i: (b, 0, i)),
            small((1, LANES)), small((1, LANES)), small((nu, 1)), small((nu, 1))],
        out_specs=[unit(ts, GDN_DV), unit(2 * ts, GDN_DK), unit(ts, GDN_DK), unit(ts, GDN_CHUNK),
                   pl.BlockSpec((1, GDN_NG, nu * GDN_GC, LANES), lambda b, i: (b, i, 0, 0))],
        out_shape=[jax.ShapeDtypeStruct((bsz, nu, t, GDN_DV), F32),
                   jax.ShapeDtypeStruct((bsz, nu, 2 * t, GDN_DK), BF16),
                   jax.ShapeDtypeStruct((bsz, nu, t, GDN_DK), BF16),
                   jax.ShapeDtypeStruct((bsz, nu, t, GDN_CHUNK), BF16),
                   jax.ShapeDtypeStruct((bsz, t // GDN_GS, nu * GDN_GC, LANES), F32)],
        compiler_params=_cparams(("parallel", "parallel")),
        name="gdnchunk",
    )(proj, proj, proj, conv_w, proj, abt, alr, dtr, alc, dtc)


GDN_SB = 4
GDN_ST = GDN_SB * GDN_CHUNK


def _gdnscan_kernel(uf, ub, wqf, wqb, kdf, kdb, inf, inb, glf, glb, of_ref, ob_ref, s_ref):
    i = pl.program_id(0)

    @pl.when(i == 0)
    def _():
        s_ref[...] = jnp.zeros_like(s_ref)

    bsz = uf.shape[0]
    c, nh = GDN_CHUNK, GDN_HEADS
    chains = [(b, d, h) for b in range(bsz) for d in range(2) for h in range(nh)]
    for step in range(GDN_SB):
        zs, vns = [], []
        for b, d, h in chains:
            cc = step if d == 0 else GDN_SB - 1 - step
            wq = (wqf, wqb)[d]
            st = s_ref[(b * 2 + d) * nh + h]
            zs.append(_mm(wq[b, h, 2 * cc * c:(2 * cc + 2) * c, :], st.astype(BF16)))
        for (b, d, h), z in zip(chains, zs):
            cc = step if d == 0 else GDN_SB - 1 - step
            u = (uf, ub)[d]
            vns.append((u[b, h, cc * c:(cc + 1) * c, :] - z[:c]).astype(BF16))
        for (b, d, h), z, vn in zip(chains, zs, vns):
            cc = step if d == 0 else GDN_SB - 1 - step
            rows = slice(cc * c, (cc + 1) * c)
            intra = (inf, inb)[d]
            o_ref = (of_ref, ob_ref)[d]
            o_ref[b, rows, h * GDN_DV:(h + 1) * GDN_DV] = z[c:] + _mm(intra[b, h, rows, :], vn)
        for (b, d, h), vn in zip(chains, vns):
            cc = step if d == 0 else GDN_SB - 1 - step
            rows = slice(cc * c, (cc + 1) * c)
            kd = (kdf, kdb)[d]
            gl = (glf, glb)[d]
            r = (d * nh + h) * GDN_GC + cc % GDN_GC
            sidx = (b * 2 + d) * nh + h
            s_ref[sidx] = s_ref[sidx] * gl[b, cc // GDN_GC, r:r + 1, :] + _tn(kd[b, h, rows, :], vn)


def _gdnscan(u, wq, kd, intra, gl):
    bsz, nu, t, _ = u.shape
    nh = GDN_HEADS
    st = GDN_ST
    nb = t // st
    ngs = st // GDN_GS

    def unit(rows, w, d):
        if d == 0:
            return pl.BlockSpec((bsz, nh, rows, w), lambda i: (0, 0, i, 0))
        return pl.BlockSpec((bsz, nh, rows, w), lambda i: (0, 1, nb - 1 - i, 0))

    glspec = lambda d: pl.BlockSpec((bsz, ngs, nu * GDN_GC, LANES),
                                    (lambda i: (0, i, 0, 0)) if d == 0 else (lambda i: (0, nb - 1 - i, 0, 0)))
    return pl.pallas_call(
        _gdnscan_kernel,
        grid=(nb,),
        in_specs=[unit(st, GDN_DV, 0), unit(st, GDN_DV, 1), unit(2 * st, GDN_DK, 0), unit(2 * st, GDN_DK, 1),
                  unit(st, GDN_DK, 0), unit(st, GDN_DK, 1), unit(st, GDN_CHUNK, 0), unit(st, GDN_CHUNK, 1),
                  glspec(0), glspec(1)],
        out_specs=[pl.BlockSpec((bsz, st, GDN_W), lambda i: (0, i, 0)),
                   pl.BlockSpec((bsz, st, GDN_W), lambda i: (0, nb - 1 - i, 0))],
        out_shape=[jax.ShapeDtypeStruct((bsz, t, GDN_W), F32), jax.ShapeDtypeStruct((bsz, t, GDN_W), F32)],
        scratch_shapes=[pltpu.VMEM((bsz * nu, GDN_DK, GDN_DV), F32)],
        compiler_params=_cparams(("arbitrary",)),
        name="gdnscan",
    )(u, u, wq, wq, kd, kd, intra, intra, gl, gl)


OUT_SPLIT = 1


def _outproj_kernel(x_ref, cm_ref, cp_ref, cn_ref, cw_ref, yn_ref, of_ref, ob_ref, z_ref, gnw_ref, wo_ref, g1_ref,
                    nw_ref, sc_ref, sh_ref, wr_ref, br_ref, xo_ref, h_ref, e_ref, g_ref, hist_ref):
    yc = _convmix_tile(cm_ref, cp_ref, cn_ref, cw_ref)
    tm = x_ref.shape[1]
    lane = lax.broadcasted_iota(jnp.int32, (tm // OUT_SPLIT, LANES), 1)
    picks = []
    for r in range(OUT_SPLIT):
        rows = slice(r * (tm // OUT_SPLIT), (r + 1) * (tm // OUT_SPLIT))
        o = of_ref[0, rows, :] + ob_ref[0, rows, :]
        z = z_ref[0, rows, :]
        parts = [yc[rows], yn_ref[0, rows, :]]
        for h in range(GDN_HEADS):
            sl = slice(h * GDN_DV, (h + 1) * GDN_DV)
            oh = o[:, sl]
            oh = oh * lax.rsqrt(jnp.mean(oh * oh, axis=-1, keepdims=True) + EPS) * gnw_ref[...]
            parts.append(oh * _silu(z[:, sl]))
        mixed = _mm(jnp.concatenate(parts, axis=-1).astype(BF16), wo_ref[0])
        xn = x_ref[0, rows, :] + g1_ref[0] * mixed
        xo_ref[0, rows, :] = xn
        hf = _modnorm(xn, nw_ref[...], sc_ref[0], sh_ref[0])
        h_ref[0, rows, :] = hf
        hf_hi = hf.astype(BF16)
        hf_lo = (hf - hf_hi.astype(F32)).astype(BF16)
        logits = (_mm(hf_hi, wr_ref[0]) + (_mm(hf_hi, wr_ref[1]) + _mm(hf_lo, wr_ref[0]))) + br_ref[...]
        gl = jnp.where(lane < N_GROUPS, logits, NEG)
        gm = jnp.max(gl, axis=-1, keepdims=True)
        den = jnp.sum(jnp.exp(gl - gm), axis=-1, keepdims=True)
        grp = jnp.min(jnp.where(gl == gm, lane, LANES), axis=-1, keepdims=True)
        pg_top = 1.0 / den
        ex = lane - N_GROUPS
        in_grp = (ex >= grp * EXPERTS_PER_GROUP) & (ex < (grp + 1) * EXPERTS_PER_GROUP)
        el = jnp.where(in_grp, logits, NEG)
        m1 = jnp.max(el, axis=-1, keepdims=True)
        i1 = jnp.min(jnp.where(el == m1, lane, LANES), axis=-1, keepdims=True)
        el2 = jnp.where(lane == i1, NEG, el)
        m2 = jnp.max(el2, axis=-1, keepdims=True)
        i2 = jnp.min(jnp.where(el2 == m2, lane, LANES), axis=-1, keepdims=True)
        e2 = jnp.exp(m2 - m1)
        w1 = pg_top / (1.0 + e2)
        w2 = pg_top * e2 / (1.0 + e2)
        g_ref[0, rows, :] = jnp.where(lane == 0, w1, jnp.where(lane == 1, w2, 0.0))
        picks.append((i1 - N_GROUPS, i2 - N_GROUPS))
    e1 = jnp.concatenate([p[0] for p in picks], axis=0)
    e2i = jnp.concatenate([p[1] for p in picks], axis=0)
    lane = lax.broadcasted_iota(jnp.int32, (tm, LANES), 1)
    oh1 = (lane == e1).astype(F32)
    oh2 = (lane == e2i).astype(F32)
    both = oh1 + oh2
    earlier = (lax.broadcasted_iota(jnp.int32, (tm, tm), 0) > lax.broadcasted_iota(jnp.int32, (tm, tm), 1))
    cnt = _mm(earlier.astype(BF16), both.astype(BF16))
    r1 = jnp.sum(cnt * oh1, axis=-1, keepdims=True).astype(jnp.int32)
    r2 = jnp.sum(cnt * oh2, axis=-1, keepdims=True).astype(jnp.int32)
    e_ref[0] = jnp.where(lane == 0, e1, jnp.where(lane == 1, e2i, jnp.where(lane == 2, r1, jnp.where(lane == 3, r2, 0))))
    hist_ref[0, 0] = jnp.broadcast_to(jnp.sum(both, axis=0, keepdims=True), (SUBLANES, LANES))


def _outproj(x, conv_w, yn, of, ob, proj, gnw, wo, g1, nw, sc, sh, wr, br, layer):
    bsz, t, d = x.shape
    tm = min(ROW_TILE, t)
    tok = lambda w: pl.BlockSpec((1, tm, w), lambda b, i: (b, i, 0))
    perb = pl.BlockSpec((1, 1, d), lambda b, i: (b, 0, 0))
    full = lambda shape: pl.BlockSpec(shape, lambda b, i: (0, 0))
    return pl.pallas_call(
        _outproj_kernel,
        grid=(bsz, t // tm),
        in_specs=[tok(d)] + _halo_specs(tm, t, 3 * CONV_W, COL_CONV // (3 * CONV_W)) + [
                  full((3, CONV_W)), tok(NA_W), tok(GDN_W), tok(GDN_W),
                  pl.BlockSpec((1, tm, GDN_W), lambda b, i: (b, i, COL_GZ // GDN_W)),
                  full((1, GDN_DV)), pl.BlockSpec((1, d, d), lambda b, i: (layer, 0, 0)), perb, full((1, d)), perb, perb,
                  pl.BlockSpec((2, d, LANES), lambda b, i: (0, 0, 0)), full((1, LANES))],
        out_specs=[tok(d), tok(d), tok(LANES), tok(LANES),
                   pl.BlockSpec((1, 1, SUBLANES, LANES), lambda b, i: (b, i, 0, 0))],
        out_shape=[jax.ShapeDtypeStruct((bsz, t, d), F32), jax.ShapeDtypeStruct((bsz, t, d), F32),
                   jax.ShapeDtypeStruct((bsz, t, LANES), jnp.int32), jax.ShapeDtypeStruct((bsz, t, LANES), F32),
                   jax.ShapeDtypeStruct((bsz, t // tm, SUBLANES, LANES), F32)],
        compiler_params=_cparams(("parallel", "parallel")),
        name="outproj",
    )(x, proj, proj, proj, conv_w, yn, of, ob, proj, gnw, wo, g1, nw, sc, sh, wr, br)


MOE_TM = 256


PIECE = SUBLANES
SORT_ROWS = TOP_K * ROW_TILE + N_EXPERTS * PIECE
MAX_PIECES = SORT_ROWS // PIECE


def _dispatch_kernel(lb_ref, nu_ref, np_ref, dst_ref, dprev_ref, e_ref, a8_ref, h_ref, xs_out, zbuf, sbuf, sem, zsem):
    i = pl.program_id(0)
    nt = pl.num_programs(0)
    tm = h_ref.shape[0]
    bm = zbuf.shape[0]
    nblk = xs_out.shape[0] // bm
    slot = i % 2

    def zero_copy(blk):
        return pltpu.make_async_copy(zbuf, xs_out.at[pl.ds(pl.multiple_of(blk * bm, bm), bm), :], zsem)

    @pl.when(i == 0)
    def _():
        zbuf[...] = jnp.zeros_like(zbuf)
        for e in range(N_EXPERTS):
            @pl.when(lb_ref[e] >= 0)
            def _():
                zero_copy(lb_ref[e]).start()

        def start_trailing(j, carry):
            zero_copy(j).start()
            return carry

        def wait_one(j, carry):
            zero_copy(0).wait()
            return carry

        lax.fori_loop(nu_ref[0], nblk, start_trailing, 0)
        lax.fori_loop(nu_ref[0], nblk, wait_one, 0)
        for e in range(N_EXPERTS):
            @pl.when(lb_ref[e] >= 0)
            def _():
                zero_copy(0).wait()

    e = e_ref[...]
    lane = lax.broadcasted_iota(jnp.int32, e.shape, 1)
    a8 = a8_ref[0, 0:1, :]
    rows = []
    eye = lax.broadcasted_iota(jnp.int32, (tm, tm), 0) == lax.broadcasted_iota(jnp.int32, (tm, tm), 1)
    for k in range(TOP_K):
        col = (jnp.sum(jnp.where(lane == e[:, k:k + 1], a8, 0.0), axis=-1, keepdims=True)
               + e[:, TOP_K + k:TOP_K + k + 1].astype(F32))
        rows.append(jnp.sum(jnp.where(eye, col, 0.0), axis=0, keepdims=True))
    j = lax.broadcasted_iota(jnp.int32, (SORT_ROWS, tm), 0).astype(F32)
    perm = ((j == rows[0]) | (j == rows[1])).astype(BF16)
    hb = h_ref[...].astype(BF16)

    def start_pieces(table_ref, s, lo, hi):
        def body(g, carry):
            src = pl.multiple_of(g * PIECE, PIECE)
            dst = pl.multiple_of(table_ref[0, 0, g] * PIECE, PIECE)
            pltpu.make_async_copy(sbuf.at[s, pl.ds(src, PIECE), :], xs_out.at[pl.ds(dst, PIECE), :],
                                  sem.at[s]).start()
            return carry
        lax.fori_loop(lo, hi, body, 0)

    def wait_pieces(s, count):
        def body(g, carry):
            pltpu.make_async_copy(sbuf.at[s, pl.ds(0, PIECE), :], xs_out.at[pl.ds(0, PIECE), :], sem.at[s]).wait()
            return carry
        lax.fori_loop(0, count, body, 0)

    n_prev = np_ref[jnp.maximum(i - 1, 0)]

    @pl.when(i > 0)
    def _():
        start_pieces(dprev_ref, 1 - slot, 0, n_prev)

    sbuf[slot] = _mm(perm, hb)

    @pl.when(i > 0)
    def _():
        wait_pieces(1 - slot, n_prev)

    @pl.when(i == nt - 1)
    def _():
        start_pieces(dst_ref, slot, 0, np_ref[i])
        wait_pieces(slot, np_ref[i])


def _dispatch(hf, eidx, a8, dst8, npieces, last_blk, n_used, padded_rows):
    n, d = hf.shape
    tm = ROW_TILE
    nt = n // tm
    grid_spec = pltpu.PrefetchScalarGridSpec(
        num_scalar_prefetch=3,
        grid=(nt,),
        in_specs=[pl.BlockSpec((1, 1, MAX_PIECES), lambda i, lb, nu, npc: (i, 0, 0), memory_space=pltpu.SMEM),
                  pl.BlockSpec((1, 1, MAX_PIECES), lambda i, lb, nu, npc: (jnp.maximum(i - 1, 0), 0, 0),
                               memory_space=pltpu.SMEM),
                  pl.BlockSpec((tm, LANES), lambda i, lb, nu, npc: (i, 0)),
                  pl.BlockSpec((1, SUBLANES, LANES), lambda i, lb, nu, npc: (i, 0, 0)),
                  pl.BlockSpec((tm, d), lambda i, lb, nu, npc: (i, 0))],
        out_specs=pl.BlockSpec(memory_space=pl.ANY),
        scratch_shapes=[pltpu.VMEM((MOE_BM, d), F32), pltpu.VMEM((2, SORT_ROWS, d), F32),
                        pltpu.SemaphoreType.DMA((2,)), pltpu.SemaphoreType.DMA(())])
    return pl.pallas_call(
        _dispatch_kernel,
        grid_spec=grid_spec,
        out_shape=jax.ShapeDtypeStruct((padded_rows, d), F32),
        compiler_params=_cparams(("arbitrary",)),
        name="dispatch",
    )(last_blk, n_used, npieces, dst8, dst8, eidx, a8, hf)


def _experts_kernel(be_ref, nu_ref, x_ref, w1_ref, w3_ref, w2_ref, o_ref, w1b, w3b, w2b):
    i = pl.program_id(0)
    used = nu_ref[0]

    @pl.when((i == 0) | (be_ref[i] != be_ref[jnp.maximum(i - 1, 0)]))
    def _():
        w1b[...] = w1_ref[0, 0].astype(BF16)
        w3b[...] = w3_ref[0, 0].astype(BF16)
        w2b[...] = w2_ref[0, 0].astype(BF16)

    @pl.when(i < used)
    def _():
        xb = x_ref[...].astype(BF16)
        act = (_silu(_mm(xb, w1b[...])) * _mm(xb, w3b[...])).astype(BF16)
        o_ref[...] = _mm(act, w2b[...])

    @pl.when(i >= used)
    def _():
        o_ref[...] = jnp.zeros_like(o_ref)


def _experts(xs, block_e, n_used, w1, w3, w2, layer):
    padded_rows, d = xs.shape
    bm = MOE_BM
    nblk = padded_rows // bm
    de = w1.shape[-1]
    grid_spec = pltpu.PrefetchScalarGridSpec(
        num_scalar_prefetch=2,
        grid=(nblk,),
        in_specs=[pl.BlockSpec((bm, d), lambda i, be, nu: (jnp.maximum(jnp.minimum(i, nu[0] - 1), 0), 0)),
                  pl.BlockSpec((1, 1, d, de), lambda i, be, nu: (layer, be[i], 0, 0)),
                  pl.BlockSpec((1, 1, d, de), lambda i, be, nu: (layer, be[i], 0, 0)),
                  pl.BlockSpec((1, 1, de, d), lambda i, be, nu: (layer, be[i], 0, 0))],
        out_specs=pl.BlockSpec((bm, d), lambda i, be, nu: (i, 0)),
        scratch_shapes=[pltpu.VMEM((d, de), BF16), pltpu.VMEM((d, de), BF16), pltpu.VMEM((de, d), BF16)])
    return pl.pallas_call(
        _experts_kernel,
        grid_spec=grid_spec,
        out_shape=jax.ShapeDtypeStruct((padded_rows, d), F32),
        compiler_params=_cparams(("arbitrary",)),
        name="experts",
    )(block_e, n_used, xs, w1, w3, w2)


def _gather_rows(idx_ref, ys_hbm, ybuf, sem, slot, lo, hi):
    def body(r, carry):
        pltpu.make_async_copy(ys_hbm.at[pl.ds(idx_ref[0, 0, r], 1), :], ybuf.at[slot, pl.ds(r, 1), :],
                              sem.at[slot]).start()
        return carry
    lax.fori_loop(lo, hi, body, 0, unroll=8)


def _combine_rows(pc_ref, x_ref, gt_ref, g2_ref, ys_hbm, ybuf, sem):
    i = pl.program_id(0)
    tm = x_ref.shape[0]
    slot = i % 2

    @pl.when(i == 0)
    def _():
        _gather_rows(pc_ref, ys_hbm, ybuf, sem, 0, 0, TOP_K * tm)

    pltpu.make_async_copy(ys_hbm.at[pl.ds(0, TOP_K * tm), :], ybuf.at[slot], sem.at[slot]).wait()
    gt = gt_ref[...]
    y = gt[:, 0:1] * ybuf[slot, 0:tm, :] + gt[:, 1:2] * ybuf[slot, tm:2 * tm, :]
    return x_ref[...] + g2_ref[0] * y


def _combine_final_kernel(pc_ref, pn_ref, x_ref, gt_ref, g2_ref, fw_ref, ys_hbm, o_ref, ybuf, sem):
    i = pl.program_id(0)
    tm = x_ref.shape[0]

    @pl.when(i + 1 < pl.num_programs(0))
    def _():
        _gather_rows(pn_ref, ys_hbm, ybuf, sem, 1 - i % 2, 0, TOP_K * tm)

    xn = _combine_rows(pc_ref, x_ref, gt_ref, g2_ref, ys_hbm, ybuf, sem)
    ms = jnp.mean(xn * xn, axis=-1, keepdims=True)
    o_ref[...] = xn * lax.rsqrt(ms + EPS) * fw_ref[...]


PROJ_CHUNK = 512


def _combine_inproj_kernel(pc_ref, pn_ref, x_ref, gt_ref, g2_ref, nw_ref, sc_ref, sh_ref, w_ref, wabt_ref, ys_hbm,
                           xo_ref, o_ref, ot_ref, ybuf, sem):
    i = pl.program_id(0)
    tm = x_ref.shape[0]
    xn = _combine_rows(pc_ref, x_ref, gt_ref, g2_ref, ys_hbm, ybuf, sem)
    xo_ref[...] = xn
    h = _modnorm(xn, nw_ref[...], sc_ref[0], sh_ref[0]).astype(BF16)
    ot_ref[0] = _nt(wabt_ref[0], h)
    ncol = o_ref.shape[2]
    starts = list(range(0, ncol, PROJ_CHUNK))
    per = TOP_K * tm // len(starts)
    nxt = 1 - i % 2
    for j, c0 in enumerate(starts):
        for r in range(j * per, TOP_K * tm if j == len(starts) - 1 else (j + 1) * per):
            pltpu.make_async_copy(ys_hbm.at[pl.ds(pn_ref[0, 0, r], 1), :], ybuf.at[nxt, pl.ds(r, 1), :],
                                  sem.at[nxt]).start()
        c1 = min(c0 + PROJ_CHUNK, ncol)
        o_ref[0, :, c0:c1] = _mm(h, w_ref[0, :, c0:c1])

    @pl.when(i + 1 == pl.num_programs(0))
    def _():
        pltpu.make_async_copy(ys_hbm.at[pl.ds(0, TOP_K * tm), :], ybuf.at[nxt], sem.at[nxt]).wait()


def _combine_specs(n, d, t):
    tm = MOE_TM
    nt = n // tm
    per_b = t // tm
    specs = [pl.BlockSpec((1, 1, TOP_K * tm), lambda i: (i, 0, 0), memory_space=pltpu.SMEM),
             pl.BlockSpec((1, 1, TOP_K * tm), lambda i: (jnp.minimum(i + 1, nt - 1), 0, 0), memory_space=pltpu.SMEM),
             pl.BlockSpec((tm, d), lambda i: (i, 0)),
             pl.BlockSpec((tm, LANES), lambda i: (i, 0)),
             pl.BlockSpec((1, 1, d), lambda i: (i // per_b, 0, 0))]
    scratch = [pltpu.VMEM((2, TOP_K * tm, d), F32), pltpu.SemaphoreType.DMA((2,))]
    return tm, nt, per_b, specs, scratch


def _combine_final(x2, gates, g2, fw, ys, pos3, t):
    n, d = x2.shape
    tm, nt, per_b, specs, scratch = _combine_specs(n, d, t)
    return pl.pallas_call(
        _combine_final_kernel,
        grid=(nt,),
        in_specs=specs + [pl.BlockSpec((1, d), lambda i: (0, 0)), pl.BlockSpec(memory_space=pl.ANY)],
        out_specs=pl.BlockSpec((tm, d), lambda i: (i, 0)),
        out_shape=jax.ShapeDtypeStruct((n, d), F32),
        scratch_shapes=scratch,
        compiler_params=_cparams(("arbitrary",)),
        name="combine",
    )(pos3, pos3, x2, gates, g2, fw, ys)


def _combine_inproj(x2, gates, g2, ys, pos3, t, nw, sc, sh, w_pad, wabt, layer):
    n, d = x2.shape
    bsz = n // t
    nab = wabt.shape[1]
    tm, nt, per_b, specs, scratch = _combine_specs(n, d, t)
    perb = pl.BlockSpec((1, 1, d), lambda i: (i // per_b, 0, 0))
    return pl.pallas_call(
        _combine_inproj_kernel,
        grid=(nt,),
        in_specs=specs + [pl.BlockSpec((1, d), lambda i: (0, 0)), perb, perb,
                          pl.BlockSpec((1, d, D_IN_PAD), lambda i: (layer, 0, 0)),
                          pl.BlockSpec((1, nab, d), lambda i: (layer, 0, 0)),
                          pl.BlockSpec(memory_space=pl.ANY)],
        out_specs=[pl.BlockSpec((tm, d), lambda i: (i, 0)),
                   pl.BlockSpec((1, tm, D_IN_PAD), lambda i: (i // per_b, i % per_b, 0)),
                   pl.BlockSpec((1, nab, tm), lambda i: (i // per_b, 0, i % per_b))],
        out_shape=[jax.ShapeDtypeStruct((n, d), F32), jax.ShapeDtypeStruct((bsz, t, D_IN_PAD), F32),
                   jax.ShapeDtypeStruct((bsz, nab, t), F32)],
        scratch_shapes=scratch,
        compiler_params=_cparams(("arbitrary",)),
        name="combine_inproj",
    )(pos3, pos3, x2, gates, g2, nw, sc, sh, w_pad, wabt, ys)


def _moe_plan(eidx, hist):
    n = eidx.shape[0]
    bm = MOE_BM
    ntile = n // ROW_TILE
    experts = jnp.arange(N_EXPERTS, dtype=jnp.int32)
    hist = hist[:, 0, :N_EXPERTS].astype(jnp.int32)
    cnt = (hist + PIECE - 1) // PIECE * PIECE
    seg_start = jnp.cumsum(cnt, axis=1) - cnt
    pieces = cnt // PIECE
    piece_end = jnp.cumsum(pieces, axis=1)
    npieces = piece_end[:, -1].astype(jnp.int32)
    sizes = jnp.sum(cnt, axis=0)
    base = jnp.cumsum(cnt, axis=0) - cnt
    padded = (sizes + bm - 1) // bm * bm
    pad_end = jnp.cumsum(padded)
    tbl = (pad_end - padded)[None, :] + base
    e = eidx[:, :TOP_K].reshape(ntile, ROW_TILE, TOP_K)
    onehot = e[..., None] == experts
    pos = jnp.sum(jnp.where(onehot, tbl[:, None, None, :], 0), axis=-1).reshape(n, TOP_K) + eidx[:, TOP_K:2 * TOP_K]
    g = jnp.arange(MAX_PIECES, dtype=jnp.int32)
    e_of_g = jnp.minimum(jnp.sum(piece_end[:, None, :] <= g[None, :, None], axis=-1), N_EXPERTS - 1)
    sel = e_of_g[..., None] == experts
    first = jnp.sum(jnp.where(sel, (piece_end - pieces)[:, None, :], 0), axis=-1)
    dst8 = jnp.sum(jnp.where(sel, tbl[:, None, :], 0), axis=-1) // PIECE + (g[None, :] - first)
    dst8 = jnp.where(g[None, :] < npieces[:, None], dst8, 0).astype(jnp.int32).reshape(ntile, 1, MAX_PIECES)
    seg8 = jnp.broadcast_to(jnp.pad(seg_start.astype(F32), ((0, 0), (0, LANES - N_EXPERTS)))[:, None, :],
                            (ntile, SUBLANES, LANES))
    padded_rows = (n * TOP_K + ntile * N_EXPERTS * (PIECE - 1) + N_EXPERTS * (bm - 1) + bm - 1) // bm * bm
    n_blocks = padded_rows // bm
    blk_start = jnp.arange(n_blocks, dtype=jnp.int32) * bm
    block_e = jnp.minimum(jnp.sum(pad_end[None, :] <= blk_start[:, None], axis=1), N_EXPERTS - 1).astype(jnp.int32)
    n_used = (pad_end[-1] // bm).astype(jnp.int32).reshape(1)
    last_blk = jnp.where(padded > 0, pad_end // bm - 1, -1).astype(jnp.int32)
    nt = n // MOE_TM
    pos3 = jnp.transpose(pos.astype(jnp.int32).reshape(nt, MOE_TM, TOP_K), (0, 2, 1)).reshape(nt, 1, TOP_K * MOE_TM)
    return pos3, seg8, dst8, npieces, block_e, n_used, last_blk, padded_rows


def kernel(x, c, norm_mix_w, norm_ffn_w, w_ada, b_ada, w_in, conv_a_w, na_rpb, gdn_conv_w, gdn_a_log, gdn_dt_bias,
           gdn_norm_w, w_out, router_group_w, router_group_b, router_expert_w, router_expert_b, expert_w1,
           expert_w3, expert_w2, final_norm_w):
    bsz, t, d = x.shape
    depth = w_ada.shape[0]
    n = bsz * t
    assert t % ROW_TILE == 0 and (t // GRID_W) % NA_KH == 0, "sequence length must be a multiple of 512"
    mod = _ada(c, w_ada, b_ada)
    w_in_pad = jnp.pad(w_in.astype(BF16), ((0, 0), (0, 0), (0, D_IN_PAD - D_IN)))
    wabt = jnp.transpose(w_in[:, :, COL_AB:], (0, 2, 1)).astype(BF16)
    w_out_b = w_out.astype(BF16)
    wr = jnp.pad(jnp.concatenate([router_group_w, router_expert_w], axis=-1),
                 ((0, 0), (0, 0), (0, LANES - N_GROUPS - N_EXPERTS)))
    wr_hi = wr.astype(BF16)
    wr = jnp.stack([wr_hi, (wr - wr_hi.astype(F32)).astype(BF16)], axis=1)
    br = jnp.pad(jnp.concatenate([router_group_b, router_expert_b], axis=-1),
                 ((0, 0), (0, LANES - N_GROUPS - N_EXPERTS)))
    fw = final_norm_w.reshape(1, d)
    na_bias = _na_bias_table(na_rpb)
    mods = [[mod[l, :, j * d:(j + 1) * d].reshape(bsz, 1, d) for j in range(6)] for l in range(depth)]
    proj, abt = _inproj(x, norm_mix_w[0].reshape(1, d), mods[0][1], mods[0][0], w_in_pad, wabt, 0)
    for l in range(depth):
        sh1, sc1, g1, sh2, sc2, g2 = mods[l]
        y_na = _na(proj, na_bias, l)
        o_f, o_b = _gdnscan(*_gdnchunk(proj, abt, gdn_conv_w[l], gdn_a_log[l], gdn_dt_bias[l]))
        x, hf, eidx, gates, hist = _outproj(x, conv_a_w[l], y_na, o_f, o_b, proj, gdn_norm_w[l].reshape(1, GDN_DV),
                                            w_out_b, g1, norm_ffn_w[l].reshape(1, d), sc2, sh2, wr[l],
                                            br[l].reshape(1, LANES), l)
        eidx = eidx.reshape(n, LANES)
        pos3, seg8, dst8, npieces, block_e, n_used, last_blk, padded_rows = _moe_plan(
            eidx, hist.reshape(-1, SUBLANES, LANES))
        xs = _dispatch(hf.reshape(n, d), eidx, seg8, dst8, npieces, last_blk, n_used, padded_rows)
        ys = _experts(xs, block_e, n_used, expert_w1, expert_w3, expert_w2, l)
        x2, gt = x.reshape(n, d), gates.reshape(n, LANES)
        if l + 1 < depth:
            x2, proj, abt = _combine_inproj(x2, gt, g2, ys, pos3, t, norm_mix_w[l + 1].reshape(1, d), mods[l + 1][1],
                                            mods[l + 1][0], w_in_pad, wabt, l + 1)
            x = x2.reshape(bsz, t, d)
        else:
            x = _combine_final(x2, gt, g2, fw, ys, pos3, t).reshape(bsz, t, d)
    return x
```

```python
import jax
import jax.numpy as jnp
from jax import lax
from jax.experimental import pallas as pl
from jax.experimental.pallas import tpu as pltpu

F32 = jnp.float32
BF16 = jnp.bfloat16

EPS = 1e-6
GRID_W = 64
CONV_W = 256
NA_HEADS = 4
NA_DH = 64
NA_W = NA_HEADS * NA_DH
NA_KH = 8
NA_KW = 16
GDN_HEADS = 4
GDN_DK = 128
GDN_DV = 128
GDN_W = GDN_HEADS * GDN_DV
GDN_QKV = 2 * GDN_HEADS * GDN_DK + GDN_W
GDN_CHUNK = 64
N_GROUPS = 4
EXPERTS_PER_GROUP = 8
N_EXPERTS = N_GROUPS * EXPERTS_PER_GROUP
TOP_K = 2

COL_CONV = 0
COL_NA = 3 * CONV_W
COL_GQKV = COL_NA + 3 * NA_W
COL_GZ = COL_GQKV + GDN_QKV
COL_AB = COL_GZ + GDN_W
D_IN = COL_AB + 4 * GDN_HEADS
D_IN_PAD = COL_AB + 128
LANES = 128
SUBLANES = 8

NEG = -1e30
VMEM_LIMIT = 56 * 1024 * 1024

MOE_BM = 512
ROW_TILE = 512


def _cparams(sem):
    return pltpu.CompilerParams(dimension_semantics=sem, vmem_limit_bytes=VMEM_LIMIT)


def _silu(x):
    return x * (1.0 / (1.0 + jnp.exp(-x)))


def _nt(a, b):
    return lax.dot_general(a, b, (((1,), (1,)), ((), ())), preferred_element_type=F32)


def _tn(a, b):
    return lax.dot_general(a, b, (((0,), (0,)), ((), ())), preferred_element_type=F32)


def _mm(a, b):
    return jnp.dot(a, b, preferred_element_type=F32)


def _ada_kernel(ct_ref, w_ref, b_ref, o_ref):
    w = w_ref[0]
    nb = ct_ref.shape[1]
    for r in range(nb):
        col = _silu(ct_ref[:, r:r + 1])
        o_ref[0, r:r + 1, :] = jnp.sum(w * col, axis=0, keepdims=True) + b_ref[0]


def _ada(c, w_ada, b_ada):
    depth, d, n6 = w_ada.shape
    bsz = c.shape[0]
    tn = 512
    return pl.pallas_call(
        _ada_kernel,
        grid=(depth, n6 // tn),
        in_specs=[pl.BlockSpec((d, bsz), lambda l, j: (0, 0)),
                  pl.BlockSpec((1, d, tn), lambda l, j: (l, 0, j)),
                  pl.BlockSpec((1, 1, tn), lambda l, j: (l, 0, j))],
        out_specs=pl.BlockSpec((1, bsz, tn), lambda l, j: (l, 0, j)),
        out_shape=jax.ShapeDtypeStruct((depth, bsz, n6), F32),
        compiler_params=_cparams(("parallel", "parallel")),
        name="ada",
    )(c.T, w_ada, b_ada.reshape(depth, 1, n6))


def _modnorm(x, nw, sc, sh):
    ms = jnp.mean(x * x, axis=-1, keepdims=True)
    y = x * lax.rsqrt(ms + EPS)
    return (y * nw) * (1.0 + sc) + sh


def _project(h, w_ref, wabt_ref, o_ref, ot_ref):
    o_ref[0] = _mm(h, w_ref[0])
    ot_ref[0] = _nt(wabt_ref[0], h)


def _inproj_kernel(x_ref, nw_ref, sc_ref, sh_ref, w_ref, wabt_ref, o_ref, ot_ref):
    h = _modnorm(x_ref[0], nw_ref[...], sc_ref[0], sh_ref[0]).astype(BF16)
    _project(h, w_ref, wabt_ref, o_ref, ot_ref)


def _inproj(x, nw, sc, sh, w_pad, wabt, layer):
    bsz, t, d = x.shape
    tm = min(ROW_TILE, t)
    nab = wabt.shape[1]
    return pl.pallas_call(
        _inproj_kernel,
        grid=(bsz, t // tm),
        in_specs=[pl.BlockSpec((1, tm, d), lambda b, i: (b, i, 0)),
                  pl.BlockSpec((1, d), lambda b, i: (0, 0)),
                  pl.BlockSpec((1, 1, d), lambda b, i: (b, 0, 0)),
                  pl.BlockSpec((1, 1, d), lambda b, i: (b, 0, 0)),
                  pl.BlockSpec((1, d, D_IN_PAD), lambda b, i: (layer, 0, 0)),
                  pl.BlockSpec((1, nab, d), lambda b, i: (layer, 0, 0))],
        out_specs=[pl.BlockSpec((1, tm, D_IN_PAD), lambda b, i: (b, i, 0)),
                   pl.BlockSpec((1, nab, tm), lambda b, i: (b, 0, i))],
        out_shape=[jax.ShapeDtypeStruct((bsz, t, D_IN_PAD), F32),
                   jax.ShapeDtypeStruct((bsz, nab, t), F32)],
        compiler_params=_cparams(("parallel", "parallel")),
        name="inproj",
    )(x, nw, sc, sh, w_pad, wabt)


def _dwconv3(u, prev_row, next_row, w_ref):
    tt = u.shape[0]
    row = lax.broadcasted_iota(jnp.int32, u.shape, 0)
    dn = jnp.where(row == 0, prev_row, pltpu.roll(u, 1, axis=0))
    up = jnp.where(row == tt - 1, next_row, pltpu.roll(u, tt - 1, axis=0))
    return w_ref[0:1, :] * dn + w_ref[1:2, :] * u + w_ref[2:3, :] * up


def _halo_specs(tt, t, width, colblk):
    nsub = tt // SUBLANES
    last = t // SUBLANES - 1
    return [pl.BlockSpec((1, tt, width), lambda b, i: (b, i, colblk)),
            pl.BlockSpec((1, SUBLANES, width), lambda b, i: (b, jnp.maximum(i * nsub - 1, 0), colblk)),
            pl.BlockSpec((1, SUBLANES, width), lambda b, i: (b, jnp.minimum((i + 1) * nsub, last), colblk))]


def _convmix_tile(m_ref, p_ref, n_ref, w_ref):
    i = pl.program_id(1)
    nt = pl.num_programs(1)
    m = m_ref[0]
    u = m[:, CONV_W:2 * CONV_W] * m[:, 2 * CONV_W:]
    p = p_ref[0]
    n = n_ref[0]
    pu = p[SUBLANES - 1:SUBLANES, CONV_W:2 * CONV_W] * p[SUBLANES - 1:SUBLANES, 2 * CONV_W:]
    nu = n[0:1, CONV_W:2 * CONV_W] * n[0:1, 2 * CONV_W:]
    pu = jnp.where(i == 0, 0.0, pu)
    nu = jnp.where(i == nt - 1, 0.0, nu)
    return m[:, :CONV_W] * _dwconv3(u, pu, nu, w_ref)


NA_RB = NA_KH // 2
NA_TOK = NA_RB * GRID_W
NA_KEYS = 3 * NA_TOK


def _na_bias_table(rpb):
    col = jnp.arange(GRID_W)
    cstart = jnp.clip(col - NA_KW // 2, 0, GRID_W - NA_KW)
    kc = jnp.arange(GRID_W)
    valid = (kc[None, :] >= cstart[:, None]) & (kc[None, :] < cstart[:, None] + NA_KW)
    dc = kc[None, :] - col[:, None] + (NA_KW - 1)
    onehot = (dc[None] == jnp.arange(2 * NA_KW - 1)[:, None, None]) & valid[None]
    cols = jnp.einsum('lhrd,dck->lhrck', rpb, onehot.astype(F32), precision=lax.Precision.HIGHEST)
    cols = jnp.where(valid, cols, NEG)
    lo = NA_KH - 1 - NA_RB
    blk = jnp.stack([cols[:, :, lo - j:lo - j + 3 * NA_RB] for j in range(NA_RB)], axis=2)
    blk = jnp.transpose(blk, (0, 1, 2, 4, 3, 5))
    return blk.reshape(rpb.shape[0], NA_HEADS, NA_TOK, NA_KEYS)


def _na_kernel(q_ref, kp_ref, kc_ref, kn_ref, vp_ref, vc_ref, vn_ref, bias_ref, o_ref, *, rows):
    i = pl.program_id(1)
    kbuf = jnp.concatenate([kp_ref[0], kc_ref[0], kn_ref[0]], axis=0).astype(BF16)
    vbuf = jnp.concatenate([vp_ref[0], vc_ref[0], vn_ref[0]], axis=0).astype(BF16)
    q = q_ref[0] * (NA_DH ** -0.5)
    head_of_lane = lax.broadcasted_iota(jnp.int32, (1, NA_W), 1) // NA_DH
    qrow = i * NA_RB + lax.broadcasted_iota(jnp.int32, (NA_TOK, NA_KEYS), 0) // GRID_W
    krow = (i - 1) * NA_RB + lax.broadcasted_iota(jnp.int32, (NA_TOK, NA_KEYS), 1) // GRID_W
    rs = jnp.clip(qrow - NA_KH // 2, 0, rows - NA_KH)
    row_mask = jnp.where((krow >= rs) & (krow < rs + NA_KH), 0.0, NEG)
    acc = jnp.zeros((NA_TOK, NA_W), F32)
    for h in range(NA_HEADS):
        mine = head_of_lane == h
        s = _nt(jnp.where(mine, q, 0.0).astype(BF16), kbuf) + (bias_ref[0, h] + row_mask)
        m = jnp.max(s, axis=-1, keepdims=True)
        p = jnp.exp(s - m)
        l = jnp.sum(p, axis=-1, keepdims=True)
        o = _mm((p / l).astype(BF16), vbuf)
        acc = acc + jnp.where(mine, o, 0.0)
    o_ref[0] = acc


def _na(proj, bias_tbl, layer):
    bsz, t, _ = proj.shape
    rows = t // GRID_W
    nblk = rows // NA_RB
    qc, kc, vc = COL_NA // NA_W, COL_NA // NA_W + 1, COL_NA // NA_W + 2

    def spec(col, shift):
        return pl.BlockSpec((1, NA_TOK, NA_W), lambda b, i: (b, jnp.clip(i + shift, 0, nblk - 1), col))

    def na_kernel(*refs):
        _na_kernel(*refs, rows=rows)

    return pl.pallas_call(
        na_kernel,
        grid=(bsz, nblk),
        in_specs=[spec(qc, 0), spec(kc, -1), spec(kc, 0), spec(kc, 1), spec(vc, -1), spec(vc, 0), spec(vc, 1),
                  pl.BlockSpec((1, NA_HEADS, NA_TOK, NA_KEYS), lambda b, i: (layer, 0, 0, 0))],
        out_specs=pl.BlockSpec((1, NA_TOK, NA_W), lambda b, i: (b, i, 0)),
        out_shape=jax.ShapeDtypeStruct((bsz, t, NA_W), F32),
        compiler_params=_cparams(("parallel", "parallel")),
        name="na",
    )(proj, proj, proj, proj, proj, proj, proj, bias_tbl)


def _gdn_qkv(m_ref, p_ref, n_ref, w_ref):
    i = pl.program_id(1)
    nt = pl.num_programs(1)
    pu = jnp.where(i == 0, 0.0, p_ref[0, SUBLANES - 1:SUBLANES, :])
    nu = jnp.where(i == nt - 1, 0.0, n_ref[0, 0:1, :])
    c = _silu(_dwconv3(m_ref[0], pu, nu, w_ref))
    heads = []
    for hh in range(GDN_QKV // GDN_DK):
        xh = c[:, hh * GDN_DK:(hh + 1) * GDN_DK]
        if hh < 2 * GDN_HEADS:
            xh = xh * lax.rsqrt(jnp.sum(xh * xh, axis=-1, keepdims=True) + EPS)
            if hh < GDN_HEADS:
                xh = xh * (GDN_DK ** -0.5)
        heads.append(xh)
    return heads[:GDN_HEADS], heads[GDN_HEADS:2 * GDN_HEADS], heads[2 * GDN_HEADS:]


GDN_GS = 128
GDN_UNITS = 2 * GDN_HEADS
GDN_GC = GDN_GS // GDN_CHUNK
GDN_NG = 2
GDN_TS = GDN_NG * GDN_GS


def _softplus(x):
    return jnp.maximum(x, 0.0) + jnp.log1p(jnp.exp(-jnp.abs(x)))


def _seg_cumsum(x, axis, reverse):
    n = x.shape[axis]
    pos = lax.broadcasted_iota(jnp.int32, x.shape, axis) & (GDN_CHUNK - 1)
    s = 1
    while s < GDN_CHUNK:
        if reverse:
            x = x + jnp.where(pos < GDN_CHUNK - s, pltpu.roll(x, n - s, axis=axis), 0.0)
        else:
            x = x + jnp.where(pos >= s, pltpu.roll(x, s, axis=axis), 0.0)
        s *= 2
    return x


def _gdnchunk_kernel(m_ref, p_ref, n_ref, cw_ref, ab_ref, abt_ref, alr_ref, dtr_ref, alc_ref, dtc_ref,
                     u_ref, wq_ref, kd_ref, in_ref, gl_ref):
    gs, c, nh = GDN_GS, GDN_CHUNK, GDN_HEADS
    ri = lax.broadcasted_iota(jnp.int32, (gs, gs), 0)
    ci = lax.broadcasted_iota(jnp.int32, (gs, gs), 1)
    same = (ri // c) == (ci // c)
    eye = (ri == ci).astype(F32)
    rowc = lax.broadcasted_iota(jnp.int32, (gs, 1), 0) // c
    ab = ab_ref[0]
    abt = abt_ref[0]
    graw_c = -jnp.exp(alr_ref[...]) * _softplus(ab + dtr_ref[...])
    graw_r = -jnp.exp(alc_ref[...]) * _softplus(abt[0:2 * nh] + dtc_ref[...])
    beta_c = 1.0 / (1.0 + jnp.exp(-ab))
    g_col = [_seg_cumsum(graw_c, 0, False), _seg_cumsum(graw_c, 0, True)]
    g_row = [_seg_cumsum(graw_r, 1, False), _seg_cumsum(graw_r, 1, True)]
    incl = [same & (ri >= ci), same & (ri <= ci)]
    strict = [same & (ri > ci), same & (ri < ci)]

    qs, ks, vs = _gdn_qkv(m_ref, p_ref, n_ref, cw_ref)
    groups = [slice(g * gs, (g + 1) * gs) for g in range(GDN_NG)]
    grams = [[_nt(jnp.concatenate([q[rg], k[rg]], axis=0).astype(BF16), k[rg].astype(BF16))
              for q, k in zip(qs, ks)] for rg in groups]

    units = [(g, d, h) for g in range(GDN_NG) for d in range(2) for h in range(nh)]
    gcs, bcs, intras, xs, ps = [], [], [], [], []
    for g, d, h in units:
        col = d * nh + h
        rg = groups[g]
        gc = g_col[d][rg, col:col + 1]
        gr = g_row[d][col:col + 1, rg]
        bc = beta_c[rg, 2 * nh + col:2 * nh + col + 1]
        e_incl = jnp.exp(jnp.where(incl[d], gc - gr, NEG))
        a = grams[g][h][gs:] * bc * jnp.where(strict[d], e_incl, 0.0)
        gcs.append(gc), bcs.append(bc)
        intras.append(grams[g][h][:gs] * e_incl)
        xs.append(eye - a), ps.append(a)
    for _ in range(5):
        pbs = [p.astype(BF16) for p in ps]
        ps = [_mm(pb, pb) for pb in pbs]
        xs = [x + _mm(x.astype(BF16), p.astype(BF16)) for x, p in zip(xs, ps)]
    egs = [jnp.exp(gc) for gc in gcs]
    sols = [_mm(x.astype(BF16),
                jnp.concatenate([vs[h][groups[g]] * bc, ks[h][groups[g]] * (bc * eg)], axis=1).astype(BF16))
            for (g, d, h), x, bc, eg in zip(units, xs, bcs, egs)]
    for (g, d, h), gc, eg, sol, intra in zip(units, gcs, egs, sols, intras):
        col = d * nh + h
        rg = groups[g]
        glast_col = jnp.zeros_like(gc)
        for n in range(GDN_GC):
            r = n * c if d == 1 else (n + 1) * c - 1
            glast = gc[r:r + 1, :]
            glast_col = jnp.where(rowc == n, glast, glast_col)
            gl_ref[0, g, col * GDN_GC + n:col * GDN_GC + n + 1, :] = jnp.broadcast_to(jnp.exp(glast), (1, LANES))
        u_ref[0, col, rg, :] = sol[:, :GDN_DV]
        w = sol[:, GDN_DV:].astype(BF16)
        qd = (qs[h][rg] * eg).astype(BF16)
        for n in range(GDN_GC):
            base = 2 * (g * gs + n * c)
            wq_ref[0, col, base:base + c, :] = w[n * c:(n + 1) * c]
            wq_ref[0, col, base + c:base + 2 * c, :] = qd[n * c:(n + 1) * c]
        kd_ref[0, col, rg, :] = (ks[h][rg] * jnp.exp(glast_col - gc)).astype(BF16)
        in_ref[0, col, rg, :] = jnp.concatenate([intra[n * c:(n + 1) * c, n * c:(n + 1) * c] for n in range(GDN_GC)],
                                                axis=0).astype(BF16)


def _gdnchunk(proj, abt, conv_w, a_log, dt_bias):
    bsz, t, _ = proj.shape
    ts, nu = GDN_TS, GDN_UNITS
    nab = abt.shape[1]
    pad = lambda r: jnp.pad(r.reshape(1, -1), ((0, 0), (0, LANES - r.size)))
    alr, dtr = pad(a_log), pad(dt_bias)
    alc, dtc = a_log.reshape(-1, 1), dt_bias.reshape(-1, 1)
    small = lambda shape: pl.BlockSpec(shape, lambda b, i: (0, 0))
    unit = lambda rows, w: pl.BlockSpec((1, nu, rows, w), lambda b, i: (b, 0, i, 0))
    return pl.pallas_call(
        _gdnchunk_kernel,
        grid=(bsz, t // ts),
        in_specs=_halo_specs(ts, t, GDN_QKV, COL_GQKV // GDN_QKV) + [
            small((3, GDN_QKV)),
            pl.BlockSpec((1, ts, LANES), lambda b, i: (b, i, COL_AB // LANES)),
            pl.BlockSpec((1, nab, ts), lambda b, i: (b, 0, i)),
            small((1, LANES)), small((1, LANES)), small((nu, 1)), small((nu, 1))],
        out_specs=[unit(ts, GDN_DV), unit(2 * ts, GDN_DK), unit(ts, GDN_DK), unit(ts, GDN_CHUNK),
                   pl.BlockSpec((1, GDN_NG, nu * GDN_GC, LANES), lambda b, i: (b, i, 0, 0))],
        out_shape=[jax.ShapeDtypeStruct((bsz, nu, t, GDN_DV), F32),
                   jax.ShapeDtypeStruct((bsz, nu, 2 * t, GDN_DK), BF16),
                   jax.ShapeDtypeStruct((bsz, nu, t, GDN_DK), BF16),
                   jax.ShapeDtypeStruct((bsz, nu, t, GDN_CHUNK), BF16),
                   jax.ShapeDtypeStruct((bsz, t // GDN_GS, nu * GDN_GC, LANES), F32)],
        compiler_params=_cparams(("parallel", "parallel")),
        name="gdnchunk",
    )(proj, proj, proj, conv_w, proj, abt, alr, dtr, alc, dtc)


GDN_SB = 4
GDN_ST = GDN_SB * GDN_CHUNK


def _gdnscan_kernel(uf, ub, wqf, wqb, kdf, kdb, inf, inb, glf, glb, of_ref, ob_ref, s_ref):
    i = pl.program_id(0)

    @pl.when(i == 0)
    def _():
        s_ref[...] = jnp.zeros_like(s_ref)

    bsz = uf.shape[0]
    c, nh = GDN_CHUNK, GDN_HEADS
    chains = [(b, d, h) for b in range(bsz) for d in range(2) for h in range(nh)]
    for step in range(GDN_SB):
        zs, vns = [], []
        for b, d, h in chains:
            cc = step if d == 0 else GDN_SB - 1 - step
            wq = (wqf, wqb)[d]
            st = s_ref[(b * 2 + d) * nh + h]
            zs.append(_mm(wq[b, h, 2 * cc * c:(2 * cc + 2) * c, :], st.astype(BF16)))
        for (b, d, h), z in zip(chains, zs):
            cc = step if d == 0 else GDN_SB - 1 - step
            u = (uf, ub)[d]
            vns.append((u[b, h, cc * c:(cc + 1) * c, :] - z[:c]).astype(BF16))
        for (b, d, h), z, vn in zip(chains, zs, vns):
            cc = step if d == 0 else GDN_SB - 1 - step
            rows = slice(cc * c, (cc + 1) * c)
            intra = (inf, inb)[d]
            o_ref = (of_ref, ob_ref)[d]
            o_ref[b, rows, h * GDN_DV:(h + 1) * GDN_DV] = z[c:] + _mm(intra[b, h, rows, :], vn)
        for (b, d, h), vn in zip(chains, vns):
            cc = step if d == 0 else GDN_SB - 1 - step
            rows = slice(cc * c, (cc + 1) * c)
            kd = (kdf, kdb)[d]
            gl = (glf, glb)[d]
            r = (d * nh + h) * GDN_GC + cc % GDN_GC
            sidx = (b * 2 + d) * nh + h
            s_ref[sidx] = s_ref[sidx] * gl[b, cc // GDN_GC, r:r + 1, :] + _tn(kd[b, h, rows, :], vn)


def _gdnscan(u, wq, kd, intra, gl):
    bsz, nu, t, _ = u.shape
    nh = GDN_HEADS
    st = GDN_ST
    nb = t // st
    ngs = st // GDN_GS

    def unit(rows, w, d):
        if d == 0:
            return pl.BlockSpec((bsz, nh, rows, w), lambda i: (0, 0, i, 0))
        return pl.BlockSpec((bsz, nh, rows, w), lambda i: (0, 1, nb - 1 - i, 0))

    glspec = lambda d: pl.BlockSpec((bsz, ngs, nu * GDN_GC, LANES),
                                    (lambda i: (0, i, 0, 0)) if d == 0 else (lambda i: (0, nb - 1 - i, 0, 0)))
    return pl.pallas_call(
        _gdnscan_kernel,
        grid=(nb,),
        in_specs=[unit(st, GDN_DV, 0), unit(st, GDN_DV, 1), unit(2 * st, GDN_DK, 0), unit(2 * st, GDN_DK, 1),
                  unit(st, GDN_DK, 0), unit(st, GDN_DK, 1), unit(st, GDN_CHUNK, 0), unit(st, GDN_CHUNK, 1),
                  glspec(0), glspec(1)],
        out_specs=[pl.BlockSpec((bsz, st, GDN_W), lambda i: (0, i, 0)),
                   pl.BlockSpec((bsz, st, GDN_W), lambda i: (0, nb - 1 - i, 0))],
        out_shape=[jax.ShapeDtypeStruct((bsz, t, GDN_W), F32), jax.ShapeDtypeStruct((bsz, t, GDN_W), F32)],
        scratch_shapes=[pltpu.VMEM((bsz * nu, GDN_DK, GDN_DV), F32)],
        compiler_params=_cparams(("arbitrary",)),
        name="gdnscan",
    )(u, u, wq, wq, kd, kd, intra, intra, gl, gl)


OUT_SPLIT = 1


def _outproj_kernel(x_ref, cm_ref, cp_ref, cn_ref, cw_ref, yn_ref, of_ref, ob_ref, z_ref, gnw_ref, wo_ref, g1_ref,
                    nw_ref, sc_ref, sh_ref, wr_ref, br_ref, xo_ref, h_ref, e_ref, g_ref, hist_ref):
    yc = _convmix_tile(cm_ref, cp_ref, cn_ref, cw_ref)
    tm = x_ref.shape[1]
    lane = lax.broadcasted_iota(jnp.int32, (tm // OUT_SPLIT, LANES), 1)
    picks = []
    for r in range(OUT_SPLIT):
        rows = slice(r * (tm // OUT_SPLIT), (r + 1) * (tm // OUT_SPLIT))
        o = of_ref[0, rows, :] + ob_ref[0, rows, :]
        z = z_ref[0, rows, :]
        parts = [yc[rows], yn_ref[0, rows, :]]
        for h in range(GDN_HEADS):
            sl = slice(h * GDN_DV, (h + 1) * GDN_DV)
            oh = o[:, sl]
            oh = oh * lax.rsqrt(jnp.mean(oh * oh, axis=-1, keepdims=True) + EPS) * gnw_ref[...]
            parts.append(oh * _silu(z[:, sl]))
        mixed = _mm(jnp.concatenate(parts, axis=-1).astype(BF16), wo_ref[0])
        xn = x_ref[0, rows, :] + g1_ref[0] * mixed
        xo_ref[0, rows, :] = xn
        hf = _modnorm(xn, nw_ref[...], sc_ref[0], sh_ref[0])
        h_ref[0, rows, :] = hf
        hf_hi = hf.astype(BF16)
        hf_lo = (hf - hf_hi.astype(F32)).astype(BF16)
        logits = (_mm(hf_hi, wr_ref[0]) + (_mm(hf_hi, wr_ref[1]) + _mm(hf_lo, wr_ref[0]))) + br_ref[...]
        gl = jnp.where(lane < N_GROUPS, logits, NEG)
        gm = jnp.max(gl, axis=-1, keepdims=True)
        den = jnp.sum(jnp.exp(gl - gm), axis=-1, keepdims=True)
        grp = jnp.min(jnp.where(gl == gm, lane, LANES), axis=-1, keepdims=True)
        pg_top = 1.0 / den
        ex = lane - N_GROUPS
        in_grp = (ex >= grp * EXPERTS_PER_GROUP) & (ex < (grp + 1) * EXPERTS_PER_GROUP)
        el = jnp.where(in_grp, logits, NEG)
        m1 = jnp.max(el, axis=-1, keepdims=True)
        i1 = jnp.min(jnp.where(el == m1, lane, LANES), axis=-1, keepdims=True)
        el2 = jnp.where(lane == i1, NEG, el)
        m2 = jnp.max(el2, axis=-1, keepdims=True)
        i2 = jnp.min(jnp.where(el2 == m2, lane, LANES), axis=-1, keepdims=True)
        e2 = jnp.exp(m2 - m1)
        w1 = pg_top / (1.0 + e2)
        w2 = pg_top * e2 / (1.0 + e2)
        g_ref[0, rows, :] = jnp.where(lane == 0, w1, jnp.where(lane == 1, w2, 0.0))
        picks.append((i1 - N_GROUPS, i2 - N_GROUPS))
    e1 = jnp.concatenate([p[0] for p in picks], axis=0)
    e2i = jnp.concatenate([p[1] for p in picks], axis=0)
    lane = lax.broadcasted_iota(jnp.int32, (tm, LANES), 1)
    oh1 = (lane == e1).astype(F32)
    oh2 = (lane == e2i).astype(F32)
    both = oh1 + oh2
    earlier = (lax.broadcasted_iota(jnp.int32, (tm, tm), 0) > lax.broadcasted_iota(jnp.int32, (tm, tm), 1))
    cnt = _mm(earlier.astype(BF16), both.astype(BF16))
    r1 = jnp.sum(cnt * oh1, axis=-1, keepdims=True).astype(jnp.int32)
    r2 = jnp.sum(cnt * oh2, axis=-1, keepdims=True).astype(jnp.int32)
    e_ref[0] = jnp.where(lane == 0, e1, jnp.where(lane == 1, e2i, jnp.where(lane == 2, r1, jnp.where(lane == 3, r2, 0))))
    hist_ref[0, 0] = jnp.broadcast_to(jnp.sum(both, axis=0, keepdims=True), (SUBLANES, LANES))


def _outproj(x, conv_w, yn, of, ob, proj, gnw, wo, g1, nw, sc, sh, wr, br, layer):
    bsz, t, d = x.shape
    tm = min(ROW_TILE, t)
    tok = lambda w: pl.BlockSpec((1, tm, w), lambda b, i: (b, i, 0))
    perb = pl.BlockSpec((1, 1, d), lambda b, i: (b, 0, 0))
    full = lambda shape: pl.BlockSpec(shape, lambda b, i: (0, 0))
    return pl.pallas_call(
        _outproj_kernel,
        grid=(bsz, t // tm),
        in_specs=[tok(d)] + _halo_specs(tm, t, 3 * CONV_W, COL_CONV // (3 * CONV_W)) + [
                  full((3, CONV_W)), tok(NA_W), tok(GDN_W), tok(GDN_W),
                  pl.BlockSpec((1, tm, GDN_W), lambda b, i: (b, i, COL_GZ // GDN_W)),
                  full((1, GDN_DV)), pl.BlockSpec((1, d, d), lambda b, i: (layer, 0, 0)), perb, full((1, d)), perb, perb,
                  pl.BlockSpec((2, d, LANES), lambda b, i: (0, 0, 0)), full((1, LANES))],
        out_specs=[tok(d), tok(d), tok(LANES), tok(LANES),
                   pl.BlockSpec((1, 1, SUBLANES, LANES), lambda b, i: (b, i, 0, 0))],
        out_shape=[jax.ShapeDtypeStruct((bsz, t, d), F32), jax.ShapeDtypeStruct((bsz, t, d), F32),
                   jax.ShapeDtypeStruct((bsz, t, LANES), jnp.int32), jax.ShapeDtypeStruct((bsz, t, LANES), F32),
                   jax.ShapeDtypeStruct((bsz, t // tm, SUBLANES, LANES), F32)],
        compiler_params=_cparams(("parallel", "parallel")),
        name="outproj",
    )(x, proj, proj, proj, conv_w, yn, of, ob, proj, gnw, wo, g1, nw, sc, sh, wr, br)


MOE_TM = 256


PIECE = SUBLANES
SORT_ROWS = TOP_K * ROW_TILE + N_EXPERTS * PIECE
MAX_PIECES = SORT_ROWS // PIECE


def _dispatch_kernel(lb_ref, nu_ref, np_ref, dst_ref, e_ref, a8_ref, h_ref, xs_out, zbuf, sbuf, sem, zsem):
    i = pl.program_id(0)
    nt = pl.num_programs(0)
    tm = h_ref.shape[0]
    bm = zbuf.shape[0]
    nblk = xs_out.shape[0] // bm
    slot = i % 2

    def zero_copy(blk):
        return pltpu.make_async_copy(zbuf, xs_out.at[pl.ds(pl.multiple_of(blk * bm, bm), bm), :], zsem)

    @pl.when(i == 0)
    def _():
        zbuf[...] = jnp.zeros_like(zbuf)
        for e in range(N_EXPERTS):
            @pl.when(lb_ref[e] >= 0)
            def _():
                zero_copy(lb_ref[e]).start()

        def start_trailing(j, carry):
            zero_copy(j).start()
            return carry

        def wait_one(j, carry):
            zero_copy(0).wait()
            return carry

        lax.fori_loop(nu_ref[0], nblk, start_trailing, 0)
        lax.fori_loop(nu_ref[0], nblk, wait_one, 0)
        for e in range(N_EXPERTS):
            @pl.when(lb_ref[e] >= 0)
            def _():
                zero_copy(0).wait()

    e = e_ref[...]
    lane = lax.broadcasted_iota(jnp.int32, e.shape, 1)
    a8 = a8_ref[0, 0:1, :]
    rows = []
    eye = lax.broadcasted_iota(jnp.int32, (tm, tm), 0) == lax.broadcasted_iota(jnp.int32, (tm, tm), 1)
    for k in range(TOP_K):
        col = (jnp.sum(jnp.where(lane == e[:, k:k + 1], a8, 0.0), axis=-1, keepdims=True)
               + e[:, TOP_K + k:TOP_K + k + 1].astype(F32))
        rows.append(jnp.sum(jnp.where(eye, col, 0.0), axis=0, keepdims=True))
    j = lax.broadcasted_iota(jnp.int32, (SORT_ROWS, tm), 0).astype(F32)
    perm = ((j == rows[0]) | (j == rows[1])).astype(BF16)
    hb = h_ref[...].astype(BF16)

    def start_pieces(table_ref, s, lo, hi):
        def body(g, carry):
            src = pl.multiple_of(g * PIECE, PIECE)
            dst = pl.multiple_of(table_ref[0, 0, g] * PIECE, PIECE)
            pltpu.make_async_copy(sbuf.at[s, pl.ds(src, PIECE), :], xs_out.at[pl.ds(dst, PIECE), :],
                                  sem.at[s]).start()
            return carry
        lax.fori_loop(lo, hi, body, 0)

    def wait_pieces(s, count):
        def body(g, carry):
            pltpu.make_async_copy(sbuf.at[s, pl.ds(0, PIECE), :], xs_out.at[pl.ds(0, PIECE), :], sem.at[s]).wait()
            return carry
        lax.fori_loop(0, count, body, 0)

    sbuf[slot] = _mm(perm, hb)
    start_pieces(dst_ref, slot, 0, np_ref[i])

    @pl.when(i > 0)
    def _():
        wait_pieces(1 - slot, np_ref[jnp.maximum(i - 1, 0)])

    @pl.when(i == nt - 1)
    def _():
        wait_pieces(slot, np_ref[i])


def _dispatch(hf, eidx, a8, dst8, npieces, last_blk, n_used, padded_rows):
    n, d = hf.shape
    tm = ROW_TILE
    nt = n // tm
    grid_spec = pltpu.PrefetchScalarGridSpec(
        num_scalar_prefetch=3,
        grid=(nt,),
        in_specs=[pl.BlockSpec((1, 1, MAX_PIECES), lambda i, lb, nu, npc: (i, 0, 0), memory_space=pltpu.SMEM),
                  pl.BlockSpec((tm, LANES), lambda i, lb, nu, npc: (i, 0)),
                  pl.BlockSpec((1, SUBLANES, LANES), lambda i, lb, nu, npc: (i, 0, 0)),
                  pl.BlockSpec((tm, d), lambda i, lb, nu, npc: (i, 0))],
        out_specs=pl.BlockSpec(memory_space=pl.ANY),
        scratch_shapes=[pltpu.VMEM((MOE_BM, d), F32), pltpu.VMEM((2, SORT_ROWS, d), F32),
                        pltpu.SemaphoreType.DMA((2,)), pltpu.SemaphoreType.DMA(())])
    return pl.pallas_call(
        _dispatch_kernel,
        grid_spec=grid_spec,
        out_shape=jax.ShapeDtypeStruct((padded_rows, d), F32),
        compiler_params=_cparams(("arbitrary",)),
        name="dispatch",
    )(last_blk, n_used, npieces, dst8, eidx, a8, hf)


def _experts_kernel(be_ref, nu_ref, x_ref, w1_ref, w3_ref, w2_ref, o_ref, w1b, w3b, w2b):
    i = pl.program_id(0)
    used = nu_ref[0]

    @pl.when((i == 0) | (be_ref[i] != be_ref[jnp.maximum(i - 1, 0)]))
    def _():
        w1b[...] = w1_ref[0, 0].astype(BF16)
        w3b[...] = w3_ref[0, 0].astype(BF16)
        w2b[...] = w2_ref[0, 0].astype(BF16)

    @pl.when(i < used)
    def _():
        xb = x_ref[...].astype(BF16)
        act = (_silu(_mm(xb, w1b[...])) * _mm(xb, w3b[...])).astype(BF16)
        o_ref[...] = _mm(act, w2b[...])

    @pl.when(i >= used)
    def _():
        o_ref[...] = jnp.zeros_like(o_ref)


def _experts(xs, block_e, n_used, w1, w3, w2, layer):
    padded_rows, d = xs.shape
    bm = MOE_BM
    nblk = padded_rows // bm
    de = w1.shape[-1]
    grid_spec = pltpu.PrefetchScalarGridSpec(
        num_scalar_prefetch=2,
        grid=(nblk,),
        in_specs=[pl.BlockSpec((bm, d), lambda i, be, nu: (jnp.maximum(jnp.minimum(i, nu[0] - 1), 0), 0)),
                  pl.BlockSpec((1, 1, d, de), lambda i, be, nu: (layer, be[i], 0, 0)),
                  pl.BlockSpec((1, 1, d, de), lambda i, be, nu: (layer, be[i], 0, 0)),
                  pl.BlockSpec((1, 1, de, d), lambda i, be, nu: (layer, be[i], 0, 0))],
        out_specs=pl.BlockSpec((bm, d), lambda i, be, nu: (i, 0)),
        scratch_shapes=[pltpu.VMEM((d, de), BF16), pltpu.VMEM((d, de), BF16), pltpu.VMEM((de, d), BF16)])
    return pl.pallas_call(
        _experts_kernel,
        grid_spec=grid_spec,
        out_shape=jax.ShapeDtypeStruct((padded_rows, d), F32),
        compiler_params=_cparams(("arbitrary",)),
        name="experts",
    )(block_e, n_used, xs, w1, w3, w2)


def _gather_rows(idx_ref, ys_hbm, ybuf, sem, slot, lo, hi):
    def body(r, carry):
        pltpu.make_async_copy(ys_hbm.at[pl.ds(idx_ref[0, 0, r], 1), :], ybuf.at[slot, pl.ds(r, 1), :],
                              sem.at[slot]).start()
        return carry
    lax.fori_loop(lo, hi, body, 0, unroll=8)


def _combine_rows(pc_ref, x_ref, gt_ref, g2_ref, ys_hbm, ybuf, sem):
    i = pl.program_id(0)
    tm = x_ref.shape[0]
    slot = i % 2

    @pl.when(i == 0)
    def _():
        _gather_rows(pc_ref, ys_hbm, ybuf, sem, 0, 0, TOP_K * tm)

    pltpu.make_async_copy(ys_hbm.at[pl.ds(0, TOP_K * tm), :], ybuf.at[slot], sem.at[slot]).wait()
    gt = gt_ref[...]
    y = gt[:, 0:1] * ybuf[slot, 0:tm, :] + gt[:, 1:2] * ybuf[slot, tm:2 * tm, :]
    return x_ref[...] + g2_ref[0] * y


def _combine_final_kernel(np_ref, dc_ref, dn_ref, e_ref, a8_ref, x_ref, gt_ref, g2_ref, fw_ref, ys_hbm, o_ref,
                          ybuf, sem):
    i = pl.program_id(0)
    nt = pl.num_programs(0)
    tm = x_ref.shape[0]
    slot = i % 2

    def start_pieces(table_ref, s, count):
        def body(g, carry):
            src = pl.multiple_of(table_ref[0, 0, g] * PIECE, PIECE)
            dst = pl.multiple_of(g * PIECE, PIECE)
            pltpu.make_async_copy(ys_hbm.at[pl.ds(src, PIECE), :], ybuf.at[s, pl.ds(dst, PIECE), :],
                                  sem.at[s]).start()
            return carry
        lax.fori_loop(0, count, body, 0)

    @pl.when(i == 0)
    def _():
        ybuf[...] = jnp.zeros_like(ybuf)
        start_pieces(dc_ref, 0, np_ref[0])

    @pl.when(i + 1 < nt)
    def _():
        start_pieces(dn_ref, 1 - slot, np_ref[jnp.minimum(i + 1, nt - 1)])

    def wait_piece(g, carry):
        pltpu.make_async_copy(ys_hbm.at[pl.ds(0, PIECE), :], ybuf.at[slot, pl.ds(0, PIECE), :], sem.at[slot]).wait()
        return carry

    lax.fori_loop(0, np_ref[i], wait_piece, 0)
    e = e_ref[...]
    lane = lax.broadcasted_iota(jnp.int32, e.shape, 1)
    a8 = a8_ref[0, 0:1, :]
    yb = ybuf[slot].astype(BF16)
    j = lax.broadcasted_iota(jnp.int32, (tm, SORT_ROWS), 1).astype(F32)
    gt = gt_ref[...]
    y = jnp.zeros((tm, x_ref.shape[1]), F32)
    for k in range(TOP_K):
        row = (jnp.sum(jnp.where(lane == e[:, k:k + 1], a8, 0.0), axis=-1, keepdims=True)
               + e[:, TOP_K + k:TOP_K + k + 1].astype(F32))
        y = y + gt[:, k:k + 1] * _mm((j == row).astype(BF16), yb)
    xn = x_ref[...] + g2_ref[0] * y
    ms = jnp.mean(xn * xn, axis=-1, keepdims=True)
    o_ref[...] = xn * lax.rsqrt(ms + EPS) * fw_ref[...]


PROJ_CHUNK = 512


def _combine_inproj_kernel(pc_ref, pn_ref, x_ref, gt_ref, g2_ref, nw_ref, sc_ref, sh_ref, w_ref, wabt_ref, ys_hbm,
                           xo_ref, o_ref, ot_ref, ybuf, sem):
    i = pl.program_id(0)
    tm = x_ref.shape[0]
    xn = _combine_rows(pc_ref, x_ref, gt_ref, g2_ref, ys_hbm, ybuf, sem)
    xo_ref[...] = xn
    h = _modnorm(xn, nw_ref[...], sc_ref[0], sh_ref[0]).astype(BF16)
    ot_ref[0] = _nt(wabt_ref[0], h)
    ncol = o_ref.shape[2]
    starts = list(range(0, ncol, PROJ_CHUNK))
    per = TOP_K * tm // len(starts)
    nxt = 1 - i % 2
    for j, c0 in enumerate(starts):
        for r in range(j * per, TOP_K * tm if j == len(starts) - 1 else (j + 1) * per):
            pltpu.make_async_copy(ys_hbm.at[pl.ds(pn_ref[0, 0, r], 1), :], ybuf.at[nxt, pl.ds(r, 1), :],
                                  sem.at[nxt]).start()
        c1 = min(c0 + PROJ_CHUNK, ncol)
        o_ref[0, :, c0:c1] = _mm(h, w_ref[0, :, c0:c1])

    @pl.when(i + 1 == pl.num_programs(0))
    def _():
        pltpu.make_async_copy(ys_hbm.at[pl.ds(0, TOP_K * tm), :], ybuf.at[nxt], sem.at[nxt]).wait()


def _combine_specs(n, d, t):
    tm = MOE_TM
    nt = n // tm
    per_b = t // tm
    specs = [pl.BlockSpec((1, 1, TOP_K * tm), lambda i: (i, 0, 0), memory_space=pltpu.SMEM),
             pl.BlockSpec((1, 1, TOP_K * tm), lambda i: (jnp.minimum(i + 1, nt - 1), 0, 0), memory_space=pltpu.SMEM),
             pl.BlockSpec((tm, d), lambda i: (i, 0)),
             pl.BlockSpec((tm, LANES), lambda i: (i, 0)),
             pl.BlockSpec((1, 1, d), lambda i: (i // per_b, 0, 0))]
    scratch = [pltpu.VMEM((2, TOP_K * tm, d), F32), pltpu.SemaphoreType.DMA((2,))]
    return tm, nt, per_b, specs, scratch


def _combine_final(x2, eidx, gates, a8, dst8, npieces, g2, fw, ys, t):
    n, d = x2.shape
    tm = ROW_TILE
    nt = n // tm
    per_b = t // tm
    grid_spec = pltpu.PrefetchScalarGridSpec(
        num_scalar_prefetch=1,
        grid=(nt,),
        in_specs=[pl.BlockSpec((1, 1, MAX_PIECES), lambda i, npc: (i, 0, 0), memory_space=pltpu.SMEM),
                  pl.BlockSpec((1, 1, MAX_PIECES), lambda i, npc: (jnp.minimum(i + 1, nt - 1), 0, 0),
                               memory_space=pltpu.SMEM),
                  pl.BlockSpec((tm, LANES), lambda i, npc: (i, 0)),
                  pl.BlockSpec((1, SUBLANES, LANES), lambda i, npc: (i, 0, 0)),
                  pl.BlockSpec((tm, d), lambda i, npc: (i, 0)),
                  pl.BlockSpec((tm, LANES), lambda i, npc: (i, 0)),
                  pl.BlockSpec((1, 1, d), lambda i, npc: (i // per_b, 0, 0)),
                  pl.BlockSpec((1, d), lambda i, npc: (0, 0)),
                  pl.BlockSpec(memory_space=pl.ANY)],
        out_specs=pl.BlockSpec((tm, d), lambda i, npc: (i, 0)),
        scratch_shapes=[pltpu.VMEM((2, SORT_ROWS, d), F32), pltpu.SemaphoreType.DMA((2,))])
    return pl.pallas_call(
        _combine_final_kernel,
        grid_spec=grid_spec,
        out_shape=jax.ShapeDtypeStruct((n, d), F32),
        compiler_params=_cparams(("arbitrary",)),
        name="combine",
    )(npieces, dst8, dst8, eidx, a8, x2, gates, g2, fw, ys)


def _combine_inproj(x2, gates, g2, ys, pos3, t, nw, sc, sh, w_pad, wabt, layer):
    n, d = x2.shape
    bsz = n // t
    nab = wabt.shape[1]
    tm, nt, per_b, specs, scratch = _combine_specs(n, d, t)
    perb = pl.BlockSpec((1, 1, d), lambda i: (i // per_b, 0, 0))
    return pl.pallas_call(
        _combine_inproj_kernel,
        grid=(nt,),
        in_specs=specs + [pl.BlockSpec((1, d), lambda i: (0, 0)), perb, perb,
                          pl.BlockSpec((1, d, D_IN_PAD), lambda i: (layer, 0, 0)),
                          pl.BlockSpec((1, nab, d), lambda i: (layer, 0, 0)),
                          pl.BlockSpec(memory_space=pl.ANY)],
        out_specs=[pl.BlockSpec((tm, d), lambda i: (i, 0)),
                   pl.BlockSpec((1, tm, D_IN_PAD), lambda i: (i // per_b, i % per_b, 0)),
                   pl.BlockSpec((1, nab, tm), lambda i: (i // per_b, 0, i % per_b))],
        out_shape=[jax.ShapeDtypeStruct((n, d), F32), jax.ShapeDtypeStruct((bsz, t, D_IN_PAD), F32),
                   jax.ShapeDtypeStruct((bsz, nab, t), F32)],
        scratch_shapes=scratch,
        compiler_params=_cparams(("arbitrary",)),
        name="combine_inproj",
    )(pos3, pos3, x2, gates, g2, nw, sc, sh, w_pad, wabt, ys)


def _moe_plan(eidx, hist):
    n = eidx.shape[0]
    bm = MOE_BM
    ntile = n // ROW_TILE
    experts = jnp.arange(N_EXPERTS, dtype=jnp.int32)
    hist = hist[:, 0, :N_EXPERTS].astype(jnp.int32)
    cnt = (hist + PIECE - 1) // PIECE * PIECE
    seg_start = jnp.cumsum(cnt, axis=1) - cnt
    pieces = cnt // PIECE
    piece_end = jnp.cumsum(pieces, axis=1)
    npieces = piece_end[:, -1].astype(jnp.int32)
    sizes = jnp.sum(cnt, axis=0)
    base = jnp.cumsum(cnt, axis=0) - cnt
    padded = (sizes + bm - 1) // bm * bm
    pad_end = jnp.cumsum(padded)
    tbl = (pad_end - padded)[None, :] + base
    e = eidx[:, :TOP_K].reshape(ntile, ROW_TILE, TOP_K)
    onehot = e[..., None] == experts
    pos = jnp.sum(jnp.where(onehot, tbl[:, None, None, :], 0), axis=-1).reshape(n, TOP_K) + eidx[:, TOP_K:2 * TOP_K]
    g = jnp.arange(MAX_PIECES, dtype=jnp.int32)
    e_of_g = jnp.minimum(jnp.sum(piece_end[:, None, :] <= g[None, :, None], axis=-1), N_EXPERTS - 1)
    sel = e_of_g[..., None] == experts
    first = jnp.sum(jnp.where(sel, (piece_end - pieces)[:, None, :], 0), axis=-1)
    dst8 = jnp.sum(jnp.where(sel, tbl[:, None, :], 0), axis=-1) // PIECE + (g[None, :] - first)
    dst8 = jnp.where(g[None, :] < npieces[:, None], dst8, 0).astype(jnp.int32).reshape(ntile, 1, MAX_PIECES)
    seg8 = jnp.broadcast_to(jnp.pad(seg_start.astype(F32), ((0, 0), (0, LANES - N_EXPERTS)))[:, None, :],
                            (ntile, SUBLANES, LANES))
    padded_rows = (n * TOP_K + ntile * N_EXPERTS * (PIECE - 1) + N_EXPERTS * (bm - 1) + bm - 1) // bm * bm
    n_blocks = padded_rows // bm
    blk_start = jnp.arange(n_blocks, dtype=jnp.int32) * bm
    block_e = jnp.minimum(jnp.sum(pad_end[None, :] <= blk_start[:, None], axis=1), N_EXPERTS - 1).astype(jnp.int32)
    n_used = (pad_end[-1] // bm).astype(jnp.int32).reshape(1)
    last_blk = jnp.where(padded > 0, pad_end // bm - 1, -1).astype(jnp.int32)
    nt = n // MOE_TM
    pos3 = jnp.transpose(pos.astype(jnp.int32).reshape(nt, MOE_TM, TOP_K), (0, 2, 1)).reshape(nt, 1, TOP_K * MOE_TM)
    return pos3, seg8, dst8, npieces, block_e, n_used, last_blk, padded_rows


def kernel(x, c, norm_mix_w, norm_ffn_w, w_ada, b_ada, w_in, conv_a_w, na_rpb, gdn_conv_w, gdn_a_log, gdn_dt_bias,
           gdn_norm_w, w_out, router_group_w, router_group_b, router_expert_w, router_expert_b, expert_w1,
           expert_w3, expert_w2, final_norm_w):
    bsz, t, d = x.shape
    depth = w_ada.shape[0]
    n = bsz * t
    assert t % ROW_TILE == 0 and (t // GRID_W) % NA_KH == 0, "sequence length must be a multiple of 512"
    mod = _ada(c, w_ada, b_ada)
    w_in_pad = jnp.pad(w_in.astype(BF16), ((0, 0), (0, 0), (0, D_IN_PAD - D_IN)))
    wabt = jnp.transpose(w_in[:, :, COL_AB:], (0, 2, 1)).astype(BF16)
    w_out_b = w_out.astype(BF16)
    wr = jnp.pad(jnp.concatenate([router_group_w, router_expert_w], axis=-1),
                 ((0, 0), (0, 0), (0, LANES - N_GROUPS - N_EXPERTS)))
    wr_hi = wr.astype(BF16)
    wr = jnp.stack([wr_hi, (wr - wr_hi.astype(F32)).astype(BF16)], axis=1)
    br = jnp.pad(jnp.concatenate([router_group_b, router_expert_b], axis=-1),
                 ((0, 0), (0, LANES - N_GROUPS - N_EXPERTS)))
    fw = final_norm_w.reshape(1, d)
    na_bias = _na_bias_table(na_rpb)
    mods = [[mod[l, :, j * d:(j + 1) * d].reshape(bsz, 1, d) for j in range(6)] for l in range(depth)]
    proj, abt = _inproj(x, norm_mix_w[0].reshape(1, d), mods[0][1], mods[0][0], w_in_pad, wabt, 0)
    for l in range(depth):
        sh1, sc1, g1, sh2, sc2, g2 = mods[l]
        y_na = _na(proj, na_bias, l)
        o_f, o_b = _gdnscan(*_gdnchunk(proj, abt, gdn_conv_w[l], gdn_a_log[l], gdn_dt_bias[l]))
        x, hf, eidx, gates, hist = _outproj(x, conv_a_w[l], y_na, o_f, o_b, proj, gdn_norm_w[l].reshape(1, GDN_DV),
                                            w_out_b, g1, norm_ffn_w[l].reshape(1, d), sc2, sh2, wr[l],
                                            br[l].reshape(1, LANES), l)
        eidx = eidx.reshape(n, LANES)
        pos3, seg8, dst8, npieces, block_e, n_used, last_blk, padded_rows = _moe_plan(
            eidx, hist.reshape(-1, SUBLANES, LANES))
        xs = _dispatch(hf.reshape(n, d), eidx, seg8, dst8, npieces, last_blk, n_used, padded_rows)
        ys = _experts(xs, block_e, n_used, expert_w1, expert_w3, expert_w2, l)
        x2, gt = x.reshape(n, d), gates.reshape(n, LANES)
        if l + 1 < depth:
            x2, proj, abt = _combine_inproj(x2, gt, g2, ys, pos3, t, norm_mix_w[l + 1].reshape(1, d), mods[l + 1][1],
                                            mods[l + 1][0], w_in_pad, wabt, l + 1)
            x = x2.reshape(bsz, t, d)
        else:
            x = _combine_final(x2, eidx, gt, seg8, dst8, npieces, g2, fw, ys, t).reshape(bsz, t, d)
    return x
```

```python
import jax
import jax.numpy as jnp
from jax import lax
from jax.experimental import pallas as pl
from jax.experimental.pallas import tpu as pltpu

F32 = jnp.float32
BF16 = jnp.bfloat16

EPS = 1e-6
GRID_W = 64
CONV_W = 256
NA_HEADS = 4
NA_DH = 64
NA_W = NA_HEADS * NA_DH
NA_KH = 8
NA_KW = 16
GDN_HEADS = 4
GDN_DK = 128
GDN_DV = 128
GDN_W = GDN_HEADS * GDN_DV
GDN_QKV = 2 * GDN_HEADS * GDN_DK + GDN_W
GDN_CHUNK = 64
N_GROUPS = 4
EXPERTS_PER_GROUP = 8
N_EXPERTS = N_GROUPS * EXPERTS_PER_GROUP
TOP_K = 2

COL_CONV = 0
COL_NA = 3 * CONV_W
COL_GQKV = COL_NA + 3 * NA_W
COL_GZ = COL_GQKV + GDN_QKV
COL_AB = COL_GZ + GDN_W
D_IN = COL_AB + 4 * GDN_HEADS
D_IN_PAD = COL_AB + 128
LANES = 128
SUBLANES = 8

NEG = -1e30
VMEM_LIMIT = 56 * 1024 * 1024

MOE_BM = 512
ROW_TILE = 512


def _cparams(sem):
    return pltpu.CompilerParams(dimension_semantics=sem, vmem_limit_bytes=VMEM_LIMIT)


def _silu(x):
    return x * (1.0 / (1.0 + jnp.exp(-x)))


def _nt(a, b):
    return lax.dot_general(a, b, (((1,), (1,)), ((), ())), preferred_element_type=F32)


def _tn(a, b):
    return lax.dot_general(a, b, (((0,), (0,)), ((), ())), preferred_element_type=F32)


def _mm(a, b):
    return jnp.dot(a, b, preferred_element_type=F32)


def _ada_kernel(ct_ref, w_ref, b_ref, o_ref):
    w = w_ref[0]
    nb = ct_ref.shape[1]
    for r in range(nb):
        col = _silu(ct_ref[:, r:r + 1])
        o_ref[0, r:r + 1, :] = jnp.sum(w * col, axis=0, keepdims=True) + b_ref[0]


def _ada(c, w_ada, b_ada):
    depth, d, n6 = w_ada.shape
    bsz = c.shape[0]
    tn = 512
    return pl.pallas_call(
        _ada_kernel,
        grid=(depth, n6 // tn),
        in_specs=[pl.BlockSpec((d, bsz), lambda l, j: (0, 0)),
                  pl.BlockSpec((1, d, tn), lambda l, j: (l, 0, j)),
                  pl.BlockSpec((1, 1, tn), lambda l, j: (l, 0, j))],
        out_specs=pl.BlockSpec((1, bsz, tn), lambda l, j: (l, 0, j)),
        out_shape=jax.ShapeDtypeStruct((depth, bsz, n6), F32),
        compiler_params=_cparams(("parallel", "parallel")),
        name="ada",
    )(c.T, w_ada, b_ada.reshape(depth, 1, n6))


def _modnorm(x, nw, sc, sh):
    ms = jnp.mean(x * x, axis=-1, keepdims=True)
    y = x * lax.rsqrt(ms + EPS)
    return (y * nw) * (1.0 + sc) + sh


def _project(h, w_ref, wabt_ref, o_ref, ot_ref):
    o_ref[0] = _mm(h, w_ref[0])
    ot_ref[0] = _nt(wabt_ref[0], h)


def _inproj_kernel(x_ref, nw_ref, sc_ref, sh_ref, w_ref, wabt_ref, o_ref, ot_ref):
    h = _modnorm(x_ref[0], nw_ref[...], sc_ref[0], sh_ref[0]).astype(BF16)
    _project(h, w_ref, wabt_ref, o_ref, ot_ref)


def _inproj(x, nw, sc, sh, w_pad, wabt, layer):
    bsz, t, d = x.shape
    tm = min(ROW_TILE, t)
    nab = wabt.shape[1]
    return pl.pallas_call(
        _inproj_kernel,
        grid=(bsz, t // tm),
        in_specs=[pl.BlockSpec((1, tm, d), lambda b, i: (b, i, 0)),
                  pl.BlockSpec((1, d), lambda b, i: (0, 0)),
                  pl.BlockSpec((1, 1, d), lambda b, i: (b, 0, 0)),
                  pl.BlockSpec((1, 1, d), lambda b, i: (b, 0, 0)),
                  pl.BlockSpec((1, d, D_IN_PAD), lambda b, i: (layer, 0, 0)),
                  pl.BlockSpec((1, nab, d), lambda b, i: (layer, 0, 0))],
        out_specs=[pl.BlockSpec((1, tm, D_IN_PAD), lambda b, i: (b, i, 0)),
                   pl.BlockSpec((1, nab, tm), lambda b, i: (b, 0, i))],
        out_shape=[jax.ShapeDtypeStruct((bsz, t, D_IN_PAD), F32),
                   jax.ShapeDtypeStruct((bsz, nab, t), F32)],
        compiler_params=_cparams(("parallel", "parallel")),
        name="inproj",
    )(x, nw, sc, sh, w_pad, wabt)


def _dwconv3(u, prev_row, next_row, w_ref):
    tt = u.shape[0]
    row = lax.broadcasted_iota(jnp.int32, u.shape, 0)
    dn = jnp.where(row == 0, prev_row, pltpu.roll(u, 1, axis=0))
    up = jnp.where(row == tt - 1, next_row, pltpu.roll(u, tt - 1, axis=0))
    return w_ref[0:1, :] * dn + w_ref[1:2, :] * u + w_ref[2:3, :] * up


def _halo_specs(tt, t, width, colblk):
    nsub = tt // SUBLANES
    last = t // SUBLANES - 1
    return [pl.BlockSpec((1, tt, width), lambda b, i: (b, i, colblk)),
            pl.BlockSpec((1, SUBLANES, width), lambda b, i: (b, jnp.maximum(i * nsub - 1, 0), colblk)),
            pl.BlockSpec((1, SUBLANES, width), lambda b, i: (b, jnp.minimum((i + 1) * nsub, last), colblk))]


def _convmix_tile(m_ref, p_ref, n_ref, w_ref):
    i = pl.program_id(1)
    nt = pl.num_programs(1)
    m = m_ref[0]
    u = m[:, CONV_W:2 * CONV_W] * m[:, 2 * CONV_W:]
    p = p_ref[0]
    n = n_ref[0]
    pu = p[SUBLANES - 1:SUBLANES, CONV_W:2 * CONV_W] * p[SUBLANES - 1:SUBLANES, 2 * CONV_W:]
    nu = n[0:1, CONV_W:2 * CONV_W] * n[0:1, 2 * CONV_W:]
    pu = jnp.where(i == 0, 0.0, pu)
    nu = jnp.where(i == nt - 1, 0.0, nu)
    return m[:, :CONV_W] * _dwconv3(u, pu, nu, w_ref)


NA_RB = NA_KH // 2
NA_TOK = NA_RB * GRID_W
NA_KEYS = 3 * NA_TOK


def _na_bias_table(rpb):
    col = jnp.arange(GRID_W)
    cstart = jnp.clip(col - NA_KW // 2, 0, GRID_W - NA_KW)
    kc = jnp.arange(GRID_W)
    valid = (kc[None, :] >= cstart[:, None]) & (kc[None, :] < cstart[:, None] + NA_KW)
    dc = kc[None, :] - col[:, None] + (NA_KW - 1)
    onehot = (dc[None] == jnp.arange(2 * NA_KW - 1)[:, None, None]) & valid[None]
    cols = jnp.einsum('lhrd,dck->lhrck', rpb, onehot.astype(F32), precision=lax.Precision.HIGHEST)
    cols = jnp.where(valid, cols, NEG)
    lo = NA_KH - 1 - NA_RB
    blk = jnp.stack([cols[:, :, lo - j:lo - j + 3 * NA_RB] for j in range(NA_RB)], axis=2)
    blk = jnp.transpose(blk, (0, 1, 2, 4, 3, 5))
    return blk.reshape(rpb.shape[0], NA_HEADS, NA_TOK, NA_KEYS)


def _na_kernel(q_ref, kp_ref, kc_ref, kn_ref, vp_ref, vc_ref, vn_ref, bias_ref, o_ref, *, rows):
    i = pl.program_id(1)
    kbuf = jnp.concatenate([kp_ref[0], kc_ref[0], kn_ref[0]], axis=0).astype(BF16)
    vbuf = jnp.concatenate([vp_ref[0], vc_ref[0], vn_ref[0]], axis=0).astype(BF16)
    q = q_ref[0] * (NA_DH ** -0.5)
    head_of_lane = lax.broadcasted_iota(jnp.int32, (1, NA_W), 1) // NA_DH
    qrow = i * NA_RB + lax.broadcasted_iota(jnp.int32, (NA_TOK, NA_KEYS), 0) // GRID_W
    krow = (i - 1) * NA_RB + lax.broadcasted_iota(jnp.int32, (NA_TOK, NA_KEYS), 1) // GRID_W
    rs = jnp.clip(qrow - NA_KH // 2, 0, rows - NA_KH)
    row_mask = jnp.where((krow >= rs) & (krow < rs + NA_KH), 0.0, NEG)
    acc = jnp.zeros((NA_TOK, NA_W), F32)
    for h in range(NA_HEADS):
        mine = head_of_lane == h
        s = _nt(jnp.where(mine, q, 0.0).astype(BF16), kbuf) + (bias_ref[0, h] + row_mask)
        m = jnp.max(s, axis=-1, keepdims=True)
        p = jnp.exp(s - m)
        l = jnp.sum(p, axis=-1, keepdims=True)
        o = _mm((p / l).astype(BF16), vbuf)
        acc = acc + jnp.where(mine, o, 0.0)
    o_ref[0] = acc


def _na(proj, bias_tbl, layer):
    bsz, t, _ = proj.shape
    rows = t // GRID_W
    nblk = rows // NA_RB
    qc, kc, vc = COL_NA // NA_W, COL_NA // NA_W + 1, COL_NA // NA_W + 2

    def spec(col, shift):
        return pl.BlockSpec((1, NA_TOK, NA_W), lambda b, i: (b, jnp.clip(i + shift, 0, nblk - 1), col))

    def na_kernel(*refs):
        _na_kernel(*refs, rows=rows)

    return pl.pallas_call(
        na_kernel,
        grid=(bsz, nblk),
        in_specs=[spec(qc, 0), spec(kc, -1), spec(kc, 0), spec(kc, 1), spec(vc, -1), spec(vc, 0), spec(vc, 1),
                  pl.BlockSpec((1, NA_HEADS, NA_TOK, NA_KEYS), lambda b, i: (layer, 0, 0, 0))],
        out_specs=pl.BlockSpec((1, NA_TOK, NA_W), lambda b, i: (b, i, 0)),
        out_shape=jax.ShapeDtypeStruct((bsz, t, NA_W), F32),
        compiler_params=_cparams(("parallel", "parallel")),
        name="na",
    )(proj, proj, proj, proj, proj, proj, proj, bias_tbl)


def _gdn_qkv(m_ref, p_ref, n_ref, w_ref):
    i = pl.program_id(1)
    nt = pl.num_programs(1)
    pu = jnp.where(i == 0, 0.0, p_ref[0, SUBLANES - 1:SUBLANES, :])
    nu = jnp.where(i == nt - 1, 0.0, n_ref[0, 0:1, :])
    c = _silu(_dwconv3(m_ref[0], pu, nu, w_ref))
    heads = []
    for hh in range(GDN_QKV // GDN_DK):
        xh = c[:, hh * GDN_DK:(hh + 1) * GDN_DK]
        if hh < 2 * GDN_HEADS:
            xh = xh * lax.rsqrt(jnp.sum(xh * xh, axis=-1, keepdims=True) + EPS)
            if hh < GDN_HEADS:
                xh = xh * (GDN_DK ** -0.5)
        heads.append(xh)
    return heads[:GDN_HEADS], heads[GDN_HEADS:2 * GDN_HEADS], heads[2 * GDN_HEADS:]


GDN_GS = 128
GDN_UNITS = 2 * GDN_HEADS
GDN_GC = GDN_GS // GDN_CHUNK
GDN_NG = 2
GDN_TS = GDN_NG * GDN_GS


def _softplus(x):
    return jnp.maximum(x, 0.0) + jnp.log1p(jnp.exp(-jnp.abs(x)))


def _seg_cumsum(x, axis, reverse):
    n = x.shape[axis]
    pos = lax.broadcasted_iota(jnp.int32, x.shape, axis) & (GDN_CHUNK - 1)
    s = 1
    while s < GDN_CHUNK:
        if reverse:
            x = x + jnp.where(pos < GDN_CHUNK - s, pltpu.roll(x, n - s, axis=axis), 0.0)
        else:
            x = x + jnp.where(pos >= s, pltpu.roll(x, s, axis=axis), 0.0)
        s *= 2
    return x


def _gdnchunk_kernel(m_ref, p_ref, n_ref, cw_ref, ab_ref, abt_ref, alr_ref, dtr_ref, alc_ref, dtc_ref,
                     u_ref, wq_ref, kd_ref, in_ref, gl_ref):
    gs, c, nh = GDN_GS, GDN_CHUNK, GDN_HEADS
    ri = lax.broadcasted_iota(jnp.int32, (gs, gs), 0)
    ci = lax.broadcasted_iota(jnp.int32, (gs, gs), 1)
    same = (ri // c) == (ci // c)
    eye = (ri == ci).astype(F32)
    rowc = lax.broadcasted_iota(jnp.int32, (gs, 1), 0) // c
    ab = ab_ref[0]
    abt = abt_ref[0]
    graw_c = -jnp.exp(alr_ref[...]) * _softplus(ab + dtr_ref[...])
    graw_r = -jnp.exp(alc_ref[...]) * _softplus(abt[0:2 * nh] + dtc_ref[...])
    beta_c = 1.0 / (1.0 + jnp.exp(-ab))
    g_col = [_seg_cumsum(graw_c, 0, False), _seg_cumsum(graw_c, 0, True)]
    g_row = [_seg_cumsum(graw_r, 1, False), _seg_cumsum(graw_r, 1, True)]
    incl = [same & (ri >= ci), same & (ri <= ci)]
    strict = [same & (ri > ci), same & (ri < ci)]

    qs, ks, vs = _gdn_qkv(m_ref, p_ref, n_ref, cw_ref)
    groups = [slice(g * gs, (g + 1) * gs) for g in range(GDN_NG)]
    grams = [[_nt(jnp.concatenate([q[rg], k[rg]], axis=0).astype(BF16), k[rg].astype(BF16))
              for q, k in zip(qs, ks)] for rg in groups]

    units = [(g, d, h) for g in range(GDN_NG) for d in range(2) for h in range(nh)]
    gcs, bcs, intras, xs, ps = [], [], [], [], []
    for g, d, h in units:
        col = d * nh + h
        rg = groups[g]
        gc = g_col[d][rg, col:col + 1]
        gr = g_row[d][col:col + 1, rg]
        bc = beta_c[rg, 2 * nh + col:2 * nh + col + 1]
        e_incl = jnp.exp(jnp.where(incl[d], gc - gr, NEG))
        a = grams[g][h][gs:] * bc * jnp.where(strict[d], e_incl, 0.0)
        gcs.append(gc), bcs.append(bc)
        intras.append(grams[g][h][:gs] * e_incl)
        xs.append(eye - a), ps.append(a)
    for _ in range(5):
        pbs = [p.astype(BF16) for p in ps]
        ps = [_mm(pb, pb) for pb in pbs]
        xs = [x + _mm(x.astype(BF16), p.astype(BF16)) for x, p in zip(xs, ps)]
    egs = [jnp.exp(gc) for gc in gcs]
    sols = [_mm(x.astype(BF16),
                jnp.concatenate([vs[h][groups[g]] * bc, ks[h][groups[g]] * (bc * eg)], axis=1).astype(BF16))
            for (g, d, h), x, bc, eg in zip(units, xs, bcs, egs)]
    for (g, d, h), gc, eg, sol, intra in zip(units, gcs, egs, sols, intras):
        col = d * nh + h
        rg = groups[g]
        glast_col = jnp.zeros_like(gc)
        for n in range(GDN_GC):
            r = n * c if d == 1 else (n + 1) * c - 1
            glast = gc[r:r + 1, :]
            glast_col = jnp.where(rowc == n, glast, glast_col)
            gl_ref[0, g, col * GDN_GC + n:col * GDN_GC + n + 1, :] = jnp.broadcast_to(jnp.exp(glast), (1, LANES))
        u_ref[0, col, rg, :] = sol[:, :GDN_DV]
        w = sol[:, GDN_DV:].astype(BF16)
        qd = (qs[h][rg] * eg).astype(BF16)
        for n in range(GDN_GC):
            base = 2 * (g * gs + n * c)
            wq_ref[0, col, base:base + c, :] = w[n * c:(n + 1) * c]
            wq_ref[0, col, base + c:base + 2 * c, :] = qd[n * c:(n + 1) * c]
        kd_ref[0, col, rg, :] = (ks[h][rg] * jnp.exp(glast_col - gc)).astype(BF16)
        in_ref[0, col, rg, :] = jnp.concatenate([intra[n * c:(n + 1) * c, n * c:(n + 1) * c] for n in range(GDN_GC)],
                                                axis=0).astype(BF16)


def _gdnchunk(proj, abt, conv_w, a_log, dt_bias):
    bsz, t, _ = proj.shape
    ts, nu = GDN_TS, GDN_UNITS
    nab = abt.shape[1]
    pad = lambda r: jnp.pad(r.reshape(1, -1), ((0, 0), (0, LANES - r.size)))
    alr, dtr = pad(a_log), pad(dt_bias)
    alc, dtc = a_log.reshape(-1, 1), dt_bias.reshape(-1, 1)
    small = lambda shape: pl.BlockSpec(shape, lambda b, i: (0, 0))
    unit = lambda rows, w: pl.BlockSpec((1, nu, rows, w), lambda b, i: (b, 0, i, 0))
    return pl.pallas_call(
        _gdnchunk_kernel,
        grid=(bsz, t // ts),
        in_specs=_halo_specs(ts, t, GDN_QKV, COL_GQKV // GDN_QKV) + [
            small((3, GDN_QKV)),
            pl.BlockSpec((1, ts, LANES), lambda b, i: (b, i, COL_AB // LANES)),
            pl.BlockSpec((1, nab, ts), lambda b, i: (b, 0, i)),
            small((1, LANES)), small((1, LANES)), small((nu, 1)), small((nu, 1))],
        out_specs=[unit(ts, GDN_DV), unit(2 * ts, GDN_DK), unit(ts, GDN_DK), unit(ts, GDN_CHUNK),
                   pl.BlockSpec((1, GDN_NG, nu * GDN_GC, LANES), lambda b, i: (b, i, 0, 0))],
        out_shape=[jax.ShapeDtypeStruct((bsz, nu, t, GDN_DV), F32),
                   jax.ShapeDtypeStruct((bsz, nu, 2 * t, GDN_DK), BF16),
                   jax.ShapeDtypeStruct((bsz, nu, t, GDN_DK), BF16),
                   jax.ShapeDtypeStruct((bsz, nu, t, GDN_CHUNK), BF16),
                   jax.ShapeDtypeStruct((bsz, t // GDN_GS, nu * GDN_GC, LANES), F32)],
        compiler_params=_cparams(("parallel", "parallel")),
        name="gdnchunk",
    )(proj, proj, proj, conv_w, proj, abt, alr, dtr, alc, dtc)


GDN_SB = 4
GDN_ST = GDN_SB * GDN_CHUNK


def _gdnscan_kernel(uf, ub, wqf, wqb, kdf, kdb, inf, inb, glf, glb, of_ref, ob_ref, s_ref):
    i = pl.program_id(0)

    @pl.when(i == 0)
    def _():
        s_ref[...] = jnp.zeros_like(s_ref)

    bsz = uf.shape[0]
    c, nh = GDN_CHUNK, GDN_HEADS
    chains = [(b, d, h) for b in range(bsz) for d in range(2) for h in range(nh)]
    for step in range(GDN_SB):
        zs, vns = [], []
        for b, d, h in chains:
            cc = step if d == 0 else GDN_SB - 1 - step
            wq = (wqf, wqb)[d]
            st = s_ref[(b * 2 + d) * nh + h]
            zs.append(_mm(wq[b, h, 2 * cc * c:(2 * cc + 2) * c, :], st.astype(BF16)))
        for (b, d, h), z in zip(chains, zs):
            cc = step if d == 0 else GDN_SB - 1 - step
            u = (uf, ub)[d]
            vns.append((u[b, h, cc * c:(cc + 1) * c, :] - z[:c]).astype(BF16))
        for (b, d, h), z, vn in zip(chains, zs, vns):
            cc = step if d == 0 else GDN_SB - 1 - step
            rows = slice(cc * c, (cc + 1) * c)
            intra = (inf, inb)[d]
            o_ref = (of_ref, ob_ref)[d]
            o_ref[b, rows, h * GDN_DV:(h + 1) * GDN_DV] = z[c:] + _mm(intra[b, h, rows, :], vn)
        for (b, d, h), vn in zip(chains, vns):
            cc = step if d == 0 else GDN_SB - 1 - step
            rows = slice(cc * c, (cc + 1) * c)
            kd = (kdf, kdb)[d]
            gl = (glf, glb)[d]
            r = (d * nh + h) * GDN_GC + cc % GDN_GC
            sidx = (b * 2 + d) * nh + h
            s_ref[sidx] = s_ref[sidx] * gl[b, cc // GDN_GC, r:r + 1, :] + _tn(kd[b, h, rows, :], vn)


def _gdnscan(u, wq, kd, intra, gl):
    bsz, nu, t, _ = u.shape
    nh = GDN_HEADS
    st = GDN_ST
    nb = t // st
    ngs = st // GDN_GS

    def unit(rows, w, d):
        if d == 0:
            return pl.BlockSpec((bsz, nh, rows, w), lambda i: (0, 0, i, 0))
        return pl.BlockSpec((bsz, nh, rows, w), lambda i: (0, 1, nb - 1 - i, 0))

    glspec = lambda d: pl.BlockSpec((bsz, ngs, nu * GDN_GC, LANES),
                                    (lambda i: (0, i, 0, 0)) if d == 0 else (lambda i: (0, nb - 1 - i, 0, 0)))
    return pl.pallas_call(
        _gdnscan_kernel,
        grid=(nb,),
        in_specs=[unit(st, GDN_DV, 0), unit(st, GDN_DV, 1), unit(2 * st, GDN_DK, 0), unit(2 * st, GDN_DK, 1),
                  unit(st, GDN_DK, 0), unit(st, GDN_DK, 1), unit(st, GDN_CHUNK, 0), unit(st, GDN_CHUNK, 1),
                  glspec(0), glspec(1)],
        out_specs=[pl.BlockSpec((bsz, st, GDN_W), lambda i: (0, i, 0)),
                   pl.BlockSpec((bsz, st, GDN_W), lambda i: (0, nb - 1 - i, 0))],
        out_shape=[jax.ShapeDtypeStruct((bsz, t, GDN_W), F32), jax.ShapeDtypeStruct((bsz, t, GDN_W), F32)],
        scratch_shapes=[pltpu.VMEM((bsz * nu, GDN_DK, GDN_DV), F32)],
        compiler_params=_cparams(("arbitrary",)),
        name="gdnscan",
    )(u, u, wq, wq, kd, kd, intra, intra, gl, gl)


OUT_SPLIT = 1


def _outproj_kernel(x_ref, cm_ref, cp_ref, cn_ref, cw_ref, yn_ref, of_ref, ob_ref, z_ref, gnw_ref, wo_ref, g1_ref,
                    nw_ref, sc_ref, sh_ref, wr_ref, br_ref, xo_ref, h_ref, e_ref, g_ref, hist_ref):
    yc = _convmix_tile(cm_ref, cp_ref, cn_ref, cw_ref)
    tm = x_ref.shape[1]
    lane = lax.broadcasted_iota(jnp.int32, (tm // OUT_SPLIT, LANES), 1)
    picks = []
    for r in range(OUT_SPLIT):
        rows = slice(r * (tm // OUT_SPLIT), (r + 1) * (tm // OUT_SPLIT))
        o = of_ref[0, rows, :] + ob_ref[0, rows, :]
        z = z_ref[0, rows, :]
        parts = [yc[rows], yn_ref[0, rows, :]]
        for h in range(GDN_HEADS):
            sl = slice(h * GDN_DV, (h + 1) * GDN_DV)
            oh = o[:, sl]
            oh = oh * lax.rsqrt(jnp.mean(oh * oh, axis=-1, keepdims=True) + EPS) * gnw_ref[...]
            parts.append(oh * _silu(z[:, sl]))
        mixed = _mm(jnp.concatenate(parts, axis=-1).astype(BF16), wo_ref[0])
        xn = x_ref[0, rows, :] + g1_ref[0] * mixed
        xo_ref[0, rows, :] = xn
        hf = _modnorm(xn, nw_ref[...], sc_ref[0], sh_ref[0])
        h_ref[0, rows, :] = hf
        hf_hi = hf.astype(BF16)
        hf_lo = (hf - hf_hi.astype(F32)).astype(BF16)
        logits = (_mm(hf_hi, wr_ref[0]) + (_mm(hf_hi, wr_ref[1]) + _mm(hf_lo, wr_ref[0]))) + br_ref[...]
        gl = jnp.where(lane < N_GROUPS, logits, NEG)
        gm = jnp.max(gl, axis=-1, keepdims=True)
        den = jnp.sum(jnp.exp(gl - gm), axis=-1, keepdims=True)
        grp = jnp.min(jnp.where(gl == gm, lane, LANES), axis=-1, keepdims=True)
        pg_top = 1.0 / den
        ex = lane - N_GROUPS
        in_grp = (ex >= grp * EXPERTS_PER_GROUP) & (ex < (grp + 1) * EXPERTS_PER_GROUP)
        el = jnp.where(in_grp, logits, NEG)
        m1 = jnp.max(el, axis=-1, keepdims=True)
        i1 = jnp.min(jnp.where(el == m1, lane, LANES), axis=-1, keepdims=True)
        el2 = jnp.where(lane == i1, NEG, el)
        m2 = jnp.max(el2, axis=-1, keepdims=True)
        i2 = jnp.min(jnp.where(el2 == m2, lane, LANES), axis=-1, keepdims=True)
        e2 = jnp.exp(m2 - m1)
        w1 = pg_top / (1.0 + e2)
        w2 = pg_top * e2 / (1.0 + e2)
        g_ref[0, rows, :] = jnp.where(lane == 0, w1, jnp.where(lane == 1, w2, 0.0))
        picks.append((i1 - N_GROUPS, i2 - N_GROUPS))
    e1 = jnp.concatenate([p[0] for p in picks], axis=0)
    e2i = jnp.concatenate([p[1] for p in picks], axis=0)
    lane = lax.broadcasted_iota(jnp.int32, (tm, LANES), 1)
    oh1 = (lane == e1).astype(F32)
    oh2 = (lane == e2i).astype(F32)
    both = oh1 + oh2
    earlier = (lax.broadcasted_iota(jnp.int32, (tm, tm), 0) > lax.broadcasted_iota(jnp.int32, (tm, tm), 1))
    cnt = _mm(earlier.astype(BF16), both.astype(BF16))
    r1 = jnp.sum(cnt * oh1, axis=-1, keepdims=True).astype(jnp.int32)
    r2 = jnp.sum(cnt * oh2, axis=-1, keepdims=True).astype(jnp.int32)
    e_ref[0] = jnp.where(lane == 0, e1, jnp.where(lane == 1, e2i, jnp.where(lane == 2, r1, jnp.where(lane == 3, r2, 0))))
    hist_ref[0, 0] = jnp.broadcast_to(jnp.sum(both, axis=0, keepdims=True), (SUBLANES, LANES))


def _outproj(x, conv_w, yn, of, ob, proj, gnw, wo, g1, nw, sc, sh, wr, br, layer):
    bsz, t, d = x.shape
    tm = min(ROW_TILE, t)
    tok = lambda w: pl.BlockSpec((1, tm, w), lambda b, i: (b, i, 0))
    perb = pl.BlockSpec((1, 1, d), lambda b, i: (b, 0, 0))
    full = lambda shape: pl.BlockSpec(shape, lambda b, i: (0, 0))
    return pl.pallas_call(
        _outproj_kernel,
        grid=(bsz, t // tm),
        in_specs=[tok(d)] + _halo_specs(tm, t, 3 * CONV_W, COL_CONV // (3 * CONV_W)) + [
                  full((3, CONV_W)), tok(NA_W), tok(GDN_W), tok(GDN_W),
                  pl.BlockSpec((1, tm, GDN_W), lambda b, i: (b, i, COL_GZ // GDN_W)),
                  full((1, GDN_DV)), pl.BlockSpec((1, d, d), lambda b, i: (layer, 0, 0)), perb, full((1, d)), perb, perb,
                  pl.BlockSpec((2, d, LANES), lambda b, i: (0, 0, 0)), full((1, LANES))],
        out_specs=[tok(d), tok(d), tok(LANES), tok(LANES),
                   pl.BlockSpec((1, 1, SUBLANES, LANES), lambda b, i: (b, i, 0, 0))],
        out_shape=[jax.ShapeDtypeStruct((bsz, t, d), F32), jax.ShapeDtypeStruct((bsz, t, d), F32),
                   jax.ShapeDtypeStruct((bsz, t, LANES), jnp.int32), jax.ShapeDtypeStruct((bsz, t, LANES), F32),
                   jax.ShapeDtypeStruct((bsz, t // tm, SUBLANES, LANES), F32)],
        compiler_params=_cparams(("parallel", "parallel")),
        name="outproj",
    )(x, proj, proj, proj, conv_w, yn, of, ob, proj, gnw, wo, g1, nw, sc, sh, wr, br)


MOE_TM = 256


PIECE = SUBLANES
SORT_ROWS = TOP_K * ROW_TILE + N_EXPERTS * PIECE
MAX_PIECES = SORT_ROWS // PIECE


def _dispatch_kernel(lb_ref, nu_ref, np_ref, dst_ref, e_ref, a8_ref, h_ref, xs_out, zbuf, sbuf, sem, zsem):
    i = pl.program_id(0)
    nt = pl.num_programs(0)
    tm = h_ref.shape[0]
    bm = zbuf.shape[0]
    nblk = xs_out.shape[0] // bm
    slot = i % 2

    def zero_copy(blk):
        return pltpu.make_async_copy(zbuf, xs_out.at[pl.ds(pl.multiple_of(blk * bm, bm), bm), :], zsem)

    @pl.when(i == 0)
    def _():
        zbuf[...] = jnp.zeros_like(zbuf)
        for e in range(N_EXPERTS):
            @pl.when(lb_ref[e] >= 0)
            def _():
                zero_copy(lb_ref[e]).start()

        def start_trailing(j, carry):
            zero_copy(j).start()
            return carry

        def wait_one(j, carry):
            zero_copy(0).wait()
            return carry

        lax.fori_loop(nu_ref[0], nblk, start_trailing, 0)
        lax.fori_loop(nu_ref[0], nblk, wait_one, 0)
        for e in range(N_EXPERTS):
            @pl.when(lb_ref[e] >= 0)
            def _():
                zero_copy(0).wait()

    e = e_ref[...]
    lane = lax.broadcasted_iota(jnp.int32, e.shape, 1)
    a8 = a8_ref[0, 0:1, :]
    rows = []
    eye = lax.broadcasted_iota(jnp.int32, (tm, tm), 0) == lax.broadcasted_iota(jnp.int32, (tm, tm), 1)
    for k in range(TOP_K):
        col = (jnp.sum(jnp.where(lane == e[:, k:k + 1], a8, 0.0), axis=-1, keepdims=True)
               + e[:, TOP_K + k:TOP_K + k + 1].astype(F32))
        rows.append(jnp.sum(jnp.where(eye, col, 0.0), axis=0, keepdims=True))
    j = lax.broadcasted_iota(jnp.int32, (SORT_ROWS, tm), 0).astype(F32)
    perm = ((j == rows[0]) | (j == rows[1])).astype(BF16)
    hb = h_ref[...].astype(BF16)

    def start_pieces(table_ref, s, lo, hi):
        def body(g, carry):
            src = pl.multiple_of(g * PIECE, PIECE)
            dst = pl.multiple_of(table_ref[0, 0, g] * PIECE, PIECE)
            pltpu.make_async_copy(sbuf.at[s, pl.ds(src, PIECE), :], xs_out.at[pl.ds(dst, PIECE), :],
                                  sem.at[s]).start()
            return carry
        lax.fori_loop(lo, hi, body, 0)

    def wait_pieces(s, count):
        def body(g, carry):
            pltpu.make_async_copy(sbuf.at[s, pl.ds(0, PIECE), :], xs_out.at[pl.ds(0, PIECE), :], sem.at[s]).wait()
            return carry
        lax.fori_loop(0, count, body, 0)

    sbuf[slot] = _mm(perm, hb)
    start_pieces(dst_ref, slot, 0, np_ref[i])

    @pl.when(i > 0)
    def _():
        wait_pieces(1 - slot, np_ref[jnp.maximum(i - 1, 0)])

    @pl.when(i == nt - 1)
    def _():
        wait_pieces(slot, np_ref[i])


def _dispatch(hf, eidx, a8, dst8, npieces, last_blk, n_used, padded_rows):
    n, d = hf.shape
    tm = ROW_TILE
    nt = n // tm
    grid_spec = pltpu.PrefetchScalarGridSpec(
        num_scalar_prefetch=3,
        grid=(nt,),
        in_specs=[pl.BlockSpec((1, 1, MAX_PIECES), lambda i, lb, nu, npc: (i, 0, 0), memory_space=pltpu.SMEM),
                  pl.BlockSpec((tm, LANES), lambda i, lb, nu, npc: (i, 0)),
                  pl.BlockSpec((1, SUBLANES, LANES), lambda i, lb, nu, npc: (i, 0, 0)),
                  pl.BlockSpec((tm, d), lambda i, lb, nu, npc: (i, 0))],
        out_specs=pl.BlockSpec(memory_space=pl.ANY),
        scratch_shapes=[pltpu.VMEM((MOE_BM, d), F32), pltpu.VMEM((2, SORT_ROWS, d), F32),
                        pltpu.SemaphoreType.DMA((2,)), pltpu.SemaphoreType.DMA(())])
    return pl.pallas_call(
        _dispatch_kernel,
        grid_spec=grid_spec,
        out_shape=jax.ShapeDtypeStruct((padded_rows, d), F32),
        compiler_params=_cparams(("arbitrary",)),
        name="dispatch",
    )(last_blk, n_used, npieces, dst8, eidx, a8, hf)


def _experts_kernel(be_ref, nu_ref, nx_ref, sg_ref, x_ref, w1_hbm, w3_hbm, w2_hbm, o_ref,
                    w1f, w3f, w2f, w1b, w3b, w2b, sem, *, layer):
    i = pl.program_id(0)
    used = nu_ref[0]

    def fetch(e, s):
        return [pltpu.make_async_copy(w_hbm.at[layer, e], wf.at[s], sem.at[s])
                for w_hbm, wf in ((w1_hbm, w1f), (w3_hbm, w3f), (w2_hbm, w2f))]

    first = ((i == 0) | (be_ref[i] != be_ref[jnp.maximum(i - 1, 0)])) & (i < used)

    @pl.when(first)
    def _():
        s = sg_ref[i] % 2

        @pl.when(i == 0)
        def _():
            for cp in fetch(be_ref[0], 0):
                cp.start()

        for cp in fetch(be_ref[i], s):
            cp.wait()
        w1b[...] = w1f[s].astype(BF16)
        w3b[...] = w3f[s].astype(BF16)
        w2b[...] = w2f[s].astype(BF16)

        @pl.when(nx_ref[i] >= 0)
        def _():
            for cp in fetch(nx_ref[i], 1 - s):
                cp.start()

    @pl.when(i < used)
    def _():
        xb = x_ref[...].astype(BF16)
        act = (_silu(_mm(xb, w1b[...])) * _mm(xb, w3b[...])).astype(BF16)
        o_ref[...] = _mm(act, w2b[...])

    @pl.when(i >= used)
    def _():
        o_ref[...] = jnp.zeros_like(o_ref)


def _experts(xs, block_e, n_used, next_e, seg_id, w1, w3, w2, layer):
    padded_rows, d = xs.shape
    bm = MOE_BM
    nblk = padded_rows // bm
    de = w1.shape[-1]
    grid_spec = pltpu.PrefetchScalarGridSpec(
        num_scalar_prefetch=4,
        grid=(nblk,),
        in_specs=[pl.BlockSpec((bm, d), lambda i, be, nu, nx, sg: (jnp.maximum(jnp.minimum(i, nu[0] - 1), 0), 0)),
                  pl.BlockSpec(memory_space=pl.ANY), pl.BlockSpec(memory_space=pl.ANY),
                  pl.BlockSpec(memory_space=pl.ANY)],
        out_specs=pl.BlockSpec((bm, d), lambda i, be, nu, nx, sg: (i, 0)),
        scratch_shapes=[pltpu.VMEM((2, d, de), F32), pltpu.VMEM((2, d, de), F32), pltpu.VMEM((2, de, d), F32),
                        pltpu.VMEM((d, de), BF16), pltpu.VMEM((d, de), BF16), pltpu.VMEM((de, d), BF16),
                        pltpu.SemaphoreType.DMA((2,))])

    def experts_kernel(*refs):
        _experts_kernel(*refs, layer=layer)

    return pl.pallas_call(
        experts_kernel,
        grid_spec=grid_spec,
        out_shape=jax.ShapeDtypeStruct((padded_rows, d), F32),
        compiler_params=_cparams(("arbitrary",)),
        name="experts",
    )(block_e, n_used, next_e, seg_id, xs, w1, w3, w2)


def _gather_rows(idx_ref, ys_hbm, ybuf, sem, slot, lo, hi):
    def body(r, carry):
        pltpu.make_async_copy(ys_hbm.at[pl.ds(idx_ref[0, 0, r], 1), :], ybuf.at[slot, pl.ds(r, 1), :],
                              sem.at[slot]).start()
        return carry
    lax.fori_loop(lo, hi, body, 0, unroll=8)


def _combine_rows(pc_ref, x_ref, gt_ref, g2_ref, ys_hbm, ybuf, sem):
    i = pl.program_id(0)
    tm = x_ref.shape[0]
    slot = i % 2

    @pl.when(i == 0)
    def _():
        _gather_rows(pc_ref, ys_hbm, ybuf, sem, 0, 0, TOP_K * tm)

    pltpu.make_async_copy(ys_hbm.at[pl.ds(0, TOP_K * tm), :], ybuf.at[slot], sem.at[slot]).wait()
    gt = gt_ref[...]
    y = gt[:, 0:1] * ybuf[slot, 0:tm, :] + gt[:, 1:2] * ybuf[slot, tm:2 * tm, :]
    return x_ref[...] + g2_ref[0] * y


def _combine_final_kernel(np_ref, dc_ref, dn_ref, e_ref, a8_ref, x_ref, gt_ref, g2_ref, fw_ref, ys_hbm, o_ref,
                          ybuf, sem):
    i = pl.program_id(0)
    nt = pl.num_programs(0)
    tm = x_ref.shape[0]
    slot = i % 2

    def start_pieces(table_ref, s, count):
        def body(g, carry):
            src = pl.multiple_of(table_ref[0, 0, g] * PIECE, PIECE)
            dst = pl.multiple_of(g * PIECE, PIECE)
            pltpu.make_async_copy(ys_hbm.at[pl.ds(src, PIECE), :], ybuf.at[s, pl.ds(dst, PIECE), :],
                                  sem.at[s]).start()
            return carry
        lax.fori_loop(0, count, body, 0)

    @pl.when(i == 0)
    def _():
        ybuf[...] = jnp.zeros_like(ybuf)
        start_pieces(dc_ref, 0, np_ref[0])

    @pl.when(i + 1 < nt)
    def _():
        start_pieces(dn_ref, 1 - slot, np_ref[jnp.minimum(i + 1, nt - 1)])

    def wait_piece(g, carry):
        pltpu.make_async_copy(ys_hbm.at[pl.ds(0, PIECE), :], ybuf.at[slot, pl.ds(0, PIECE), :], sem.at[slot]).wait()
        return carry

    lax.fori_loop(0, np_ref[i], wait_piece, 0)
    e = e_ref[...]
    lane = lax.broadcasted_iota(jnp.int32, e.shape, 1)
    a8 = a8_ref[0, 0:1, :]
    yb = ybuf[slot].astype(BF16)
    j = lax.broadcasted_iota(jnp.int32, (tm, SORT_ROWS), 1).astype(F32)
    gt = gt_ref[...]
    y = jnp.zeros((tm, x_ref.shape[1]), F32)
    for k in range(TOP_K):
        row = (jnp.sum(jnp.where(lane == e[:, k:k + 1], a8, 0.0), axis=-1, keepdims=True)
               + e[:, TOP_K + k:TOP_K + k + 1].astype(F32))
        y = y + gt[:, k:k + 1] * _mm((j == row).astype(BF16), yb)
    xn = x_ref[...] + g2_ref[0] * y
    ms = jnp.mean(xn * xn, axis=-1, keepdims=True)
    o_ref[...] = xn * lax.rsqrt(ms + EPS) * fw_ref[...]


PROJ_CHUNK = 512


def _combine_inproj_kernel(pc_ref, pn_ref, x_ref, gt_ref, g2_ref, nw_ref, sc_ref, sh_ref, w_ref, wabt_ref, ys_hbm,
                           xo_ref, o_ref, ot_ref, ybuf, sem):
    i = pl.program_id(0)
    tm = x_ref.shape[0]
    xn = _combine_rows(pc_ref, x_ref, gt_ref, g2_ref, ys_hbm, ybuf, sem)
    xo_ref[...] = xn
    h = _modnorm(xn, nw_ref[...], sc_ref[0], sh_ref[0]).astype(BF16)
    ot_ref[0] = _nt(wabt_ref[0], h)
    ncol = o_ref.shape[2]
    starts = list(range(0, ncol, PROJ_CHUNK))
    per = TOP_K * tm // len(starts)
    nxt = 1 - i % 2
    for j, c0 in enumerate(starts):
        for r in range(j * per, TOP_K * tm if j == len(starts) - 1 else (j + 1) * per):
            pltpu.make_async_copy(ys_hbm.at[pl.ds(pn_ref[0, 0, r], 1), :], ybuf.at[nxt, pl.ds(r, 1), :],
                                  sem.at[nxt]).start()
        c1 = min(c0 + PROJ_CHUNK, ncol)
        o_ref[0, :, c0:c1] = _mm(h, w_ref[0, :, c0:c1])

    @pl.when(i + 1 == pl.num_programs(0))
    def _():
        pltpu.make_async_copy(ys_hbm.at[pl.ds(0, TOP_K * tm), :], ybuf.at[nxt], sem.at[nxt]).wait()


def _combine_specs(n, d, t):
    tm = MOE_TM
    nt = n // tm
    per_b = t // tm
    specs = [pl.BlockSpec((1, 1, TOP_K * tm), lambda i: (i, 0, 0), memory_space=pltpu.SMEM),
             pl.BlockSpec((1, 1, TOP_K * tm), lambda i: (jnp.minimum(i + 1, nt - 1), 0, 0), memory_space=pltpu.SMEM),
             pl.BlockSpec((tm, d), lambda i: (i, 0)),
             pl.BlockSpec((tm, LANES), lambda i: (i, 0)),
             pl.BlockSpec((1, 1, d), lambda i: (i // per_b, 0, 0))]
    scratch = [pltpu.VMEM((2, TOP_K * tm, d), F32), pltpu.SemaphoreType.DMA((2,))]
    return tm, nt, per_b, specs, scratch


def _combine_final(x2, eidx, gates, a8, dst8, npieces, g2, fw, ys, t):
    n, d = x2.shape
    tm = ROW_TILE
    nt = n // tm
    per_b = t // tm
    grid_spec = pltpu.PrefetchScalarGridSpec(
        num_scalar_prefetch=1,
        grid=(nt,),
        in_specs=[pl.BlockSpec((1, 1, MAX_PIECES), lambda i, npc: (i, 0, 0), memory_space=pltpu.SMEM),
                  pl.BlockSpec((1, 1, MAX_PIECES), lambda i, npc: (jnp.minimum(i + 1, nt - 1), 0, 0),
                               memory_space=pltpu.SMEM),
                  pl.BlockSpec((tm, LANES), lambda i, npc: (i, 0)),
                  pl.BlockSpec((1, SUBLANES, LANES), lambda i, npc: (i, 0, 0)),
                  pl.BlockSpec((tm, d), lambda i, npc: (i, 0)),
                  pl.BlockSpec((tm, LANES), lambda i, npc: (i, 0)),
                  pl.BlockSpec((1, 1, d), lambda i, npc: (i // per_b, 0, 0)),
                  pl.BlockSpec((1, d), lambda i, npc: (0, 0)),
                  pl.BlockSpec(memory_space=pl.ANY)],
        out_specs=pl.BlockSpec((tm, d), lambda i, npc: (i, 0)),
        scratch_shapes=[pltpu.VMEM((2, SORT_ROWS, d), F32), pltpu.SemaphoreType.DMA((2,))])
    return pl.pallas_call(
        _combine_final_kernel,
        grid_spec=grid_spec,
        out_shape=jax.ShapeDtypeStruct((n, d), F32),
        compiler_params=_cparams(("arbitrary",)),
        name="combine",
    )(npieces, dst8, dst8, eidx, a8, x2, gates, g2, fw, ys)


def _combine_inproj(x2, gates, g2, ys, pos3, t, nw, sc, sh, w_pad, wabt, layer):
    n, d = x2.shape
    bsz = n // t
    nab = wabt.shape[1]
    tm, nt, per_b, specs, scratch = _combine_specs(n, d, t)
    perb = pl.BlockSpec((1, 1, d), lambda i: (i // per_b, 0, 0))
    return pl.pallas_call(
        _combine_inproj_kernel,
        grid=(nt,),
        in_specs=specs + [pl.BlockSpec((1, d), lambda i: (0, 0)), perb, perb,
                          pl.BlockSpec((1, d, D_IN_PAD), lambda i: (layer, 0, 0)),
                          pl.BlockSpec((1, nab, d), lambda i: (layer, 0, 0)),
                          pl.BlockSpec(memory_space=pl.ANY)],
        out_specs=[pl.BlockSpec((tm, d), lambda i: (i, 0)),
                   pl.BlockSpec((1, tm, D_IN_PAD), lambda i: (i // per_b, i % per_b, 0)),
                   pl.BlockSpec((1, nab, tm), lambda i: (i // per_b, 0, i % per_b))],
        out_shape=[jax.ShapeDtypeStruct((n, d), F32), jax.ShapeDtypeStruct((bsz, t, D_IN_PAD), F32),
                   jax.ShapeDtypeStruct((bsz, nab, t), F32)],
        scratch_shapes=scratch,
        compiler_params=_cparams(("arbitrary",)),
        name="combine_inproj",
    )(pos3, pos3, x2, gates, g2, nw, sc, sh, w_pad, wabt, ys)


def _moe_plan(eidx, hist):
    n = eidx.shape[0]
    bm = MOE_BM
    ntile = n // ROW_TILE
    experts = jnp.arange(N_EXPERTS, dtype=jnp.int32)
    hist = hist[:, 0, :N_EXPERTS].astype(jnp.int32)
    cnt = (hist + PIECE - 1) // PIECE * PIECE
    seg_start = jnp.cumsum(cnt, axis=1) - cnt
    pieces = cnt // PIECE
    piece_end = jnp.cumsum(pieces, axis=1)
    npieces = piece_end[:, -1].astype(jnp.int32)
    sizes = jnp.sum(cnt, axis=0)
    base = jnp.cumsum(cnt, axis=0) - cnt
    padded = (sizes + bm - 1) // bm * bm
    pad_end = jnp.cumsum(padded)
    tbl = (pad_end - padded)[None, :] + base
    e = eidx[:, :TOP_K].reshape(ntile, ROW_TILE, TOP_K)
    onehot = e[..., None] == experts
    pos = jnp.sum(jnp.where(onehot, tbl[:, None, None, :], 0), axis=-1).reshape(n, TOP_K) + eidx[:, TOP_K:2 * TOP_K]
    g = jnp.arange(MAX_PIECES, dtype=jnp.int32)
    e_of_g = jnp.minimum(jnp.sum(piece_end[:, None, :] <= g[None, :, None], axis=-1), N_EXPERTS - 1)
    sel = e_of_g[..., None] == experts
    first = jnp.sum(jnp.where(sel, (piece_end - pieces)[:, None, :], 0), axis=-1)
    dst8 = jnp.sum(jnp.where(sel, tbl[:, None, :], 0), axis=-1) // PIECE + (g[None, :] - first)
    dst8 = jnp.where(g[None, :] < npieces[:, None], dst8, 0).astype(jnp.int32).reshape(ntile, 1, MAX_PIECES)
    seg8 = jnp.broadcast_to(jnp.pad(seg_start.astype(F32), ((0, 0), (0, LANES - N_EXPERTS)))[:, None, :],
                            (ntile, SUBLANES, LANES))
    padded_rows = (n * TOP_K + ntile * N_EXPERTS * (PIECE - 1) + N_EXPERTS * (bm - 1) + bm - 1) // bm * bm
    n_blocks = padded_rows // bm
    blk_start = jnp.arange(n_blocks, dtype=jnp.int32) * bm
    block_e = jnp.minimum(jnp.sum(pad_end[None, :] <= blk_start[:, None], axis=1), N_EXPERTS - 1).astype(jnp.int32)
    n_used = (pad_end[-1] // bm).astype(jnp.int32).reshape(1)
    last_blk = jnp.where(padded > 0, pad_end // bm - 1, -1).astype(jnp.int32)
    blk = jnp.arange(n_blocks, dtype=jnp.int32)
    change = (blk == 0) | (block_e != jnp.roll(block_e, 1))
    seg_id = (jnp.cumsum(change.astype(jnp.int32)) - 1).astype(jnp.int32)
    later = (blk[None, :] > blk[:, None]) & (block_e[None, :] != block_e[:, None]) & (blk[None, :] < n_used[0])
    nxt = jnp.min(jnp.where(later, blk[None, :], n_blocks), axis=1)
    next_e = jnp.where(nxt < n_blocks, block_e[jnp.minimum(nxt, n_blocks - 1)], -1).astype(jnp.int32)
    nt = n // MOE_TM
    pos3 = jnp.transpose(pos.astype(jnp.int32).reshape(nt, MOE_TM, TOP_K), (0, 2, 1)).reshape(nt, 1, TOP_K * MOE_TM)
    return pos3, seg8, dst8, npieces, block_e, n_used, next_e, seg_id, last_blk, padded_rows


def kernel(x, c, norm_mix_w, norm_ffn_w, w_ada, b_ada, w_in, conv_a_w, na_rpb, gdn_conv_w, gdn_a_log, gdn_dt_bias,
           gdn_norm_w, w_out, router_group_w, router_group_b, router_expert_w, router_expert_b, expert_w1,
           expert_w3, expert_w2, final_norm_w):
    bsz, t, d = x.shape
    depth = w_ada.shape[0]
    n = bsz * t
    assert t % ROW_TILE == 0 and (t // GRID_W) % NA_KH == 0, "sequence length must be a multiple of 512"
    mod = _ada(c, w_ada, b_ada)
    w_in_pad = jnp.pad(w_in.astype(BF16), ((0, 0), (0, 0), (0, D_IN_PAD - D_IN)))
    wabt = jnp.transpose(w_in[:, :, COL_AB:], (0, 2, 1)).astype(BF16)
    w_out_b = w_out.astype(BF16)
    wr = jnp.pad(jnp.concatenate([router_group_w, router_expert_w], axis=-1),
                 ((0, 0), (0, 0), (0, LANES - N_GROUPS - N_EXPERTS)))
    wr_hi = wr.astype(BF16)
    wr = jnp.stack([wr_hi, (wr - wr_hi.astype(F32)).astype(BF16)], axis=1)
    br = jnp.pad(jnp.concatenate([router_group_b, router_expert_b], axis=-1),
                 ((0, 0), (0, LANES - N_GROUPS - N_EXPERTS)))
    fw = final_norm_w.reshape(1, d)
    na_bias = _na_bias_table(na_rpb)
    mods = [[mod[l, :, j * d:(j + 1) * d].reshape(bsz, 1, d) for j in range(6)] for l in range(depth)]
    proj, abt = _inproj(x, norm_mix_w[0].reshape(1, d), mods[0][1], mods[0][0], w_in_pad, wabt, 0)
    for l in range(depth):
        sh1, sc1, g1, sh2, sc2, g2 = mods[l]
        y_na = _na(proj, na_bias, l)
        o_f, o_b = _gdnscan(*_gdnchunk(proj, abt, gdn_conv_w[l], gdn_a_log[l], gdn_dt_bias[l]))
        x, hf, eidx, gates, hist = _outproj(x, conv_a_w[l], y_na, o_f, o_b, proj, gdn_norm_w[l].reshape(1, GDN_DV),
                                            w_out_b, g1, norm_ffn_w[l].reshape(1, d), sc2, sh2, wr[l],
                                            br[l].reshape(1, LANES), l)
        eidx = eidx.reshape(n, LANES)
        pos3, seg8, dst8, npieces, block_e, n_used, next_e, seg_id, last_blk, padded_rows = _moe_plan(
            eidx, hist.reshape(-1, SUBLANES, LANES))
        xs = _dispatch(hf.reshape(n, d), eidx, seg8, dst8, npieces, last_blk, n_used, padded_rows)
        ys = _experts(xs, block_e, n_used, next_e, seg_id, expert_w1, expert_w3, expert_w2, l)
        x2, gt = x.reshape(n, d), gates.reshape(n, LANES)
        if l + 1 < depth:
            x2, proj, abt = _combine_inproj(x2, gt, g2, ys, pos3, t, norm_mix_w[l + 1].reshape(1, d), mods[l + 1][1],
                                            mods[l + 1][0], w_in_pad, wabt, l + 1)
            x = x2.reshape(bsz, t, d)
        else:
            x = _combine_final(x2, eidx, gt, seg8, dst8, npieces, g2, fw, ys, t).reshape(bsz, t, d)
    return x
```

```python
import jax
import jax.numpy as jnp
from jax import lax
from jax.experimental import pallas as pl
from jax.experimental.pallas import tpu as pltpu

F32 = jnp.float32
BF16 = jnp.bfloat16

EPS = 1e-6
GRID_W = 64
CONV_W = 256
NA_HEADS = 4
NA_DH = 64
NA_W = NA_HEADS * NA_DH
NA_KH = 8
NA_KW = 16
GDN_HEADS = 4
GDN_DK = 128
GDN_DV = 128
GDN_W = GDN_HEADS * GDN_DV
GDN_QKV = 2 * GDN_HEADS * GDN_DK + GDN_W
GDN_CHUNK = 64
N_GROUPS = 4
EXPERTS_PER_GROUP = 8
N_EXPERTS = N_GROUPS * EXPERTS_PER_GROUP
TOP_K = 2

COL_CONV = 0
COL_NA = 3 * CONV_W
COL_GQKV = COL_NA + 3 * NA_W
COL_GZ = COL_GQKV + GDN_QKV
COL_AB = COL_GZ + GDN_W
D_IN = COL_AB + 4 * GDN_HEADS
D_IN_PAD = COL_AB + 128
LANES = 128
SUBLANES = 8

NEG = -1e30
VMEM_LIMIT = 56 * 1024 * 1024

MOE_BM = 512
ROW_TILE = 512


def _cparams(sem):
    return pltpu.CompilerParams(dimension_semantics=sem, vmem_limit_bytes=VMEM_LIMIT)


def _silu(x):
    return x * (1.0 / (1.0 + jnp.exp(-x)))


def _nt(a, b):
    return lax.dot_general(a, b, (((1,), (1,)), ((), ())), preferred_element_type=F32)


def _tn(a, b):
    return lax.dot_general(a, b, (((0,), (0,)), ((), ())), preferred_element_type=F32)


def _mm(a, b):
    return jnp.dot(a, b, preferred_element_type=F32)


def _ada_kernel(ct_ref, w_ref, b_ref, o_ref):
    w = w_ref[0]
    nb = ct_ref.shape[1]
    for r in range(nb):
        col = _silu(ct_ref[:, r:r + 1])
        o_ref[0, r:r + 1, :] = jnp.sum(w * col, axis=0, keepdims=True) + b_ref[0]


def _ada(c, w_ada, b_ada):
    depth, d, n6 = w_ada.shape
    bsz = c.shape[0]
    tn = 1536
    return pl.pallas_call(
        _ada_kernel,
        grid=(depth, n6 // tn),
        in_specs=[pl.BlockSpec((d, bsz), lambda l, j: (0, 0)),
                  pl.BlockSpec((1, d, tn), lambda l, j: (l, 0, j)),
                  pl.BlockSpec((1, 1, tn), lambda l, j: (l, 0, j))],
        out_specs=pl.BlockSpec((1, bsz, tn), lambda l, j: (l, 0, j)),
        out_shape=jax.ShapeDtypeStruct((depth, bsz, n6), F32),
        compiler_params=_cparams(("parallel", "parallel")),
        name="ada",
    )(c.T, w_ada, b_ada.reshape(depth, 1, n6))


def _modnorm(x, nw, sc, sh):
    ms = jnp.mean(x * x, axis=-1, keepdims=True)
    y = x * lax.rsqrt(ms + EPS)
    return (y * nw) * (1.0 + sc) + sh


def _project(h, w_ref, wabt_ref, o_ref, ot_ref):
    o_ref[0] = _mm(h, w_ref[0])
    ot_ref[0] = _nt(wabt_ref[0], h)


def _inproj_kernel(x_ref, nw_ref, sc_ref, sh_ref, w_ref, wabt_ref, o_ref, ot_ref):
    h = _modnorm(x_ref[0], nw_ref[...], sc_ref[0], sh_ref[0]).astype(BF16)
    _project(h, w_ref, wabt_ref, o_ref, ot_ref)


def _inproj(x, nw, sc, sh, w_pad, wabt, layer):
    bsz, t, d = x.shape
    tm = min(ROW_TILE, t)
    nab = wabt.shape[1]
    return pl.pallas_call(
        _inproj_kernel,
        grid=(bsz, t // tm),
        in_specs=[pl.BlockSpec((1, tm, d), lambda b, i: (b, i, 0)),
                  pl.BlockSpec((1, d), lambda b, i: (0, 0)),
                  pl.BlockSpec((1, 1, d), lambda b, i: (b, 0, 0)),
                  pl.BlockSpec((1, 1, d), lambda b, i: (b, 0, 0)),
                  pl.BlockSpec((1, d, D_IN_PAD), lambda b, i: (layer, 0, 0)),
                  pl.BlockSpec((1, nab, d), lambda b, i: (layer, 0, 0))],
        out_specs=[pl.BlockSpec((1, tm, D_IN_PAD), lambda b, i: (b, i, 0)),
                   pl.BlockSpec((1, nab, tm), lambda b, i: (b, 0, i))],
        out_shape=[jax.ShapeDtypeStruct((bsz, t, D_IN_PAD), F32),
                   jax.ShapeDtypeStruct((bsz, nab, t), F32)],
        compiler_params=_cparams(("parallel", "parallel")),
        name="inproj",
    )(x, nw, sc, sh, w_pad, wabt)


def _dwconv3(u, prev_row, next_row, w_ref):
    tt = u.shape[0]
    row = lax.broadcasted_iota(jnp.int32, u.shape, 0)
    dn = jnp.where(row == 0, prev_row, pltpu.roll(u, 1, axis=0))
    up = jnp.where(row == tt - 1, next_row, pltpu.roll(u, tt - 1, axis=0))
    return w_ref[0:1, :] * dn + w_ref[1:2, :] * u + w_ref[2:3, :] * up


def _halo_specs(tt, t, width, colblk):
    nsub = tt // SUBLANES
    last = t // SUBLANES - 1
    return [pl.BlockSpec((1, tt, width), lambda b, i: (b, i, colblk)),
            pl.BlockSpec((1, SUBLANES, width), lambda b, i: (b, jnp.maximum(i * nsub - 1, 0), colblk)),
            pl.BlockSpec((1, SUBLANES, width), lambda b, i: (b, jnp.minimum((i + 1) * nsub, last), colblk))]


def _convmix_tile(m_ref, p_ref, n_ref, w_ref):
    i = pl.program_id(1)
    nt = pl.num_programs(1)
    m = m_ref[0]
    u = m[:, CONV_W:2 * CONV_W] * m[:, 2 * CONV_W:]
    p = p_ref[0]
    n = n_ref[0]
    pu = p[SUBLANES - 1:SUBLANES, CONV_W:2 * CONV_W] * p[SUBLANES - 1:SUBLANES, 2 * CONV_W:]
    nu = n[0:1, CONV_W:2 * CONV_W] * n[0:1, 2 * CONV_W:]
    pu = jnp.where(i == 0, 0.0, pu)
    nu = jnp.where(i == nt - 1, 0.0, nu)
    return m[:, :CONV_W] * _dwconv3(u, pu, nu, w_ref)


NA_RB = NA_KH // 2
NA_TOK = NA_RB * GRID_W
NA_KEYS = 3 * NA_TOK


def _na_bias_table(rpb):
    col = jnp.arange(GRID_W)
    cstart = jnp.clip(col - NA_KW // 2, 0, GRID_W - NA_KW)
    kc = jnp.arange(GRID_W)
    valid = (kc[None, :] >= cstart[:, None]) & (kc[None, :] < cstart[:, None] + NA_KW)
    dc = kc[None, :] - col[:, None] + (NA_KW - 1)
    onehot = (dc[None] == jnp.arange(2 * NA_KW - 1)[:, None, None]) & valid[None]
    cols = jnp.einsum('lhrd,dck->lhrck', rpb, onehot.astype(F32), precision=lax.Precision.HIGHEST)
    cols = jnp.where(valid, cols, NEG)
    lo = NA_KH - 1 - NA_RB
    blk = jnp.stack([cols[:, :, lo - j:lo - j + 3 * NA_RB] for j in range(NA_RB)], axis=2)
    blk = jnp.transpose(blk, (0, 1, 2, 4, 3, 5))
    return blk.reshape(rpb.shape[0], NA_HEADS, NA_TOK, NA_KEYS)


def _na_kernel(q_ref, kp_ref, kc_ref, kn_ref, vp_ref, vc_ref, vn_ref, bias_ref, o_ref, *, rows):
    i = pl.program_id(1)
    kbuf = jnp.concatenate([kp_ref[0], kc_ref[0], kn_ref[0]], axis=0).astype(BF16)
    vbuf = jnp.concatenate([vp_ref[0], vc_ref[0], vn_ref[0]], axis=0).astype(BF16)
    q = q_ref[0] * (NA_DH ** -0.5)
    head_of_lane = lax.broadcasted_iota(jnp.int32, (1, NA_W), 1) // NA_DH
    qrow = i * NA_RB + lax.broadcasted_iota(jnp.int32, (NA_TOK, NA_KEYS), 0) // GRID_W
    krow = (i - 1) * NA_RB + lax.broadcasted_iota(jnp.int32, (NA_TOK, NA_KEYS), 1) // GRID_W
    rs = jnp.clip(qrow - NA_KH // 2, 0, rows - NA_KH)
    row_mask = jnp.where((krow >= rs) & (krow < rs + NA_KH), 0.0, NEG)
    acc = jnp.zeros((NA_TOK, NA_W), F32)
    for h in range(NA_HEADS):
        mine = head_of_lane == h
        s = _nt(jnp.where(mine, q, 0.0).astype(BF16), kbuf) + (bias_ref[0, h] + row_mask)
        m = jnp.max(s, axis=-1, keepdims=True)
        p = jnp.exp(s - m)
        l = jnp.sum(p, axis=-1, keepdims=True)
        o = _mm((p * (1.0 / l)).astype(BF16), vbuf)
        acc = acc + jnp.where(mine, o, 0.0)
    o_ref[0] = acc


def _na(proj, bias_tbl, layer):
    bsz, t, _ = proj.shape
    rows = t // GRID_W
    nblk = rows // NA_RB
    qc, kc, vc = COL_NA // NA_W, COL_NA // NA_W + 1, COL_NA // NA_W + 2

    def spec(col, shift):
        return pl.BlockSpec((1, NA_TOK, NA_W), lambda b, i: (b, jnp.clip(i + shift, 0, nblk - 1), col))

    def na_kernel(*refs):
        _na_kernel(*refs, rows=rows)

    return pl.pallas_call(
        na_kernel,
        grid=(bsz, nblk),
        in_specs=[spec(qc, 0), spec(kc, -1), spec(kc, 0), spec(kc, 1), spec(vc, -1), spec(vc, 0), spec(vc, 1),
                  pl.BlockSpec((1, NA_HEADS, NA_TOK, NA_KEYS), lambda b, i: (layer, 0, 0, 0))],
        out_specs=pl.BlockSpec((1, NA_TOK, NA_W), lambda b, i: (b, i, 0)),
        out_shape=jax.ShapeDtypeStruct((bsz, t, NA_W), F32),
        compiler_params=_cparams(("parallel", "parallel")),
        name="na",
    )(proj, proj, proj, proj, proj, proj, proj, bias_tbl)


def _gdn_qkv(m_ref, p_ref, n_ref, w_ref):
    i = pl.program_id(1)
    nt = pl.num_programs(1)
    pu = jnp.where(i == 0, 0.0, p_ref[0, SUBLANES - 1:SUBLANES, :])
    nu = jnp.where(i == nt - 1, 0.0, n_ref[0, 0:1, :])
    c = _silu(_dwconv3(m_ref[0], pu, nu, w_ref))
    heads = []
    for hh in range(GDN_QKV // GDN_DK):
        xh = c[:, hh * GDN_DK:(hh + 1) * GDN_DK]
        if hh < 2 * GDN_HEADS:
            xh = xh * lax.rsqrt(jnp.sum(xh * xh, axis=-1, keepdims=True) + EPS)
            if hh < GDN_HEADS:
                xh = xh * (GDN_DK ** -0.5)
        heads.append(xh)
    return heads[:GDN_HEADS], heads[GDN_HEADS:2 * GDN_HEADS], heads[2 * GDN_HEADS:]


GDN_GS = 128
GDN_UNITS = 2 * GDN_HEADS
GDN_GC = GDN_GS // GDN_CHUNK
GDN_NG = 2
GDN_TS = GDN_NG * GDN_GS


def _softplus(x):
    return jnp.maximum(x, 0.0) + jnp.log1p(jnp.exp(-jnp.abs(x)))


def _seg_cumsum(x, axis, reverse):
    n = x.shape[axis]
    pos = lax.broadcasted_iota(jnp.int32, x.shape, axis) & (GDN_CHUNK - 1)
    s = 1
    while s < GDN_CHUNK:
        if reverse:
            x = x + jnp.where(pos < GDN_CHUNK - s, pltpu.roll(x, n - s, axis=axis), 0.0)
        else:
            x = x + jnp.where(pos >= s, pltpu.roll(x, s, axis=axis), 0.0)
        s *= 2
    return x


def _gdnchunk_kernel(m_ref, p_ref, n_ref, cw_ref, ab_ref, abt_ref, alr_ref, dtr_ref, alc_ref, dtc_ref,
                     u_ref, wq_ref, kd_ref, in_ref, gl_ref):
    gs, c, nh = GDN_GS, GDN_CHUNK, GDN_HEADS
    ri = lax.broadcasted_iota(jnp.int32, (gs, gs), 0)
    ci = lax.broadcasted_iota(jnp.int32, (gs, gs), 1)
    same = (ri // c) == (ci // c)
    eye = (ri == ci).astype(F32)
    rowc = lax.broadcasted_iota(jnp.int32, (gs, 1), 0) // c
    ab = ab_ref[0]
    abt = abt_ref[0]
    graw_c = -jnp.exp(alr_ref[...]) * _softplus(ab + dtr_ref[...])
    graw_r = -jnp.exp(alc_ref[...]) * _softplus(abt[0:2 * nh] + dtc_ref[...])
    beta_c = 1.0 / (1.0 + jnp.exp(-ab))
    g_col = [_seg_cumsum(graw_c, 0, False), _seg_cumsum(graw_c, 0, True)]
    g_row = [_seg_cumsum(graw_r, 1, False), _seg_cumsum(graw_r, 1, True)]
    incl = [same & (ri >= ci), same & (ri <= ci)]
    strict = [same & (ri > ci), same & (ri < ci)]

    qs, ks, vs = _gdn_qkv(m_ref, p_ref, n_ref, cw_ref)
    groups = [slice(g * gs, (g + 1) * gs) for g in range(GDN_NG)]
    grams = [[_nt(jnp.concatenate([q[rg], k[rg]], axis=0).astype(BF16), k[rg].astype(BF16))
              for q, k in zip(qs, ks)] for rg in groups]

    units = [(g, d, h) for g in range(GDN_NG) for d in range(2) for h in range(nh)]
    gcs, bcs, intras, xs, ps = [], [], [], [], []
    for g, d, h in units:
        col = d * nh + h
        rg = groups[g]
        gc = g_col[d][rg, col:col + 1]
        gr = g_row[d][col:col + 1, rg]
        bc = beta_c[rg, 2 * nh + col:2 * nh + col + 1]
        e_incl = jnp.exp(jnp.where(incl[d], gc - gr, NEG))
        a = grams[g][h][gs:] * bc * jnp.where(strict[d], e_incl, 0.0)
        gcs.append(gc), bcs.append(bc)
        intras.append(grams[g][h][:gs] * e_incl)
        xs.append(eye - a), ps.append(a)
    for _ in range(5):
        pbs = [p.astype(BF16) for p in ps]
        ps = [_mm(pb, pb) for pb in pbs]
        xs = [x + _mm(x.astype(BF16), p.astype(BF16)) for x, p in zip(xs, ps)]
    egs = [jnp.exp(gc) for gc in gcs]
    sols = [_mm(x.astype(BF16),
                jnp.concatenate([vs[h][groups[g]] * bc, ks[h][groups[g]] * (bc * eg)], axis=1).astype(BF16))
            for (g, d, h), x, bc, eg in zip(units, xs, bcs, egs)]
    for (g, d, h), gc, eg, sol, intra in zip(units, gcs, egs, sols, intras):
        col = d * nh + h
        rg = groups[g]
        glast_col = jnp.zeros_like(gc)
        for n in range(GDN_GC):
            r = n * c if d == 1 else (n + 1) * c - 1
            glast = gc[r:r + 1, :]
            glast_col = jnp.where(rowc == n, glast, glast_col)
            gl_ref[0, g, col * GDN_GC + n:col * GDN_GC + n + 1, :] = jnp.broadcast_to(jnp.exp(glast), (1, LANES))
        u_ref[0, col, rg, :] = sol[:, :GDN_DV]
        w = sol[:, GDN_DV:].astype(BF16)
        qd = (qs[h][rg] * eg).astype(BF16)
        for n in range(GDN_GC):
            base = 2 * (g * gs + n * c)
            wq_ref[0, col, base:base + c, :] = w[n * c:(n + 1) * c]
            wq_ref[0, col, base + c:base + 2 * c, :] = qd[n * c:(n + 1) * c]
        kd_ref[0, col, rg, :] = (ks[h][rg] * jnp.exp(glast_col - gc)).astype(BF16)
        in_ref[0, col, rg, :] = jnp.concatenate([intra[n * c:(n + 1) * c, n * c:(n + 1) * c] for n in range(GDN_GC)],
                                                axis=0).astype(BF16)


def _gdnchunk(proj, abt, conv_w, a_log, dt_bias):
    bsz, t, _ = proj.shape
    ts, nu = GDN_TS, GDN_UNITS
    nab = abt.shape[1]
    pad = lambda r: jnp.pad(r.reshape(1, -1), ((0, 0), (0, LANES - r.size)))
    alr, dtr = pad(a_log), pad(dt_bias)
    alc, dtc = a_log.reshape(-1, 1), dt_bias.reshape(-1, 1)
    small = lambda shape: pl.BlockSpec(shape, lambda b, i: (0, 0))
    unit = lambda rows, w: pl.BlockSpec((1, nu, rows, w), lambda b, i: (b, 0, i, 0))
    return pl.pallas_call(
        _gdnchunk_kernel,
        grid=(bsz, t // ts),
        in_specs=_halo_specs(ts, t, GDN_QKV, COL_GQKV // GDN_QKV) + [
            small((3, GDN_QKV)),
            pl.BlockSpec((1, ts, LANES), lambda b, i: (b, i, COL_AB // LANES)),
            pl.BlockSpec((1, nab, ts), lambda b, i: (b, 0, i)),
            small((1, LANES)), small((1, LANES)), small((nu, 1)), small((nu, 1))],
        out_specs=[unit(ts, GDN_DV), unit(2 * ts, GDN_DK), unit(ts, GDN_DK), unit(ts, GDN_CHUNK),
                   pl.BlockSpec((1, GDN_NG, nu * GDN_GC, LANES), lambda b, i: (b, i, 0, 0))],
        out_shape=[jax.ShapeDtypeStruct((bsz, nu, t, GDN_DV), F32),
                   jax.ShapeDtypeStruct((bsz, nu, 2 * t, GDN_DK), BF16),
                   jax.ShapeDtypeStruct((bsz, nu, t, GDN_DK), BF16),
                   jax.ShapeDtypeStruct((bsz, nu, t, GDN_CHUNK), BF16),
                   jax.ShapeDtypeStruct((bsz, t // GDN_GS, nu * GDN_GC, LANES), F32)],
        compiler_params=_cparams(("parallel", "parallel")),
        name="gdnchunk",
    )(proj, proj, proj, conv_w, proj, abt, alr, dtr, alc, dtc)


GDN_SB = 8
GDN_ST = GDN_SB * GDN_CHUNK


def _gdnscan_kernel(uf, ub, wqf, wqb, kdf, kdb, inf, inb, glf, glb, of_ref, ob_ref, s_ref):
    i = pl.program_id(0)

    @pl.when(i == 0)
    def _():
        s_ref[...] = jnp.zeros_like(s_ref)

    bsz = uf.shape[0]
    c, nh = GDN_CHUNK, GDN_HEADS
    chains = [(b, d, h) for b in range(bsz) for d in range(2) for h in range(nh)]
    for step in range(GDN_SB):
        zs, vns = [], []
        for b, d, h in chains:
            cc = step if d == 0 else GDN_SB - 1 - step
            wq = (wqf, wqb)[d]
            st = s_ref[(b * 2 + d) * nh + h]
            zs.append(_mm(wq[b, h, 2 * cc * c:(2 * cc + 2) * c, :], st.astype(BF16)))
        for (b, d, h), z in zip(chains, zs):
            cc = step if d == 0 else GDN_SB - 1 - step
            u = (uf, ub)[d]
            vns.append((u[b, h, cc * c:(cc + 1) * c, :] - z[:c]).astype(BF16))
        for (b, d, h), z, vn in zip(chains, zs, vns):
            cc = step if d == 0 else GDN_SB - 1 - step
            rows = slice(cc * c, (cc + 1) * c)
            intra = (inf, inb)[d]
            o_ref = (of_ref, ob_ref)[d]
            o_ref[b, rows, h * GDN_DV:(h + 1) * GDN_DV] = z[c:] + _mm(intra[b, h, rows, :], vn)
        for (b, d, h), vn in zip(chains, vns):
            cc = step if d == 0 else GDN_SB - 1 - step
            rows = slice(cc * c, (cc + 1) * c)
            kd = (kdf, kdb)[d]
            gl = (glf, glb)[d]
            r = (d * nh + h) * GDN_GC + cc % GDN_GC
            sidx = (b * 2 + d) * nh + h
            s_ref[sidx] = s_ref[sidx] * gl[b, cc // GDN_GC, r:r + 1, :] + _tn(kd[b, h, rows, :], vn)


def _gdnscan(u, wq, kd, intra, gl):
    bsz, nu, t, _ = u.shape
    nh = GDN_HEADS
    st = GDN_ST
    nb = t // st
    ngs = st // GDN_GS

    def unit(rows, w, d):
        if d == 0:
            return pl.BlockSpec((bsz, nh, rows, w), lambda i: (0, 0, i, 0))
        return pl.BlockSpec((bsz, nh, rows, w), lambda i: (0, 1, nb - 1 - i, 0))

    glspec = lambda d: pl.BlockSpec((bsz, ngs, nu * GDN_GC, LANES),
                                    (lambda i: (0, i, 0, 0)) if d == 0 else (lambda i: (0, nb - 1 - i, 0, 0)))
    return pl.pallas_call(
        _gdnscan_kernel,
        grid=(nb,),
        in_specs=[unit(st, GDN_DV, 0), unit(st, GDN_DV, 1), unit(2 * st, GDN_DK, 0), unit(2 * st, GDN_DK, 1),
                  unit(st, GDN_DK, 0), unit(st, GDN_DK, 1), unit(st, GDN_CHUNK, 0), unit(st, GDN_CHUNK, 1),
                  glspec(0), glspec(1)],
        out_specs=[pl.BlockSpec((bsz, st, GDN_W), lambda i: (0, i, 0)),
                   pl.BlockSpec((bsz, st, GDN_W), lambda i: (0, nb - 1 - i, 0))],
        out_shape=[jax.ShapeDtypeStruct((bsz, t, GDN_W), F32), jax.ShapeDtypeStruct((bsz, t, GDN_W), F32)],
        scratch_shapes=[pltpu.VMEM((bsz * nu, GDN_DK, GDN_DV), F32)],
        compiler_params=_cparams(("arbitrary",)),
        name="gdnscan",
    )(u, u, wq, wq, kd, kd, intra, intra, gl, gl)


OUT_SPLIT = 1


def _outproj_kernel(x_ref, cm_ref, cp_ref, cn_ref, cw_ref, yn_ref, of_ref, ob_ref, z_ref, gnw_ref, wo_ref, g1_ref,
                    nw_ref, sc_ref, sh_ref, wr_ref, br_ref, xo_ref, h_ref, e_ref, g_ref, hist_ref):
    yc = _convmix_tile(cm_ref, cp_ref, cn_ref, cw_ref)
    tm = x_ref.shape[1]
    lane = lax.broadcasted_iota(jnp.int32, (tm // OUT_SPLIT, LANES), 1)
    picks = []
    for r in range(OUT_SPLIT):
        rows = slice(r * (tm // OUT_SPLIT), (r + 1) * (tm // OUT_SPLIT))
        o = of_ref[0, rows, :] + ob_ref[0, rows, :]
        z = z_ref[0, rows, :]
        parts = [yc[rows], yn_ref[0, rows, :]]
        for h in range(GDN_HEADS):
            sl = slice(h * GDN_DV, (h + 1) * GDN_DV)
            oh = o[:, sl]
            oh = oh * lax.rsqrt(jnp.mean(oh * oh, axis=-1, keepdims=True) + EPS) * gnw_ref[...]
            parts.append(oh * _silu(z[:, sl]))
        mixed = _mm(jnp.concatenate(parts, axis=-1).astype(BF16), wo_ref[0])
        xn = x_ref[0, rows, :] + g1_ref[0] * mixed
        xo_ref[0, rows, :] = xn
        hf = _modnorm(xn, nw_ref[...], sc_ref[0], sh_ref[0])
        h_ref[0, rows, :] = hf
        hf_hi = hf.astype(BF16)
        hf_lo = (hf - hf_hi.astype(F32)).astype(BF16)
        logits = (_mm(hf_hi, wr_ref[0]) + (_mm(hf_hi, wr_ref[1]) + _mm(hf_lo, wr_ref[0]))) + br_ref[...]
        gl = jnp.where(lane < N_GROUPS, logits, NEG)
        gm = jnp.max(gl, axis=-1, keepdims=True)
        den = jnp.sum(jnp.exp(gl - gm), axis=-1, keepdims=True)
        grp = jnp.min(jnp.where(gl == gm, lane, LANES), axis=-1, keepdims=True)
        pg_top = 1.0 / den
        ex = lane - N_GROUPS
        in_grp = (ex >= grp * EXPERTS_PER_GROUP) & (ex < (grp + 1) * EXPERTS_PER_GROUP)
        el = jnp.where(in_grp, logits, NEG)
        m1 = jnp.max(el, axis=-1, keepdims=True)
        i1 = jnp.min(jnp.where(el == m1, lane, LANES), axis=-1, keepdims=True)
        el2 = jnp.where(lane == i1, NEG, el)
        m2 = jnp.max(el2, axis=-1, keepdims=True)
        i2 = jnp.min(jnp.where(el2 == m2, lane, LANES), axis=-1, keepdims=True)
        e2 = jnp.exp(m2 - m1)
        w1 = pg_top / (1.0 + e2)
        w2 = pg_top * e2 / (1.0 + e2)
        g_ref[0, rows, :] = jnp.where(lane == 0, w1, jnp.where(lane == 1, w2, 0.0))
        picks.append((i1 - N_GROUPS, i2 - N_GROUPS))
    e1 = jnp.concatenate([p[0] for p in picks], axis=0)
    e2i = jnp.concatenate([p[1] for p in picks], axis=0)
    lane = lax.broadcasted_iota(jnp.int32, (tm, LANES), 1)
    oh1 = (lane == e1).astype(F32)
    oh2 = (lane == e2i).astype(F32)
    both = oh1 + oh2
    earlier = (lax.broadcasted_iota(jnp.int32, (tm, tm), 0) > lax.broadcasted_iota(jnp.int32, (tm, tm), 1))
    cnt = _mm(earlier.astype(BF16), both.astype(BF16))
    r1 = jnp.sum(cnt * oh1, axis=-1, keepdims=True).astype(jnp.int32)
    r2 = jnp.sum(cnt * oh2, axis=-1, keepdims=True).astype(jnp.int32)
    e_ref[0] = jnp.where(lane == 0, e1, jnp.where(lane == 1, e2i, jnp.where(lane == 2, r1, jnp.where(lane == 3, r2, 0))))
    hist_ref[0, 0] = jnp.broadcast_to(jnp.sum(both, axis=0, keepdims=True), (SUBLANES, LANES))


def _outproj(x, conv_w, yn, of, ob, proj, gnw, wo, g1, nw, sc, sh, wr, br, layer):
    bsz, t, d = x.shape
    tm = min(ROW_TILE, t)
    tok = lambda w: pl.BlockSpec((1, tm, w), lambda b, i: (b, i, 0))
    perb = pl.BlockSpec((1, 1, d), lambda b, i: (b, 0, 0))
    full = lambda shape: pl.BlockSpec(shape, lambda b, i: (0, 0))
    return pl.pallas_call(
        _outproj_kernel,
        grid=(bsz, t // tm),
        in_specs=[tok(d)] + _halo_specs(tm, t, 3 * CONV_W, COL_CONV // (3 * CONV_W)) + [
                  full((3, CONV_W)), tok(NA_W), tok(GDN_W), tok(GDN_W),
                  pl.BlockSpec((1, tm, GDN_W), lambda b, i: (b, i, COL_GZ // GDN_W)),
                  full((1, GDN_DV)), pl.BlockSpec((1, d, d), lambda b, i: (layer, 0, 0)), perb, full((1, d)), perb, perb,
                  pl.BlockSpec((2, d, LANES), lambda b, i: (0, 0, 0)), full((1, LANES))],
        out_specs=[tok(d), tok(d), tok(LANES), tok(LANES),
                   pl.BlockSpec((1, 1, SUBLANES, LANES), lambda b, i: (b, i, 0, 0))],
        out_shape=[jax.ShapeDtypeStruct((bsz, t, d), F32), jax.ShapeDtypeStruct((bsz, t, d), F32),
                   jax.ShapeDtypeStruct((bsz, t, LANES), jnp.int32), jax.ShapeDtypeStruct((bsz, t, LANES), F32),
                   jax.ShapeDtypeStruct((bsz, t // tm, SUBLANES, LANES), F32)],
        compiler_params=_cparams(("parallel", "parallel")),
        name="outproj",
    )(x, proj, proj, proj, conv_w, yn, of, ob, proj, gnw, wo, g1, nw, sc, sh, wr, br)


MOE_TM = 256


PIECE = SUBLANES
SORT_ROWS = TOP_K * ROW_TILE + N_EXPERTS * PIECE
PIECE_UNITS = (4, 2, 1)
PIECE_SLOTS = (SORT_ROWS // (PIECE_UNITS[0] * PIECE), N_EXPERTS, N_EXPERTS)
PIECE_TABLE = 2 * sum(PIECE_SLOTS)


def _piece_loops(tbl_ref, cnt_ref, tile, fn):
    off = 0
    for c, (units, slots) in enumerate(zip(PIECE_UNITS, PIECE_SLOTS)):
        def body(g, carry, off=off, units=units, slots=slots):
            fn(pl.multiple_of(tbl_ref[0, 0, off + g] * PIECE, PIECE),
               pl.multiple_of(tbl_ref[0, 0, off + slots + g] * PIECE, PIECE), units * PIECE)
            return carry
        lax.fori_loop(0, cnt_ref[tile * len(PIECE_UNITS) + c], body, 0)
        off += 2 * slots


def _dispatch_kernel(lb_ref, nu_ref, np_ref, tbl_ref, tprev_ref, e_ref, a8_ref, h_ref, xs_out, zbuf, sbuf, sem, zsem):
    i = pl.program_id(0)
    nt = pl.num_programs(0)
    tm = h_ref.shape[0]
    bm = zbuf.shape[0]
    nblk = xs_out.shape[0] // bm
    slot = i % 2

    def zero_copy(blk):
        return pltpu.make_async_copy(zbuf, xs_out.at[pl.ds(pl.multiple_of(blk * bm, bm), bm), :], zsem)

    @pl.when(i == 0)
    def _():
        zbuf[...] = jnp.zeros_like(zbuf)
        for e in range(N_EXPERTS):
            @pl.when(lb_ref[e] >= 0)
            def _():
                zero_copy(lb_ref[e]).start()

        def start_trailing(j, carry):
            zero_copy(j).start()
            return carry

        def wait_one(j, carry):
            zero_copy(0).wait()
            return carry

        lax.fori_loop(nu_ref[0], nblk, start_trailing, 0)
        lax.fori_loop(nu_ref[0], nblk, wait_one, 0)
        for e in range(N_EXPERTS):
            @pl.when(lb_ref[e] >= 0)
            def _():
                zero_copy(0).wait()

    e = e_ref[...]
    lane = lax.broadcasted_iota(jnp.int32, e.shape, 1)
    a8 = a8_ref[0, 0:1, :]
    rows = []
    eye = lax.broadcasted_iota(jnp.int32, (tm, tm), 0) == lax.broadcasted_iota(jnp.int32, (tm, tm), 1)
    for k in range(TOP_K):
        col = (jnp.sum(jnp.where(lane == e[:, k:k + 1], a8, 0.0), axis=-1, keepdims=True)
               + e[:, TOP_K + k:TOP_K + k + 1].astype(F32))
        rows.append(jnp.sum(jnp.where(eye, col, 0.0), axis=0, keepdims=True))
    j = lax.broadcasted_iota(jnp.int32, (SORT_ROWS, tm), 0).astype(F32)
    perm = ((j == rows[0]) | (j == rows[1])).astype(BF16)
    hb = h_ref[...].astype(BF16)

    def piece(s, wait):
        def fn(src, dst, nrows):
            cp = pltpu.make_async_copy(sbuf.at[s, pl.ds(src, nrows), :], xs_out.at[pl.ds(dst, nrows), :], sem.at[s])
            cp.wait() if wait else cp.start()
        return fn

    sbuf[slot] = _mm(perm, hb)
    _piece_loops(tbl_ref, np_ref, i, piece(slot, False))

    @pl.when(i > 0)
    def _():
        _piece_loops(tprev_ref, np_ref, jnp.maximum(i - 1, 0), piece(1 - slot, True))

    @pl.when(i == nt - 1)
    def _():
        _piece_loops(tbl_ref, np_ref, i, piece(slot, True))


def _dispatch(hf, eidx, a8, ptab, npieces, last_blk, n_used, padded_rows):
    n, d = hf.shape
    tm = ROW_TILE
    nt = n // tm
    grid_spec = pltpu.PrefetchScalarGridSpec(
        num_scalar_prefetch=3,
        grid=(nt,),
        in_specs=[pl.BlockSpec((1, 1, PIECE_TABLE), lambda i, lb, nu, npc: (i, 0, 0), memory_space=pltpu.SMEM),
                  pl.BlockSpec((1, 1, PIECE_TABLE), lambda i, lb, nu, npc: (jnp.maximum(i - 1, 0), 0, 0),
                               memory_space=pltpu.SMEM),
                  pl.BlockSpec((tm, LANES), lambda i, lb, nu, npc: (i, 0)),
                  pl.BlockSpec((1, SUBLANES, LANES), lambda i, lb, nu, npc: (i, 0, 0)),
                  pl.BlockSpec((tm, d), lambda i, lb, nu, npc: (i, 0))],
        out_specs=pl.BlockSpec(memory_space=pl.ANY),
        scratch_shapes=[pltpu.VMEM((MOE_BM, d), F32), pltpu.VMEM((2, SORT_ROWS, d), F32),
                        pltpu.SemaphoreType.DMA((2,)), pltpu.SemaphoreType.DMA(())])
    return pl.pallas_call(
        _dispatch_kernel,
        grid_spec=grid_spec,
        out_shape=jax.ShapeDtypeStruct((padded_rows, d), F32),
        compiler_params=_cparams(("arbitrary",)),
        name="dispatch",
    )(last_blk, n_used, npieces, ptab, ptab, eidx, a8, hf)


def _experts_kernel(be_ref, nu_ref, nx_ref, sg_ref, x_ref, w1_hbm, w3_hbm, w2_hbm, o_ref,
                    w1f, w3f, w2f, w1b, w3b, w2b, sem, *, layer):
    i = pl.program_id(0)
    used = nu_ref[0]

    def fetch(e, s):
        return [pltpu.make_async_copy(w_hbm.at[layer, e], wf.at[s], sem.at[s])
                for w_hbm, wf in ((w1_hbm, w1f), (w3_hbm, w3f), (w2_hbm, w2f))]

    first = ((i == 0) | (be_ref[i] != be_ref[jnp.maximum(i - 1, 0)])) & (i < used)

    @pl.when(first)
    def _():
        s = sg_ref[i] % 2

        @pl.when(i == 0)
        def _():
            for cp in fetch(be_ref[0], 0):
                cp.start()

        for cp in fetch(be_ref[i], s):
            cp.wait()
        w1b[...] = w1f[s].astype(BF16)
        w3b[...] = w3f[s].astype(BF16)
        w2b[...] = w2f[s].astype(BF16)

        @pl.when(nx_ref[i] >= 0)
        def _():
            for cp in fetch(nx_ref[i], 1 - s):
                cp.start()

    @pl.when(i < used)
    def _():
        xb = x_ref[...].astype(BF16)
        act = (_silu(_mm(xb, w1b[...])) * _mm(xb, w3b[...])).astype(BF16)
        o_ref[...] = _mm(act, w2b[...])

    @pl.when(i >= used)
    def _():
        o_ref[...] = jnp.zeros_like(o_ref)


def _experts(xs, block_e, n_used, next_e, seg_id, w1, w3, w2, layer):
    padded_rows, d = xs.shape
    bm = MOE_BM
    nblk = padded_rows // bm
    de = w1.shape[-1]
    grid_spec = pltpu.PrefetchScalarGridSpec(
        num_scalar_prefetch=4,
        grid=(nblk,),
        in_specs=[pl.BlockSpec((bm, d), lambda i, be, nu, nx, sg: (jnp.maximum(jnp.minimum(i, nu[0] - 1), 0), 0)),
                  pl.BlockSpec(memory_space=pl.ANY), pl.BlockSpec(memory_space=pl.ANY),
                  pl.BlockSpec(memory_space=pl.ANY)],
        out_specs=pl.BlockSpec((bm, d), lambda i, be, nu, nx, sg: (i, 0)),
        scratch_shapes=[pltpu.VMEM((2, d, de), F32), pltpu.VMEM((2, d, de), F32), pltpu.VMEM((2, de, d), F32),
                        pltpu.VMEM((d, de), BF16), pltpu.VMEM((d, de), BF16), pltpu.VMEM((de, d), BF16),
                        pltpu.SemaphoreType.DMA((2,))])

    def experts_kernel(*refs):
        _experts_kernel(*refs, layer=layer)

    return pl.pallas_call(
        experts_kernel,
        grid_spec=grid_spec,
        out_shape=jax.ShapeDtypeStruct((padded_rows, d), F32),
        compiler_params=_cparams(("arbitrary",)),
        name="experts",
    )(block_e, n_used, next_e, seg_id, xs, w1, w3, w2)


def _gather_rows(idx_ref, ys_hbm, ybuf, sem, slot, lo, hi):
    def body(r, carry):
        pltpu.make_async_copy(ys_hbm.at[pl.ds(idx_ref[0, 0, r], 1), :], ybuf.at[slot, pl.ds(r, 1), :],
                              sem.at[slot]).start()
        return carry
    lax.fori_loop(lo, hi, body, 0, unroll=8)


def _combine_rows(pc_ref, x_ref, gt_ref, g2_ref, ys_hbm, ybuf, sem):
    i = pl.program_id(0)
    tm = x_ref.shape[0]
    slot = i % 2

    @pl.when(i == 0)
    def _():
        _gather_rows(pc_ref, ys_hbm, ybuf, sem, 0, 0, TOP_K * tm)

    pltpu.make_async_copy(ys_hbm.at[pl.ds(0, TOP_K * tm), :], ybuf.at[slot], sem.at[slot]).wait()
    gt = gt_ref[...]
    y = gt[:, 0:1] * ybuf[slot, 0:tm, :] + gt[:, 1:2] * ybuf[slot, tm:2 * tm, :]
    return x_ref[...] + g2_ref[0] * y


def _combine_final_kernel(np_ref, dc_ref, dn_ref, e_ref, a8_ref, x_ref, gt_ref, g2_ref, fw_ref, ys_hbm, o_ref,
                          ybuf, sem):
    i = pl.program_id(0)
    nt = pl.num_programs(0)
    tm = x_ref.shape[0]
    slot = i % 2

    def piece(s, wait):
        def fn(loc, glob, nrows):
            cp = pltpu.make_async_copy(ys_hbm.at[pl.ds(glob, nrows), :], ybuf.at[s, pl.ds(loc, nrows), :], sem.at[s])
            cp.wait() if wait else cp.start()
        return fn

    @pl.when(i == 0)
    def _():
        ybuf[...] = jnp.zeros_like(ybuf)
        _piece_loops(dc_ref, np_ref, 0, piece(0, False))

    @pl.when(i + 1 < nt)
    def _():
        _piece_loops(dn_ref, np_ref, jnp.minimum(i + 1, nt - 1), piece(1 - slot, False))

    _piece_loops(dc_ref, np_ref, i, piece(slot, True))
    e = e_ref[...]
    lane = lax.broadcasted_iota(jnp.int32, e.shape, 1)
    a8 = a8_ref[0, 0:1, :]
    yb = ybuf[slot].astype(BF16)
    j = lax.broadcasted_iota(jnp.int32, (tm, SORT_ROWS), 1).astype(F32)
    gt = gt_ref[...]
    y = jnp.zeros((tm, x_ref.shape[1]), F32)
    for k in range(TOP_K):
        row = (jnp.sum(jnp.where(lane == e[:, k:k + 1], a8, 0.0), axis=-1, keepdims=True)
               + e[:, TOP_K + k:TOP_K + k + 1].astype(F32))
        y = y + gt[:, k:k + 1] * _mm((j == row).astype(BF16), yb)
    xn = x_ref[...] + g2_ref[0] * y
    ms = jnp.mean(xn * xn, axis=-1, keepdims=True)
    o_ref[...] = xn * lax.rsqrt(ms + EPS) * fw_ref[...]


PROJ_CHUNK = 512


def _combine_inproj_kernel(pc_ref, pn_ref, x_ref, gt_ref, g2_ref, nw_ref, sc_ref, sh_ref, w_ref, wabt_ref, ys_hbm,
                           xo_ref, o_ref, ot_ref, ybuf, sem):
    i = pl.program_id(0)
    tm = x_ref.shape[0]
    xn = _combine_rows(pc_ref, x_ref, gt_ref, g2_ref, ys_hbm, ybuf, sem)
    xo_ref[...] = xn
    h = _modnorm(xn, nw_ref[...], sc_ref[0], sh_ref[0]).astype(BF16)
    ot_ref[0] = _nt(wabt_ref[0], h)
    ncol = o_ref.shape[2]
    starts = list(range(0, ncol, PROJ_CHUNK))
    per = TOP_K * tm // len(starts)
    nxt = 1 - i % 2
    for j, c0 in enumerate(starts):
        for r in range(j * per, TOP_K * tm if j == len(starts) - 1 else (j + 1) * per):
            pltpu.make_async_copy(ys_hbm.at[pl.ds(pn_ref[0, 0, r], 1), :], ybuf.at[nxt, pl.ds(r, 1), :],
                                  sem.at[nxt]).start()
        c1 = min(c0 + PROJ_CHUNK, ncol)
        o_ref[0, :, c0:c1] = _mm(h, w_ref[0, :, c0:c1])

    @pl.when(i + 1 == pl.num_programs(0))
    def _():
        pltpu.make_async_copy(ys_hbm.at[pl.ds(0, TOP_K * tm), :], ybuf.at[nxt], sem.at[nxt]).wait()


def _combine_specs(n, d, t):
    tm = MOE_TM
    nt = n // tm
    per_b = t // tm
    specs = [pl.BlockSpec((1, 1, TOP_K * tm), lambda i: (i, 0, 0), memory_space=pltpu.SMEM),
             pl.BlockSpec((1, 1, TOP_K * tm), lambda i: (jnp.minimum(i + 1, nt - 1), 0, 0), memory_space=pltpu.SMEM),
             pl.BlockSpec((tm, d), lambda i: (i, 0)),
             pl.BlockSpec((tm, LANES), lambda i: (i, 0)),
             pl.BlockSpec((1, 1, d), lambda i: (i // per_b, 0, 0))]
    scratch = [pltpu.VMEM((2, TOP_K * tm, d), F32), pltpu.SemaphoreType.DMA((2,))]
    return tm, nt, per_b, specs, scratch


def _combine_final(x2, eidx, gates, a8, ptab, npieces, g2, fw, ys, t):
    n, d = x2.shape
    tm = ROW_TILE
    nt = n // tm
    per_b = t // tm
    grid_spec = pltpu.PrefetchScalarGridSpec(
        num_scalar_prefetch=1,
        grid=(nt,),
        in_specs=[pl.BlockSpec((1, 1, PIECE_TABLE), lambda i, npc: (i, 0, 0), memory_space=pltpu.SMEM),
                  pl.BlockSpec((1, 1, PIECE_TABLE), lambda i, npc: (jnp.minimum(i + 1, nt - 1), 0, 0),
                               memory_space=pltpu.SMEM),
                  pl.BlockSpec((tm, LANES), lambda i, npc: (i, 0)),
                  pl.BlockSpec((1, SUBLANES, LANES), lambda i, npc: (i, 0, 0)),
                  pl.BlockSpec((tm, d), lambda i, npc: (i, 0)),
                  pl.BlockSpec((tm, LANES), lambda i, npc: (i, 0)),
                  pl.BlockSpec((1, 1, d), lambda i, npc: (i // per_b, 0, 0)),
                  pl.BlockSpec((1, d), lambda i, npc: (0, 0)),
                  pl.BlockSpec(memory_space=pl.ANY)],
        out_specs=pl.BlockSpec((tm, d), lambda i, npc: (i, 0)),
        scratch_shapes=[pltpu.VMEM((2, SORT_ROWS, d), F32), pltpu.SemaphoreType.DMA((2,))])
    return pl.pallas_call(
        _combine_final_kernel,
        grid_spec=grid_spec,
        out_shape=jax.ShapeDtypeStruct((n, d), F32),
        compiler_params=_cparams(("arbitrary",)),
        name="combine",
    )(npieces, ptab, ptab, eidx, a8, x2, gates, g2, fw, ys)


def _combine_inproj(x2, gates, g2, ys, pos3, t, nw, sc, sh, w_pad, wabt, layer):
    n, d = x2.shape
    bsz = n // t
    nab = wabt.shape[1]
    tm, nt, per_b, specs, scratch = _combine_specs(n, d, t)
    perb = pl.BlockSpec((1, 1, d), lambda i: (i // per_b, 0, 0))
    return pl.pallas_call(
        _combine_inproj_kernel,
        grid=(nt,),
        in_specs=specs + [pl.BlockSpec((1, d), lambda i: (0, 0)), perb, perb,
                          pl.BlockSpec((1, d, D_IN_PAD), lambda i: (layer, 0, 0)),
                          pl.BlockSpec((1, nab, d), lambda i: (layer, 0, 0)),
                          pl.BlockSpec(memory_space=pl.ANY)],
        out_specs=[pl.BlockSpec((tm, d), lambda i: (i, 0)),
                   pl.BlockSpec((1, tm, D_IN_PAD), lambda i: (i // per_b, i % per_b, 0)),
                   pl.BlockSpec((1, nab, tm), lambda i: (i // per_b, 0, i % per_b))],
        out_shape=[jax.ShapeDtypeStruct((n, d), F32), jax.ShapeDtypeStruct((bsz, t, D_IN_PAD), F32),
                   jax.ShapeDtypeStruct((bsz, nab, t), F32)],
        scratch_shapes=scratch,
        compiler_params=_cparams(("arbitrary",)),
        name="combine_inproj",
    )(pos3, pos3, x2, gates, g2, nw, sc, sh, w_pad, wabt, ys)


def _moe_plan(eidx, hist):
    n = eidx.shape[0]
    bm = MOE_BM
    ntile = n // ROW_TILE
    experts = jnp.arange(N_EXPERTS, dtype=jnp.int32)
    hist = hist[:, 0, :N_EXPERTS].astype(jnp.int32)
    cnt = (hist + PIECE - 1) // PIECE * PIECE
    seg_start = jnp.cumsum(cnt, axis=1) - cnt
    sizes = jnp.sum(cnt, axis=0)
    base = jnp.cumsum(cnt, axis=0) - cnt
    padded = (sizes + bm - 1) // bm * bm
    pad_end = jnp.cumsum(padded)
    tbl = (pad_end - padded)[None, :] + base
    e = eidx[:, :TOP_K].reshape(ntile, ROW_TILE, TOP_K)
    onehot = e[..., None] == experts
    pos = jnp.sum(jnp.where(onehot, tbl[:, None, None, :], 0), axis=-1).reshape(n, TOP_K) + eidx[:, TOP_K:2 * TOP_K]
    left, tables, counts = cnt // PIECE, [], []
    done = jnp.zeros_like(left)
    for units, slots in zip(PIECE_UNITS, PIECE_SLOTS):
        num = left // units
        end = jnp.cumsum(num, axis=1)
        g = jnp.arange(slots, dtype=jnp.int32)
        e_of_g = jnp.minimum(jnp.sum(end[:, None, :] <= g[None, :, None], axis=-1), N_EXPERTS - 1)
        sel = e_of_g[..., None] == experts
        pick = lambda a, sel=sel: jnp.sum(jnp.where(sel, a[:, None, :], 0), axis=-1)
        step = units * (g[None, :] - pick(end - num))
        valid = g[None, :] < end[:, -1:]
        tables += [jnp.where(valid, pick(seg_start // PIECE + done) + step, 0),
                   jnp.where(valid, pick(tbl // PIECE + done) + step, 0)]
        counts.append(end[:, -1])
        done = done + num * units
        left = left - num * units
    ptab = jnp.concatenate(tables, axis=1).astype(jnp.int32).reshape(ntile, 1, PIECE_TABLE)
    npieces = jnp.stack(counts, axis=1).astype(jnp.int32).reshape(-1)
    seg8 = jnp.broadcast_to(jnp.pad(seg_start.astype(F32), ((0, 0), (0, LANES - N_EXPERTS)))[:, None, :],
                            (ntile, SUBLANES, LANES))
    padded_rows = (n * TOP_K + ntile * N_EXPERTS * (PIECE - 1) + N_EXPERTS * (bm - 1) + bm - 1) // bm * bm
    n_blocks = padded_rows // bm
    blk_start = jnp.arange(n_blocks, dtype=jnp.int32) * bm
    block_e = jnp.minimum(jnp.sum(pad_end[None, :] <= blk_start[:, None], axis=1), N_EXPERTS - 1).astype(jnp.int32)
    n_used = (pad_end[-1] // bm).astype(jnp.int32).reshape(1)
    last_blk = jnp.where(padded > 0, pad_end // bm - 1, -1).astype(jnp.int32)
    blk = jnp.arange(n_blocks, dtype=jnp.int32)
    change = (blk == 0) | (block_e != jnp.roll(block_e, 1))
    seg_id = (jnp.cumsum(change.astype(jnp.int32)) - 1).astype(jnp.int32)
    later = (blk[None, :] > blk[:, None]) & (block_e[None, :] != block_e[:, None]) & (blk[None, :] < n_used[0])
    nxt = jnp.min(jnp.where(later, blk[None, :], n_blocks), axis=1)
    next_e = jnp.where(nxt < n_blocks, block_e[jnp.minimum(nxt, n_blocks - 1)], -1).astype(jnp.int32)
    nt = n // MOE_TM
    pos3 = jnp.transpose(pos.astype(jnp.int32).reshape(nt, MOE_TM, TOP_K), (0, 2, 1)).reshape(nt, 1, TOP_K * MOE_TM)
    return pos3, seg8, ptab, npieces, block_e, n_used, next_e, seg_id, last_blk, padded_rows


def kernel(x, c, norm_mix_w, norm_ffn_w, w_ada, b_ada, w_in, conv_a_w, na_rpb, gdn_conv_w, gdn_a_log, gdn_dt_bias,
           gdn_norm_w, w_out, router_group_w, router_group_b, router_expert_w, router_expert_b, expert_w1,
           expert_w3, expert_w2, final_norm_w):
    bsz, t, d = x.shape
    depth = w_ada.shape[0]
    n = bsz * t
    assert t % ROW_TILE == 0 and (t // GRID_W) % NA_KH == 0, "sequence length must be a multiple of 512"
    mod = _ada(c, w_ada, b_ada)
    w_in_pad = jnp.pad(w_in.astype(BF16), ((0, 0), (0, 0), (0, D_IN_PAD - D_IN)))
    wabt = jnp.transpose(w_in[:, :, COL_AB:], (0, 2, 1)).astype(BF16)
    w_out_b = w_out.astype(BF16)
    wr = jnp.pad(jnp.concatenate([router_group_w, router_expert_w], axis=-1),
                 ((0, 0), (0, 0), (0, LANES - N_GROUPS - N_EXPERTS)))
    wr_hi = wr.astype(BF16)
    wr = jnp.stack([wr_hi, (wr - wr_hi.astype(F32)).astype(BF16)], axis=1)
    br = jnp.pad(jnp.concatenate([router_group_b, router_expert_b], axis=-1),
                 ((0, 0), (0, LANES - N_GROUPS - N_EXPERTS)))
    fw = final_norm_w.reshape(1, d)
    na_bias = _na_bias_table(na_rpb)
    mods = [[mod[l, :, j * d:(j + 1) * d].reshape(bsz, 1, d) for j in range(6)] for l in range(depth)]
    proj, abt = _inproj(x, norm_mix_w[0].reshape(1, d), mods[0][1], mods[0][0], w_in_pad, wabt, 0)
    for l in range(depth):
        sh1, sc1, g1, sh2, sc2, g2 = mods[l]
        y_na = _na(proj, na_bias, l)
        o_f, o_b = _gdnscan(*_gdnchunk(proj, abt, gdn_conv_w[l], gdn_a_log[l], gdn_dt_bias[l]))
        x, hf, eidx, gates, hist = _outproj(x, conv_a_w[l], y_na, o_f, o_b, proj, gdn_norm_w[l].reshape(1, GDN_DV),
                                            w_out_b, g1, norm_ffn_w[l].reshape(1, d), sc2, sh2, wr[l],
                                            br[l].reshape(1, LANES), l)
        eidx = eidx.reshape(n, LANES)
        pos3, seg8, ptab, npieces, block_e, n_used, next_e, seg_id, last_blk, padded_rows = _moe_plan(
            eidx, hist.reshape(-1, SUBLANES, LANES))
        xs = _dispatch(hf.reshape(n, d), eidx, seg8, ptab, npieces, last_blk, n_used, padded_rows)
        ys = _experts(xs, block_e, n_used, next_e, seg_id, expert_w1, expert_w3, expert_w2, l)
        x2, gt = x.reshape(n, d), gates.reshape(n, LANES)
        if l + 1 < depth:
            x2, proj, abt = _combine_inproj(x2, gt, g2, ys, pos3, t, norm_mix_w[l + 1].reshape(1, d), mods[l + 1][1],
                                            mods[l + 1][0], w_in_pad, wabt, l + 1)
            x = x2.reshape(bsz, t, d)
        else:
            x = _combine_final(x2, eidx, gt, seg8, ptab, npieces, g2, fw, ys, t).reshape(bsz, t, d)
    return x
```

```python
import jax
import jax.numpy as jnp
from jax import lax
from jax.experimental import pallas as pl
from jax.experimental.pallas import tpu as pltpu

F32 = jnp.float32
BF16 = jnp.bfloat16

EPS = 1e-6
GRID_W = 64
CONV_W = 256
NA_HEADS = 4
NA_DH = 64
NA_W = NA_HEADS * NA_DH
NA_KH = 8
NA_KW = 16
GDN_HEADS = 4
GDN_DK = 128
GDN_DV = 128
GDN_W = GDN_HEADS * GDN_DV
GDN_QKV = 2 * GDN_HEADS * GDN_DK + GDN_W
GDN_CHUNK = 64
N_GROUPS = 4
EXPERTS_PER_GROUP = 8
N_EXPERTS = N_GROUPS * EXPERTS_PER_GROUP
TOP_K = 2

COL_CONV = 0
COL_NA = 3 * CONV_W
COL_GQKV = COL_NA + 3 * NA_W
COL_GZ = COL_GQKV + GDN_QKV
COL_AB = COL_GZ + GDN_W
D_IN = COL_AB + 4 * GDN_HEADS
D_IN_PAD = COL_AB + 128
LANES = 128
SUBLANES = 8

NEG = -1e30
VMEM_LIMIT = 56 * 1024 * 1024

MOE_BM = 512
ROW_TILE = 512


def _cparams(sem):
    return pltpu.CompilerParams(dimension_semantics=sem, vmem_limit_bytes=VMEM_LIMIT)


def _silu(x):
    return x * (1.0 / (1.0 + jnp.exp(-x)))


def _nt(a, b):
    return lax.dot_general(a, b, (((1,), (1,)), ((), ())), preferred_element_type=F32)


def _tn(a, b):
    return lax.dot_general(a, b, (((0,), (0,)), ((), ())), preferred_element_type=F32)


def _mm(a, b):
    return jnp.dot(a, b, preferred_element_type=F32)


def _ada_kernel(ct_ref, w_ref, b_ref, o_ref):
    w = w_ref[0]
    nb = ct_ref.shape[1]
    for r in range(nb):
        col = _silu(ct_ref[:, r:r + 1])
        o_ref[0, r:r + 1, :] = jnp.sum(w * col, axis=0, keepdims=True) + b_ref[0]


def _ada(c, w_ada, b_ada):
    depth, d, n6 = w_ada.shape
    bsz = c.shape[0]
    tn = 1536
    return pl.pallas_call(
        _ada_kernel,
        grid=(depth, n6 // tn),
        in_specs=[pl.BlockSpec((d, bsz), lambda l, j: (0, 0)),
                  pl.BlockSpec((1, d, tn), lambda l, j: (l, 0, j)),
                  pl.BlockSpec((1, 1, tn), lambda l, j: (l, 0, j))],
        out_specs=pl.BlockSpec((1, bsz, tn), lambda l, j: (l, 0, j)),
        out_shape=jax.ShapeDtypeStruct((depth, bsz, n6), F32),
        compiler_params=_cparams(("parallel", "parallel")),
        name="ada",
    )(c.T, w_ada, b_ada.reshape(depth, 1, n6))


def _modnorm(x, nw, sc, sh):
    ms = jnp.mean(x * x, axis=-1, keepdims=True)
    y = x * lax.rsqrt(ms + EPS)
    return (y * nw) * (1.0 + sc) + sh


def _project(h, w_ref, wabt_ref, o_ref, ot_ref):
    o_ref[0] = _mm(h, w_ref[0])
    ot_ref[0] = _nt(wabt_ref[0], h)


def _inproj_kernel(x_ref, nw_ref, sc_ref, sh_ref, w_ref, wabt_ref, o_ref, ot_ref):
    h = _modnorm(x_ref[0], nw_ref[...], sc_ref[0], sh_ref[0]).astype(BF16)
    _project(h, w_ref, wabt_ref, o_ref, ot_ref)


def _inproj(x, nw, sc, sh, w_pad, wabt, layer):
    bsz, t, d = x.shape
    tm = min(ROW_TILE, t)
    nab = wabt.shape[1]
    return pl.pallas_call(
        _inproj_kernel,
        grid=(bsz, t // tm),
        in_specs=[pl.BlockSpec((1, tm, d), lambda b, i: (b, i, 0)),
                  pl.BlockSpec((1, d), lambda b, i: (0, 0)),
                  pl.BlockSpec((1, 1, d), lambda b, i: (b, 0, 0)),
                  pl.BlockSpec((1, 1, d), lambda b, i: (b, 0, 0)),
                  pl.BlockSpec((1, d, D_IN_PAD), lambda b, i: (layer, 0, 0)),
                  pl.BlockSpec((1, nab, d), lambda b, i: (layer, 0, 0))],
        out_specs=[pl.BlockSpec((1, tm, D_IN_PAD), lambda b, i: (b, i, 0)),
                   pl.BlockSpec((1, nab, tm), lambda b, i: (b, 0, i))],
        out_shape=[jax.ShapeDtypeStruct((bsz, t, D_IN_PAD), F32),
                   jax.ShapeDtypeStruct((bsz, nab, t), F32)],
        compiler_params=_cparams(("parallel", "parallel")),
        name="inproj",
    )(x, nw, sc, sh, w_pad, wabt)


def _dwconv3(u, prev_row, next_row, w_ref):
    tt = u.shape[0]
    row = lax.broadcasted_iota(jnp.int32, u.shape, 0)
    dn = jnp.where(row == 0, prev_row, pltpu.roll(u, 1, axis=0))
    up = jnp.where(row == tt - 1, next_row, pltpu.roll(u, tt - 1, axis=0))
    return w_ref[0:1, :] * dn + w_ref[1:2, :] * u + w_ref[2:3, :] * up


def _halo_specs(tt, t, width, colblk):
    nsub = tt // SUBLANES
    last = t // SUBLANES - 1
    return [pl.BlockSpec((1, tt, width), lambda b, i: (b, i, colblk)),
            pl.BlockSpec((1, SUBLANES, width), lambda b, i: (b, jnp.maximum(i * nsub - 1, 0), colblk)),
            pl.BlockSpec((1, SUBLANES, width), lambda b, i: (b, jnp.minimum((i + 1) * nsub, last), colblk))]


def _convmix_tile(m_ref, p_ref, n_ref, w_ref):
    i = pl.program_id(1)
    nt = pl.num_programs(1)
    m = m_ref[0]
    u = m[:, CONV_W:2 * CONV_W] * m[:, 2 * CONV_W:]
    p = p_ref[0]
    n = n_ref[0]
    pu = p[SUBLANES - 1:SUBLANES, CONV_W:2 * CONV_W] * p[SUBLANES - 1:SUBLANES, 2 * CONV_W:]
    nu = n[0:1, CONV_W:2 * CONV_W] * n[0:1, 2 * CONV_W:]
    pu = jnp.where(i == 0, 0.0, pu)
    nu = jnp.where(i == nt - 1, 0.0, nu)
    return m[:, :CONV_W] * _dwconv3(u, pu, nu, w_ref)


NA_RB = NA_KH // 2
NA_TOK = NA_RB * GRID_W
NA_KEYS = 3 * NA_TOK


def _na_bias_table(rpb, rows):
    col = jnp.arange(GRID_W)
    cstart = jnp.clip(col - NA_KW // 2, 0, GRID_W - NA_KW)
    kc = jnp.arange(GRID_W)
    valid = (kc[None, :] >= cstart[:, None]) & (kc[None, :] < cstart[:, None] + NA_KW)
    dc = kc[None, :] - col[:, None] + (NA_KW - 1)
    onehot = (dc[None] == jnp.arange(2 * NA_KW - 1)[:, None, None]) & valid[None]
    cols = jnp.einsum('lhrd,dck->lhcrk', rpb, onehot.astype(F32), precision=lax.Precision.HIGHEST)
    cols = jnp.where(valid[:, None, :], cols, NEG)
    lo = NA_KH - 1 - NA_RB
    blk = jnp.stack([cols[:, :, :, lo - j:lo - j + 3 * NA_RB] for j in range(NA_RB)], axis=2)
    blk = blk.reshape(rpb.shape[0], 1, NA_HEADS, NA_TOK, NA_KEYS)
    masks = []
    for first_row in (0, NA_RB, rows - NA_RB):
        qrow = first_row + jnp.arange(NA_TOK) // GRID_W
        krow = first_row - NA_RB + jnp.arange(NA_KEYS) // GRID_W
        rs = jnp.clip(qrow - NA_KH // 2, 0, rows - NA_KH)[:, None]
        masks.append(jnp.where((krow[None, :] >= rs) & (krow[None, :] < rs + NA_KH), 0.0, NEG))
    return blk + jnp.stack(masks, axis=0).astype(F32)[None, :, None]


def _na_kernel(q_ref, kp_ref, kc_ref, kn_ref, vp_ref, vc_ref, vn_ref, bias_ref, o_ref):
    kbuf = jnp.concatenate([kp_ref[0], kc_ref[0], kn_ref[0]], axis=0).astype(BF16)
    vbuf = jnp.concatenate([vp_ref[0], vc_ref[0], vn_ref[0]], axis=0).astype(BF16)
    q = q_ref[0] * (NA_DH ** -0.5)
    head_of_lane = lax.broadcasted_iota(jnp.int32, (1, NA_W), 1) // NA_DH
    acc = jnp.zeros((NA_TOK, NA_W), F32)
    for h in range(NA_HEADS):
        mine = head_of_lane == h
        s = _nt(jnp.where(mine, q, 0.0).astype(BF16), kbuf) + bias_ref[0, 0, h]
        m = jnp.max(s, axis=-1, keepdims=True)
        p = jnp.exp(s - m)
        l = jnp.sum(p, axis=-1, keepdims=True)
        o = _mm((p * (1.0 / l)).astype(BF16), vbuf)
        acc = acc + jnp.where(mine, o, 0.0)
    o_ref[0] = acc


def _na(proj, bias_tbl, layer):
    bsz, t, _ = proj.shape
    rows = t // GRID_W
    nblk = rows // NA_RB
    qc, kc, vc = COL_NA // NA_W, COL_NA // NA_W + 1, COL_NA // NA_W + 2

    def spec(col, shift):
        return pl.BlockSpec((1, NA_TOK, NA_W), lambda b, i: (b, jnp.clip(i + shift, 0, nblk - 1), col))

    def variant(i):
        return jnp.where(i == 0, 0, jnp.where(i == nblk - 1, 2, 1))

    return pl.pallas_call(
        _na_kernel,
        grid=(bsz, nblk),
        in_specs=[spec(qc, 0), spec(kc, -1), spec(kc, 0), spec(kc, 1), spec(vc, -1), spec(vc, 0), spec(vc, 1),
                  pl.BlockSpec((1, 1, NA_HEADS, NA_TOK, NA_KEYS), lambda b, i: (layer, variant(i), 0, 0, 0))],
        out_specs=pl.BlockSpec((1, NA_TOK, NA_W), lambda b, i: (b, i, 0)),
        out_shape=jax.ShapeDtypeStruct((bsz, t, NA_W), F32),
        compiler_params=_cparams(("parallel", "parallel")),
        name="na",
    )(proj, proj, proj, proj, proj, proj, proj, bias_tbl)


def _gdn_qkv(m_ref, p_ref, n_ref, w_ref):
    i = pl.program_id(1)
    nt = pl.num_programs(1)
    pu = jnp.where(i == 0, 0.0, p_ref[0, SUBLANES - 1:SUBLANES, :])
    nu = jnp.where(i == nt - 1, 0.0, n_ref[0, 0:1, :])
    c = _silu(_dwconv3(m_ref[0], pu, nu, w_ref))
    heads = []
    for hh in range(GDN_QKV // GDN_DK):
        xh = c[:, hh * GDN_DK:(hh + 1) * GDN_DK]
        if hh < 2 * GDN_HEADS:
            xh = xh * lax.rsqrt(jnp.sum(xh * xh, axis=-1, keepdims=True) + EPS)
            if hh < GDN_HEADS:
                xh = xh * (GDN_DK ** -0.5)
        heads.append(xh)
    return heads[:GDN_HEADS], heads[GDN_HEADS:2 * GDN_HEADS], heads[2 * GDN_HEADS:]


GDN_GS = 128
GDN_UNITS = 2 * GDN_HEADS
GDN_GC = GDN_GS // GDN_CHUNK
GDN_NG = 2
GDN_TS = GDN_NG * GDN_GS


def _softplus(x):
    return jnp.maximum(x, 0.0) + jnp.log1p(jnp.exp(-jnp.abs(x)))


def _seg_cumsum(x, axis, reverse):
    n = x.shape[axis]
    pos = lax.broadcasted_iota(jnp.int32, x.shape, axis) & (GDN_CHUNK - 1)
    s = 1
    while s < GDN_CHUNK:
        if reverse:
            x = x + jnp.where(pos < GDN_CHUNK - s, pltpu.roll(x, n - s, axis=axis), 0.0)
        else:
            x = x + jnp.where(pos >= s, pltpu.roll(x, s, axis=axis), 0.0)
        s *= 2
    return x


def _gdnchunk_kernel(m_ref, p_ref, n_ref, cw_ref, ab_ref, abt_ref, alr_ref, dtr_ref, alc_ref, dtc_ref,
                     u_ref, wq_ref, kd_ref, in_ref, gl_ref):
    gs, c, nh = GDN_GS, GDN_CHUNK, GDN_HEADS
    ri = lax.broadcasted_iota(jnp.int32, (gs, gs), 0)
    ci = lax.broadcasted_iota(jnp.int32, (gs, gs), 1)
    same = (ri // c) == (ci // c)
    eye = (ri == ci).astype(F32)
    rowc = lax.broadcasted_iota(jnp.int32, (gs, 1), 0) // c
    ab = ab_ref[0]
    abt = abt_ref[0]
    graw_c = -jnp.exp(alr_ref[...]) * _softplus(ab + dtr_ref[...])
    graw_r = -jnp.exp(alc_ref[...]) * _softplus(abt[0:2 * nh] + dtc_ref[...])
    beta_c = 1.0 / (1.0 + jnp.exp(-ab))
    g_col = [_seg_cumsum(graw_c, 0, False), _seg_cumsum(graw_c, 0, True)]
    g_row = [_seg_cumsum(graw_r, 1, False), _seg_cumsum(graw_r, 1, True)]
    incl = [same & (ri >= ci), same & (ri <= ci)]
    strict = [same & (ri > ci), same & (ri < ci)]

    qs, ks, vs = _gdn_qkv(m_ref, p_ref, n_ref, cw_ref)
    groups = [slice(g * gs, (g + 1) * gs) for g in range(GDN_NG)]
    grams = [[_nt(jnp.concatenate([q[rg], k[rg]], axis=0).astype(BF16), k[rg].astype(BF16))
              for q, k in zip(qs, ks)] for rg in groups]

    units = [(g, d, h) for g in range(GDN_NG) for d in range(2) for h in range(nh)]
    gcs, bcs, intras, xs, ps = [], [], [], [], []
    for g, d, h in units:
        col = d * nh + h
        rg = groups[g]
        gc = g_col[d][rg, col:col + 1]
        gr = g_row[d][col:col + 1, rg]
        bc = beta_c[rg, 2 * nh + col:2 * nh + col + 1]
        e_incl = jnp.exp(jnp.where(incl[d], gc - gr, NEG))
        a = grams[g][h][gs:] * bc * jnp.where(strict[d], e_incl, 0.0)
        gcs.append(gc), bcs.append(bc)
        intras.append(grams[g][h][:gs] * e_incl)
        xs.append(eye - a), ps.append(a)
    for _ in range(5):
        pbs = [p.astype(BF16) for p in ps]
        ps = [_mm(pb, pb) for pb in pbs]
        xs = [x + _mm(x.astype(BF16), p.astype(BF16)) for x, p in zip(xs, ps)]
    egs = [jnp.exp(gc) for gc in gcs]
    sols = [_mm(x.astype(BF16),
                jnp.concatenate([vs[h][groups[g]] * bc, ks[h][groups[g]] * (bc * eg)], axis=1).astype(BF16))
            for (g, d, h), x, bc, eg in zip(units, xs, bcs, egs)]
    for (g, d, h), gc, eg, sol, intra in zip(units, gcs, egs, sols, intras):
        col = d * nh + h
        rg = groups[g]
        glast_col = jnp.zeros_like(gc)
        for n in range(GDN_GC):
            r = n * c if d == 1 else (n + 1) * c - 1
            glast = gc[r:r + 1, :]
            glast_col = jnp.where(rowc == n, glast, glast_col)
            gl_ref[0, g, col * GDN_GC + n:col * GDN_GC + n + 1, :] = jnp.broadcast_to(jnp.exp(glast), (1, LANES))
        u_ref[0, col, rg, :] = sol[:, :GDN_DV]
        w = sol[:, GDN_DV:].astype(BF16)
        qd = (qs[h][rg] * eg).astype(BF16)
        for n in range(GDN_GC):
            base = 2 * (g * gs + n * c)
            wq_ref[0, col, base:base + c, :] = w[n * c:(n + 1) * c]
            wq_ref[0, col, base + c:base + 2 * c, :] = qd[n * c:(n + 1) * c]
        kd_ref[0, col, rg, :] = (ks[h][rg] * jnp.exp(glast_col - gc)).astype(BF16)
        in_ref[0, col, rg, :] = jnp.concatenate([intra[n * c:(n + 1) * c, n * c:(n + 1) * c] for n in range(GDN_GC)],
                                                axis=0).astype(BF16)


def _gdnchunk(proj, abt, conv_w, a_log, dt_bias):
    bsz, t, _ = proj.shape
    ts, nu = GDN_TS, GDN_UNITS
    nab = abt.shape[1]
    pad = lambda r: jnp.pad(r.reshape(1, -1), ((0, 0), (0, LANES - r.size)))
    alr, dtr = pad(a_log), pad(dt_bias)
    alc, dtc = a_log.reshape(-1, 1), dt_bias.reshape(-1, 1)
    small = lambda shape: pl.BlockSpec(shape, lambda b, i: (0, 0))
    unit = lambda rows, w: pl.BlockSpec((1, nu, rows, w), lambda b, i: (b, 0, i, 0))
    return pl.pallas_call(
        _gdnchunk_kernel,
        grid=(bsz, t // ts),
        in_specs=_halo_specs(ts, t, GDN_QKV, COL_GQKV // GDN_QKV) + [
            small((3, GDN_QKV)),
            pl.BlockSpec((1, ts, LANES), lambda b, i: (b, i, COL_AB // LANES)),
            pl.BlockSpec((1, nab, ts), lambda b, i: (b, 0, i)),
            small((1, LANES)), small((1, LANES)), small((nu, 1)), small((nu, 1))],
        out_specs=[unit(ts, GDN_DV), unit(2 * ts, GDN_DK), unit(ts, GDN_DK), unit(ts, GDN_CHUNK),
                   pl.BlockSpec((1, GDN_NG, nu * GDN_GC, LANES), lambda b, i: (b, i, 0, 0))],
        out_shape=[jax.ShapeDtypeStruct((bsz, nu, t, GDN_DV), F32),
                   jax.ShapeDtypeStruct((bsz, nu, 2 * t, GDN_DK), BF16),
                   jax.ShapeDtypeStruct((bsz, nu, t, GDN_DK), BF16),
                   jax.ShapeDtypeStruct((bsz, nu, t, GDN_CHUNK), BF16),
                   jax.ShapeDtypeStruct((bsz, t // GDN_GS, nu * GDN_GC, LANES), F32)],
        compiler_params=_cparams(("parallel", "parallel")),
        name="gdnchunk",
    )(proj, proj, proj, conv_w, proj, abt, alr, dtr, alc, dtc)


GDN_SB = 8
GDN_ST = GDN_SB * GDN_CHUNK


def _gdnscan_kernel(uf, ub, wqf, wqb, kdf, kdb, inf, inb, glf, glb, of_ref, ob_ref, s_ref):
    i = pl.program_id(0)

    @pl.when(i == 0)
    def _():
        s_ref[...] = jnp.zeros_like(s_ref)

    bsz = uf.shape[0]
    c, nh = GDN_CHUNK, GDN_HEADS
    chains = [(b, d, h) for b in range(bsz) for d in range(2) for h in range(nh)]
    for step in range(GDN_SB):
        zs, vns = [], []
        for b, d, h in chains:
            cc = step if d == 0 else GDN_SB - 1 - step
            wq = (wqf, wqb)[d]
            st = s_ref[(b * 2 + d) * nh + h]
            zs.append(_mm(wq[b, h, 2 * cc * c:(2 * cc + 2) * c, :], st.astype(BF16)))
        for (b, d, h), z in zip(chains, zs):
            cc = step if d == 0 else GDN_SB - 1 - step
            u = (uf, ub)[d]
            vns.append((u[b, h, cc * c:(cc + 1) * c, :] - z[:c]).astype(BF16))
        for (b, d, h), z, vn in zip(chains, zs, vns):
            cc = step if d == 0 else GDN_SB - 1 - step
            rows = slice(cc * c, (cc + 1) * c)
            intra = (inf, inb)[d]
            o_ref = (of_ref, ob_ref)[d]
            o_ref[b, rows, h * GDN_DV:(h + 1) * GDN_DV] = z[c:] + _mm(intra[b, h, rows, :], vn)
        for (b, d, h), vn in zip(chains, vns):
            cc = step if d == 0 else GDN_SB - 1 - step
            rows = slice(cc * c, (cc + 1) * c)
            kd = (kdf, kdb)[d]
            gl = (glf, glb)[d]
            r = (d * nh + h) * GDN_GC + cc % GDN_GC
            sidx = (b * 2 + d) * nh + h
            s_ref[sidx] = s_ref[sidx] * gl[b, cc // GDN_GC, r:r + 1, :] + _tn(kd[b, h, rows, :], vn)


def _gdnscan(u, wq, kd, intra, gl):
    bsz, nu, t, _ = u.shape
    nh = GDN_HEADS
    st = GDN_ST
    nb = t // st
    ngs = st // GDN_GS

    def unit(rows, w, d):
        if d == 0:
            return pl.BlockSpec((bsz, nh, rows, w), lambda i: (0, 0, i, 0))
        return pl.BlockSpec((bsz, nh, rows, w), lambda i: (0, 1, nb - 1 - i, 0))

    glspec = lambda d: pl.BlockSpec((bsz, ngs, nu * GDN_GC, LANES),
                                    (lambda i: (0, i, 0, 0)) if d == 0 else (lambda i: (0, nb - 1 - i, 0, 0)))
    return pl.pallas_call(
        _gdnscan_kernel,
        grid=(nb,),
        in_specs=[unit(st, GDN_DV, 0), unit(st, GDN_DV, 1), unit(2 * st, GDN_DK, 0), unit(2 * st, GDN_DK, 1),
                  unit(st, GDN_DK, 0), unit(st, GDN_DK, 1), unit(st, GDN_CHUNK, 0), unit(st, GDN_CHUNK, 1),
                  glspec(0), glspec(1)],
        out_specs=[pl.BlockSpec((bsz, st, GDN_W), lambda i: (0, i, 0)),
                   pl.BlockSpec((bsz, st, GDN_W), lambda i: (0, nb - 1 - i, 0))],
        out_shape=[jax.ShapeDtypeStruct((bsz, t, GDN_W), F32), jax.ShapeDtypeStruct((bsz, t, GDN_W), F32)],
        scratch_shapes=[pltpu.VMEM((bsz * nu, GDN_DK, GDN_DV), F32)],
        compiler_params=_cparams(("arbitrary",)),
        name="gdnscan",
    )(u, u, wq, wq, kd, kd, intra, intra, gl, gl)


def _outproj_kernel(x_ref, cm_ref, cp_ref, cn_ref, cw_ref, yn_ref, of_ref, ob_ref, z_ref, gnw_ref, wo_ref, g1_ref,
                    nw_ref, sc_ref, sh_ref, wr_ref, br_ref, xo_ref, h_ref, e_ref, g_ref, hist_ref):
    o = of_ref[0] + ob_ref[0]
    z = z_ref[0]
    parts = [_convmix_tile(cm_ref, cp_ref, cn_ref, cw_ref), yn_ref[0]]
    for h in range(GDN_HEADS):
        sl = slice(h * GDN_DV, (h + 1) * GDN_DV)
        oh = o[:, sl]
        oh = oh * lax.rsqrt(jnp.mean(oh * oh, axis=-1, keepdims=True) + EPS) * gnw_ref[...]
        parts.append(oh * _silu(z[:, sl]))
    mixed = _mm(jnp.concatenate(parts, axis=-1).astype(BF16), wo_ref[0])
    xn = x_ref[0] + g1_ref[0] * mixed
    xo_ref[0] = xn
    hf = _modnorm(xn, nw_ref[...], sc_ref[0], sh_ref[0])
    h_ref[0] = hf
    hf_hi = hf.astype(BF16)
    hf_lo = (hf - hf_hi.astype(F32)).astype(BF16)
    logits = (_mm(hf_hi, wr_ref[0]) + (_mm(hf_hi, wr_ref[1]) + _mm(hf_lo, wr_ref[0]))) + br_ref[...]
    lane = lax.broadcasted_iota(jnp.int32, logits.shape, 1)
    gl = jnp.where(lane < N_GROUPS, logits, NEG)
    gm = jnp.max(gl, axis=-1, keepdims=True)
    den = jnp.sum(jnp.exp(gl - gm), axis=-1, keepdims=True)
    grp = jnp.min(jnp.where(gl == gm, lane, LANES), axis=-1, keepdims=True)
    pg_top = 1.0 / den
    ex = lane - N_GROUPS
    in_grp = (ex >= grp * EXPERTS_PER_GROUP) & (ex < (grp + 1) * EXPERTS_PER_GROUP)
    el = jnp.where(in_grp, logits, NEG)
    m1 = jnp.max(el, axis=-1, keepdims=True)
    i1 = jnp.min(jnp.where(el == m1, lane, LANES), axis=-1, keepdims=True)
    el2 = jnp.where(lane == i1, NEG, el)
    m2 = jnp.max(el2, axis=-1, keepdims=True)
    i2 = jnp.min(jnp.where(el2 == m2, lane, LANES), axis=-1, keepdims=True)
    e2 = jnp.exp(m2 - m1)
    w1 = pg_top / (1.0 + e2)
    w2 = pg_top * e2 / (1.0 + e2)
    g_ref[0] = jnp.where(lane == 0, w1, jnp.where(lane == 1, w2, 0.0))
    oh1 = (lane == i1 - N_GROUPS).astype(F32)
    oh2 = (lane == i2 - N_GROUPS).astype(F32)
    both = oh1 + oh2
    tm = logits.shape[0]
    earlier = (lax.broadcasted_iota(jnp.int32, (tm, tm), 0) > lax.broadcasted_iota(jnp.int32, (tm, tm), 1))
    cnt = _mm(earlier.astype(BF16), both.astype(BF16))
    r1 = jnp.sum(cnt * oh1, axis=-1, keepdims=True).astype(jnp.int32)
    r2 = jnp.sum(cnt * oh2, axis=-1, keepdims=True).astype(jnp.int32)
    e_ref[0] = jnp.where(lane == 0, i1 - N_GROUPS, jnp.where(lane == 1, i2 - N_GROUPS,
                         jnp.where(lane == 2, r1, jnp.where(lane == 3, r2, 0))))
    hist_ref[0, 0] = jnp.broadcast_to(jnp.sum(both, axis=0, keepdims=True), (SUBLANES, LANES))


def _outproj(x, conv_w, yn, of, ob, proj, gnw, wo, g1, nw, sc, sh, wr, br, layer):
    bsz, t, d = x.shape
    tm = min(ROW_TILE, t)
    tok = lambda w: pl.BlockSpec((1, tm, w), lambda b, i: (b, i, 0))
    perb = pl.BlockSpec((1, 1, d), lambda b, i: (b, 0, 0))
    full = lambda shape: pl.BlockSpec(shape, lambda b, i: (0, 0))
    return pl.pallas_call(
        _outproj_kernel,
        grid=(bsz, t // tm),
        in_specs=[tok(d)] + _halo_specs(tm, t, 3 * CONV_W, COL_CONV // (3 * CONV_W)) + [
                  full((3, CONV_W)), tok(NA_W), tok(GDN_W), tok(GDN_W),
                  pl.BlockSpec((1, tm, GDN_W), lambda b, i: (b, i, COL_GZ // GDN_W)),
                  full((1, GDN_DV)), pl.BlockSpec((1, d, d), lambda b, i: (layer, 0, 0)), perb, full((1, d)), perb, perb,
                  pl.BlockSpec((2, d, LANES), lambda b, i: (0, 0, 0)), full((1, LANES))],
        out_specs=[tok(d), tok(d), tok(LANES), tok(LANES),
                   pl.BlockSpec((1, 1, SUBLANES, LANES), lambda b, i: (b, i, 0, 0))],
        out_shape=[jax.ShapeDtypeStruct((bsz, t, d), F32), jax.ShapeDtypeStruct((bsz, t, d), F32),
                   jax.ShapeDtypeStruct((bsz, t, LANES), jnp.int32), jax.ShapeDtypeStruct((bsz, t, LANES), F32),
                   jax.ShapeDtypeStruct((bsz, t // tm, SUBLANES, LANES), F32)],
        compiler_params=_cparams(("parallel", "parallel")),
        name="outproj",
    )(x, proj, proj, proj, conv_w, yn, of, ob, proj, gnw, wo, g1, nw, sc, sh, wr, br)


MOE_TM = 256


PIECE = SUBLANES
SORT_ROWS = TOP_K * ROW_TILE + N_EXPERTS * PIECE
PIECE_UNITS = (4, 2, 1)
PIECE_SLOTS = (SORT_ROWS // (PIECE_UNITS[0] * PIECE), N_EXPERTS, N_EXPERTS)
PIECE_TABLE = 2 * sum(PIECE_SLOTS)


def _piece_loops(tbl_ref, cnt_ref, tile, fn):
    off = 0
    for c, (units, slots) in enumerate(zip(PIECE_UNITS, PIECE_SLOTS)):
        def body(g, carry, off=off, units=units, slots=slots):
            fn(pl.multiple_of(tbl_ref[0, 0, off + g] * PIECE, PIECE),
               pl.multiple_of(tbl_ref[0, 0, off + slots + g] * PIECE, PIECE), units * PIECE)
            return carry
        lax.fori_loop(0, cnt_ref[tile * len(PIECE_UNITS) + c], body, 0)
        off += 2 * slots


def _dispatch_kernel(lb_ref, nu_ref, np_ref, tbl_ref, tprev_ref, e_ref, a8_ref, h_ref, xs_out, zbuf, sbuf, sem, zsem):
    i = pl.program_id(0)
    nt = pl.num_programs(0)
    tm = h_ref.shape[0]
    bm = zbuf.shape[0]
    nblk = xs_out.shape[0] // bm
    slot = i % 2

    def zero_copy(blk):
        return pltpu.make_async_copy(zbuf, xs_out.at[pl.ds(pl.multiple_of(blk * bm, bm), bm), :], zsem)

    @pl.when(i == 0)
    def _():
        zbuf[...] = jnp.zeros_like(zbuf)
        for e in range(N_EXPERTS):
            @pl.when(lb_ref[e] >= 0)
            def _():
                zero_copy(lb_ref[e]).start()

        def start_trailing(j, carry):
            zero_copy(j).start()
            return carry

        def wait_one(j, carry):
            zero_copy(0).wait()
            return carry

        lax.fori_loop(nu_ref[0], nblk, start_trailing, 0)
        lax.fori_loop(nu_ref[0], nblk, wait_one, 0)
        for e in range(N_EXPERTS):
            @pl.when(lb_ref[e] >= 0)
            def _():
                zero_copy(0).wait()

    e = e_ref[...]
    lane = lax.broadcasted_iota(jnp.int32, e.shape, 1)
    a8 = a8_ref[0, 0:1, :]
    rows = []
    eye = lax.broadcasted_iota(jnp.int32, (tm, tm), 0) == lax.broadcasted_iota(jnp.int32, (tm, tm), 1)
    for k in range(TOP_K):
        col = (jnp.sum(jnp.where(lane == e[:, k:k + 1], a8, 0.0), axis=-1, keepdims=True)
               + e[:, TOP_K + k:TOP_K + k + 1].astype(F32))
        rows.append(jnp.sum(jnp.where(eye, col, 0.0), axis=0, keepdims=True))
    j = lax.broadcasted_iota(jnp.int32, (SORT_ROWS, tm), 0).astype(F32)
    perm = ((j == rows[0]) | (j == rows[1])).astype(BF16)
    hb = h_ref[...].astype(BF16)

    def piece(s, wait):
        def fn(src, dst, nrows):
            cp = pltpu.make_async_copy(sbuf.at[s, pl.ds(src, nrows), :], xs_out.at[pl.ds(dst, nrows), :], sem.at[s])
            cp.wait() if wait else cp.start()
        return fn

    sbuf[slot] = _mm(perm, hb)
    _piece_loops(tbl_ref, np_ref, i, piece(slot, False))

    @pl.when(i > 0)
    def _():
        _piece_loops(tprev_ref, np_ref, jnp.maximum(i - 1, 0), piece(1 - slot, True))

    @pl.when(i == nt - 1)
    def _():
        _piece_loops(tbl_ref, np_ref, i, piece(slot, True))


def _dispatch(hf, eidx, a8, ptab, npieces, last_blk, n_used, padded_rows):
    n, d = hf.shape
    tm = ROW_TILE
    nt = n // tm
    grid_spec = pltpu.PrefetchScalarGridSpec(
        num_scalar_prefetch=3,
        grid=(nt,),
        in_specs=[pl.BlockSpec((1, 1, PIECE_TABLE), lambda i, lb, nu, npc: (i, 0, 0), memory_space=pltpu.SMEM),
                  pl.BlockSpec((1, 1, PIECE_TABLE), lambda i, lb, nu, npc: (jnp.maximum(i - 1, 0), 0, 0),
                               memory_space=pltpu.SMEM),
                  pl.BlockSpec((tm, LANES), lambda i, lb, nu, npc: (i, 0)),
                  pl.BlockSpec((1, SUBLANES, LANES), lambda i, lb, nu, npc: (i, 0, 0)),
                  pl.BlockSpec((tm, d), lambda i, lb, nu, npc: (i, 0))],
        out_specs=pl.BlockSpec(memory_space=pl.ANY),
        scratch_shapes=[pltpu.VMEM((MOE_BM, d), F32), pltpu.VMEM((2, SORT_ROWS, d), F32),
                        pltpu.SemaphoreType.DMA((2,)), pltpu.SemaphoreType.DMA(())])
    return pl.pallas_call(
        _dispatch_kernel,
        grid_spec=grid_spec,
        out_shape=jax.ShapeDtypeStruct((padded_rows, d), F32),
        compiler_params=_cparams(("arbitrary",)),
        name="dispatch",
    )(last_blk, n_used, npieces, ptab, ptab, eidx, a8, hf)


def _experts_kernel(be_ref, nu_ref, nx_ref, sg_ref, x_ref, w1_hbm, w3_hbm, w2_hbm, o_ref,
                    w1f, w3f, w2f, w1b, w3b, w2b, sem, *, layer):
    i = pl.program_id(0)
    used = nu_ref[0]

    def fetch(e, s):
        return [pltpu.make_async_copy(w_hbm.at[layer, e], wf.at[s], sem.at[s])
                for w_hbm, wf in ((w1_hbm, w1f), (w3_hbm, w3f), (w2_hbm, w2f))]

    first = ((i == 0) | (be_ref[i] != be_ref[jnp.maximum(i - 1, 0)])) & (i < used)

    @pl.when(first)
    def _():
        s = sg_ref[i] % 2

        @pl.when(i == 0)
        def _():
            for cp in fetch(be_ref[0], 0):
                cp.start()

        for cp in fetch(be_ref[i], s):
            cp.wait()
        w1b[...] = w1f[s].astype(BF16)
        w3b[...] = w3f[s].astype(BF16)
        w2b[...] = w2f[s].astype(BF16)

        @pl.when(nx_ref[i] >= 0)
        def _():
            for cp in fetch(nx_ref[i], 1 - s):
                cp.start()

    @pl.when(i < used)
    def _():
        xb = x_ref[...].astype(BF16)
        act = (_silu(_mm(xb, w1b[...])) * _mm(xb, w3b[...])).astype(BF16)
        o_ref[...] = _mm(act, w2b[...])

    @pl.when(i >= used)
    def _():
        o_ref[...] = jnp.zeros_like(o_ref)


def _experts(xs, block_e, n_used, next_e, seg_id, w1, w3, w2, layer):
    padded_rows, d = xs.shape
    bm = MOE_BM
    nblk = padded_rows // bm
    de = w1.shape[-1]
    grid_spec = pltpu.PrefetchScalarGridSpec(
        num_scalar_prefetch=4,
        grid=(nblk,),
        in_specs=[pl.BlockSpec((bm, d), lambda i, be, nu, nx, sg: (jnp.maximum(jnp.minimum(i, nu[0] - 1), 0), 0)),
                  pl.BlockSpec(memory_space=pl.ANY), pl.BlockSpec(memory_space=pl.ANY),
                  pl.BlockSpec(memory_space=pl.ANY)],
        out_specs=pl.BlockSpec((bm, d), lambda i, be, nu, nx, sg: (i, 0)),
        scratch_shapes=[pltpu.VMEM((2, d, de), F32), pltpu.VMEM((2, d, de), F32), pltpu.VMEM((2, de, d), F32),
                        pltpu.VMEM((d, de), BF16), pltpu.VMEM((d, de), BF16), pltpu.VMEM((de, d), BF16),
                        pltpu.SemaphoreType.DMA((2,))])

    def experts_kernel(*refs):
        _experts_kernel(*refs, layer=layer)

    return pl.pallas_call(
        experts_kernel,
        grid_spec=grid_spec,
        out_shape=jax.ShapeDtypeStruct((padded_rows, d), F32),
        compiler_params=_cparams(("arbitrary",)),
        name="experts",
    )(block_e, n_used, next_e, seg_id, xs, w1, w3, w2)


def _gather_rows(idx_ref, ys_hbm, ybuf, sem, slot, lo, hi):
    def body(r, carry):
        pltpu.make_async_copy(ys_hbm.at[pl.ds(idx_ref[0, 0, r], 1), :], ybuf.at[slot, pl.ds(r, 1), :],
                              sem.at[slot]).start()
        return carry
    lax.fori_loop(lo, hi, body, 0, unroll=8)


def _combine_rows(pc_ref, x_ref, gt_ref, g2_ref, ys_hbm, ybuf, sem):
    i = pl.program_id(0)
    tm = x_ref.shape[0]
    slot = i % 2

    @pl.when(i == 0)
    def _():
        _gather_rows(pc_ref, ys_hbm, ybuf, sem, 0, 0, TOP_K * tm)

    pltpu.make_async_copy(ys_hbm.at[pl.ds(0, TOP_K * tm), :], ybuf.at[slot], sem.at[slot]).wait()
    gt = gt_ref[...]
    y = gt[:, 0:1] * ybuf[slot, 0:tm, :] + gt[:, 1:2] * ybuf[slot, tm:2 * tm, :]
    return x_ref[...] + g2_ref[0] * y


def _combine_final_kernel(np_ref, dc_ref, dn_ref, e_ref, a8_ref, x_ref, gt_ref, g2_ref, fw_ref, ys_hbm, o_ref,
                          ybuf, sem):
    i = pl.program_id(0)
    nt = pl.num_programs(0)
    tm = x_ref.shape[0]
    slot = i % 2

    def piece(s, wait):
        def fn(loc, glob, nrows):
            cp = pltpu.make_async_copy(ys_hbm.at[pl.ds(glob, nrows), :], ybuf.at[s, pl.ds(loc, nrows), :], sem.at[s])
            cp.wait() if wait else cp.start()
        return fn

    @pl.when(i == 0)
    def _():
        ybuf[...] = jnp.zeros_like(ybuf)
        _piece_loops(dc_ref, np_ref, 0, piece(0, False))

    @pl.when(i + 1 < nt)
    def _():
        _piece_loops(dn_ref, np_ref, jnp.minimum(i + 1, nt - 1), piece(1 - slot, False))

    _piece_loops(dc_ref, np_ref, i, piece(slot, True))
    e = e_ref[...]
    lane = lax.broadcasted_iota(jnp.int32, e.shape, 1)
    a8 = a8_ref[0, 0:1, :]
    yb = ybuf[slot].astype(BF16)
    j = lax.broadcasted_iota(jnp.int32, (tm, SORT_ROWS), 1).astype(F32)
    gt = gt_ref[...]
    y = jnp.zeros((tm, x_ref.shape[1]), F32)
    for k in range(TOP_K):
        row = (jnp.sum(jnp.where(lane == e[:, k:k + 1], a8, 0.0), axis=-1, keepdims=True)
               + e[:, TOP_K + k:TOP_K + k + 1].astype(F32))
        y = y + gt[:, k:k + 1] * _mm((j == row).astype(BF16), yb)
    xn = x_ref[...] + g2_ref[0] * y
    ms = jnp.mean(xn * xn, axis=-1, keepdims=True)
    o_ref[...] = xn * lax.rsqrt(ms + EPS) * fw_ref[...]


PROJ_CHUNK = 512


def _combine_inproj_kernel(pc_ref, pn_ref, x_ref, gt_ref, g2_ref, nw_ref, sc_ref, sh_ref, w_ref, wabt_ref, ys_hbm,
                           xo_ref, o_ref, ot_ref, ybuf, sem):
    i = pl.program_id(0)
    tm = x_ref.shape[0]
    xn = _combine_rows(pc_ref, x_ref, gt_ref, g2_ref, ys_hbm, ybuf, sem)
    xo_ref[...] = xn
    h = _modnorm(xn, nw_ref[...], sc_ref[0], sh_ref[0]).astype(BF16)
    ot_ref[0] = _nt(wabt_ref[0], h)
    ncol = o_ref.shape[2]
    starts = list(range(0, ncol, PROJ_CHUNK))
    per = TOP_K * tm // len(starts)
    nxt = 1 - i % 2
    for j, c0 in enumerate(starts):
        for r in range(j * per, TOP_K * tm if j == len(starts) - 1 else (j + 1) * per):
            pltpu.make_async_copy(ys_hbm.at[pl.ds(pn_ref[0, 0, r], 1), :], ybuf.at[nxt, pl.ds(r, 1), :],
                                  sem.at[nxt]).start()
        c1 = min(c0 + PROJ_CHUNK, ncol)
        o_ref[0, :, c0:c1] = _mm(h, w_ref[0, :, c0:c1])

    @pl.when(i + 1 == pl.num_programs(0))
    def _():
        pltpu.make_async_copy(ys_hbm.at[pl.ds(0, TOP_K * tm), :], ybuf.at[nxt], sem.at[nxt]).wait()


def _combine_specs(n, d, t):
    tm = MOE_TM
    nt = n // tm
    per_b = t // tm
    specs = [pl.BlockSpec((1, 1, TOP_K * tm), lambda i: (i, 0, 0), memory_space=pltpu.SMEM),
             pl.BlockSpec((1, 1, TOP_K * tm), lambda i: (jnp.minimum(i + 1, nt - 1), 0, 0), memory_space=pltpu.SMEM),
             pl.BlockSpec((tm, d), lambda i: (i, 0)),
             pl.BlockSpec((tm, LANES), lambda i: (i, 0)),
             pl.BlockSpec((1, 1, d), lambda i: (i // per_b, 0, 0))]
    scratch = [pltpu.VMEM((2, TOP_K * tm, d), F32), pltpu.SemaphoreType.DMA((2,))]
    return tm, nt, per_b, specs, scratch


def _combine_final(x2, eidx, gates, a8, ptab, npieces, g2, fw, ys, t):
    n, d = x2.shape
    tm = ROW_TILE
    nt = n // tm
    per_b = t // tm
    grid_spec = pltpu.PrefetchScalarGridSpec(
        num_scalar_prefetch=1,
        grid=(nt,),
        in_specs=[pl.BlockSpec((1, 1, PIECE_TABLE), lambda i, npc: (i, 0, 0), memory_space=pltpu.SMEM),
                  pl.BlockSpec((1, 1, PIECE_TABLE), lambda i, npc: (jnp.minimum(i + 1, nt - 1), 0, 0),
                               memory_space=pltpu.SMEM),
                  pl.BlockSpec((tm, LANES), lambda i, npc: (i, 0)),
                  pl.BlockSpec((1, SUBLANES, LANES), lambda i, npc: (i, 0, 0)),
                  pl.BlockSpec((tm, d), lambda i, npc: (i, 0)),
                  pl.BlockSpec((tm, LANES), lambda i, npc: (i, 0)),
                  pl.BlockSpec((1, 1, d), lambda i, npc: (i // per_b, 0, 0)),
                  pl.BlockSpec((1, d), lambda i, npc: (0, 0)),
                  pl.BlockSpec(memory_space=pl.ANY)],
        out_specs=pl.BlockSpec((tm, d), lambda i, npc: (i, 0)),
        scratch_shapes=[pltpu.VMEM((2, SORT_ROWS, d), F32), pltpu.SemaphoreType.DMA((2,))])
    return pl.pallas_call(
        _combine_final_kernel,
        grid_spec=grid_spec,
        out_shape=jax.ShapeDtypeStruct((n, d), F32),
        compiler_params=_cparams(("arbitrary",)),
        name="combine",
    )(npieces, ptab, ptab, eidx, a8, x2, gates, g2, fw, ys)


def _combine_inproj(x2, gates, g2, ys, pos3, t, nw, sc, sh, w_pad, wabt, layer):
    n, d = x2.shape
    bsz = n // t
    nab = wabt.shape[1]
    tm, nt, per_b, specs, scratch = _combine_specs(n, d, t)
    perb = pl.BlockSpec((1, 1, d), lambda i: (i // per_b, 0, 0))
    return pl.pallas_call(
        _combine_inproj_kernel,
        grid=(nt,),
        in_specs=specs + [pl.BlockSpec((1, d), lambda i: (0, 0)), perb, perb,
                          pl.BlockSpec((1, d, D_IN_PAD), lambda i: (layer, 0, 0)),
                          pl.BlockSpec((1, nab, d), lambda i: (layer, 0, 0)),
                          pl.BlockSpec(memory_space=pl.ANY)],
        out_specs=[pl.BlockSpec((tm, d), lambda i: (i, 0)),
                   pl.BlockSpec((1, tm, D_IN_PAD), lambda i: (i // per_b, i % per_b, 0)),
                   pl.BlockSpec((1, nab, tm), lambda i: (i // per_b, 0, i % per_b))],
        out_shape=[jax.ShapeDtypeStruct((n, d), F32), jax.ShapeDtypeStruct((bsz, t, D_IN_PAD), F32),
                   jax.ShapeDtypeStruct((bsz, nab, t), F32)],
        scratch_shapes=scratch,
        compiler_params=_cparams(("arbitrary",)),
        name="combine_inproj",
    )(pos3, pos3, x2, gates, g2, nw, sc, sh, w_pad, wabt, ys)


def _moe_plan(eidx, hist):
    n = eidx.shape[0]
    bm = MOE_BM
    ntile = n // ROW_TILE
    experts = jnp.arange(N_EXPERTS, dtype=jnp.int32)
    hist = hist[:, 0, :N_EXPERTS].astype(jnp.int32)
    cnt = (hist + PIECE - 1) // PIECE * PIECE
    seg_start = jnp.cumsum(cnt, axis=1) - cnt
    sizes = jnp.sum(cnt, axis=0)
    base = jnp.cumsum(cnt, axis=0) - cnt
    padded = (sizes + bm - 1) // bm * bm
    pad_end = jnp.cumsum(padded)
    tbl = (pad_end - padded)[None, :] + base
    e = eidx[:, :TOP_K].reshape(ntile, ROW_TILE, TOP_K)
    onehot = e[..., None] == experts
    pos = jnp.sum(jnp.where(onehot, tbl[:, None, None, :], 0), axis=-1).reshape(n, TOP_K) + eidx[:, TOP_K:2 * TOP_K]
    left, tables, counts = cnt // PIECE, [], []
    done = jnp.zeros_like(left)
    for units, slots in zip(PIECE_UNITS, PIECE_SLOTS):
        num = left // units
        end = jnp.cumsum(num, axis=1)
        g = jnp.arange(slots, dtype=jnp.int32)
        e_of_g = jnp.minimum(jnp.sum(end[:, None, :] <= g[None, :, None], axis=-1), N_EXPERTS - 1)
        sel = e_of_g[..., None] == experts
        pick = lambda a, sel=sel: jnp.sum(jnp.where(sel, a[:, None, :], 0), axis=-1)
        step = units * (g[None, :] - pick(end - num))
        valid = g[None, :] < end[:, -1:]
        tables += [jnp.where(valid, pick(seg_start // PIECE + done) + step, 0),
                   jnp.where(valid, pick(tbl // PIECE + done) + step, 0)]
        counts.append(end[:, -1])
        done = done + num * units
        left = left - num * units
    ptab = jnp.concatenate(tables, axis=1).astype(jnp.int32).reshape(ntile, 1, PIECE_TABLE)
    npieces = jnp.stack(counts, axis=1).astype(jnp.int32).reshape(-1)
    seg8 = jnp.broadcast_to(jnp.pad(seg_start.astype(F32), ((0, 0), (0, LANES - N_EXPERTS)))[:, None, :],
                            (ntile, SUBLANES, LANES))
    padded_rows = (n * TOP_K + ntile * N_EXPERTS * (PIECE - 1) + N_EXPERTS * (bm - 1) + bm - 1) // bm * bm
    n_blocks = padded_rows // bm
    blk_start = jnp.arange(n_blocks, dtype=jnp.int32) * bm
    block_e = jnp.minimum(jnp.sum(pad_end[None, :] <= blk_start[:, None], axis=1), N_EXPERTS - 1).astype(jnp.int32)
    n_used = (pad_end[-1] // bm).astype(jnp.int32).reshape(1)
    last_blk = jnp.where(padded > 0, pad_end // bm - 1, -1).astype(jnp.int32)
    blk = jnp.arange(n_blocks, dtype=jnp.int32)
    change = (blk == 0) | (block_e != jnp.roll(block_e, 1))
    seg_id = (jnp.cumsum(change.astype(jnp.int32)) - 1).astype(jnp.int32)
    later = (blk[None, :] > blk[:, None]) & (block_e[None, :] != block_e[:, None]) & (blk[None, :] < n_used[0])
    nxt = jnp.min(jnp.where(later, blk[None, :], n_blocks), axis=1)
    next_e = jnp.where(nxt < n_blocks, block_e[jnp.minimum(nxt, n_blocks - 1)], -1).astype(jnp.int32)
    nt = n // MOE_TM
    pos3 = jnp.transpose(pos.astype(jnp.int32).reshape(nt, MOE_TM, TOP_K), (0, 2, 1)).reshape(nt, 1, TOP_K * MOE_TM)
    return pos3, seg8, ptab, npieces, block_e, n_used, next_e, seg_id, last_blk, padded_rows


def kernel(x, c, norm_mix_w, norm_ffn_w, w_ada, b_ada, w_in, conv_a_w, na_rpb, gdn_conv_w, gdn_a_log, gdn_dt_bias,
           gdn_norm_w, w_out, router_group_w, router_group_b, router_expert_w, router_expert_b, expert_w1,
           expert_w3, expert_w2, final_norm_w):
    bsz, t, d = x.shape
    depth = w_ada.shape[0]
    n = bsz * t
    assert t % ROW_TILE == 0 and (t // GRID_W) % NA_KH == 0, "sequence length must be a multiple of 512"
    mod = _ada(c, w_ada, b_ada)
    w_in_pad = jnp.pad(w_in, ((0, 0), (0, 0), (0, D_IN_PAD - D_IN))).astype(BF16)
    wabt = jnp.transpose(w_in[:, :, COL_AB:], (0, 2, 1)).astype(BF16)
    w_out_b = w_out.astype(BF16)
    wr = jnp.pad(jnp.concatenate([router_group_w, router_expert_w], axis=-1),
                 ((0, 0), (0, 0), (0, LANES - N_GROUPS - N_EXPERTS)))
    wr_hi = wr.astype(BF16)
    wr = jnp.stack([wr_hi, (wr - wr_hi.astype(F32)).astype(BF16)], axis=1)
    br = jnp.pad(jnp.concatenate([router_group_b, router_expert_b], axis=-1),
                 ((0, 0), (0, LANES - N_GROUPS - N_EXPERTS)))
    fw = final_norm_w.reshape(1, d)
    na_bias = _na_bias_table(na_rpb, t // GRID_W)
    mods = [[mod[l, :, j * d:(j + 1) * d].reshape(bsz, 1, d) for j in range(6)] for l in range(depth)]
    proj, abt = _inproj(x, norm_mix_w[0].reshape(1, d), mods[0][1], mods[0][0], w_in_pad, wabt, 0)
    for l in range(depth):
        sh1, sc1, g1, sh2, sc2, g2 = mods[l]
        y_na = _na(proj, na_bias, l)
        o_f, o_b = _gdnscan(*_gdnchunk(proj, abt, gdn_conv_w[l], gdn_a_log[l], gdn_dt_bias[l]))
        x, hf, eidx, gates, hist = _outproj(x, conv_a_w[l], y_na, o_f, o_b, proj, gdn_norm_w[l].reshape(1, GDN_DV),
                                            w_out_b, g1, norm_ffn_w[l].reshape(1, d), sc2, sh2, wr[l],
                                            br[l].reshape(1, LANES), l)
        eidx = eidx.reshape(n, LANES)
        pos3, seg8, ptab, npieces, block_e, n_used, next_e, seg_id, last_blk, padded_rows = _moe_plan(
            eidx, hist.reshape(-1, SUBLANES, LANES))
        xs = _dispatch(hf.reshape(n, d), eidx, seg8, ptab, npieces, last_blk, n_used, padded_rows)
        ys = _experts(xs, block_e, n_used, next_e, seg_id, expert_w1, expert_w3, expert_w2, l)
        x2, gt = x.reshape(n, d), gates.reshape(n, LANES)
        if l + 1 < depth:
            x2, proj, abt = _combine_inproj(x2, gt, g2, ys, pos3, t, norm_mix_w[l + 1].reshape(1, d), mods[l + 1][1],
                                            mods[l + 1][0], w_in_pad, wabt, l + 1)
            x = x2.reshape(bsz, t, d)
        else:
            x = _combine_final(x2, eidx, gt, seg8, ptab, npieces, g2, fw, ys, t).reshape(bsz, t, d)
    return x
```

```python
import jax
import jax.numpy as jnp
from jax import lax
from jax.experimental import pallas as pl
from jax.experimental.pallas import tpu as pltpu

F32 = jnp.float32
BF16 = jnp.bfloat16

EPS = 1e-6
GRID_W = 64
CONV_W = 256
NA_HEADS = 4
NA_DH = 64
NA_W = NA_HEADS * NA_DH
NA_KH = 8
NA_KW = 16
GDN_HEADS = 4
GDN_DK = 128
GDN_DV = 128
GDN_W = GDN_HEADS * GDN_DV
GDN_QKV = 2 * GDN_HEADS * GDN_DK + GDN_W
GDN_CHUNK = 64
N_GROUPS = 4
EXPERTS_PER_GROUP = 8
N_EXPERTS = N_GROUPS * EXPERTS_PER_GROUP
TOP_K = 2

COL_CONV = 0
COL_NA = 3 * CONV_W
COL_GQKV = COL_NA + 3 * NA_W
COL_GZ = COL_GQKV + GDN_QKV
COL_AB = COL_GZ + GDN_W
D_IN = COL_AB + 4 * GDN_HEADS
D_IN_PAD = COL_AB + 128
LANES = 128
SUBLANES = 8

NEG = -1e30
VMEM_LIMIT = 56 * 1024 * 1024

MOE_BM = 256
ROW_TILE = 512


def _cparams(sem):
    return pltpu.CompilerParams(dimension_semantics=sem, vmem_limit_bytes=VMEM_LIMIT)


def _silu(x):
    return x * (1.0 / (1.0 + jnp.exp(-x)))


def _nt(a, b):
    return lax.dot_general(a, b, (((1,), (1,)), ((), ())), preferred_element_type=F32)


def _tn(a, b):
    return lax.dot_general(a, b, (((0,), (0,)), ((), ())), preferred_element_type=F32)


def _mm(a, b):
    return jnp.dot(a, b, preferred_element_type=F32)


def _ada_kernel(ct_ref, w_ref, b_ref, o_ref):
    w = w_ref[0]
    nb = ct_ref.shape[1]
    for r in range(nb):
        col = _silu(ct_ref[:, r:r + 1])
        o_ref[0, r:r + 1, :] = jnp.sum(w * col, axis=0, keepdims=True) + b_ref[0]


def _ada(c, w_ada, b_ada):
    depth, d, n6 = w_ada.shape
    bsz = c.shape[0]
    tn = 1536
    return pl.pallas_call(
        _ada_kernel,
        grid=(depth, n6 // tn),
        in_specs=[pl.BlockSpec((d, bsz), lambda l, j: (0, 0)),
                  pl.BlockSpec((1, d, tn), lambda l, j: (l, 0, j)),
                  pl.BlockSpec((1, 1, tn), lambda l, j: (l, 0, j))],
        out_specs=pl.BlockSpec((1, bsz, tn), lambda l, j: (l, 0, j)),
        out_shape=jax.ShapeDtypeStruct((depth, bsz, n6), F32),
        compiler_params=_cparams(("parallel", "parallel")),
        name="ada",
    )(c.T, w_ada, b_ada.reshape(depth, 1, n6))


def _modnorm(x, nw, sc, sh):
    ms = jnp.mean(x * x, axis=-1, keepdims=True)
    y = x * lax.rsqrt(ms + EPS)
    return (y * nw) * (1.0 + sc) + sh


def _project(h, w_ref, wabt_ref, o_ref, ot_ref):
    o_ref[0] = _mm(h, w_ref[0])
    ot_ref[0] = _nt(wabt_ref[0], h)


def _inproj_kernel(x_ref, nw_ref, sc_ref, sh_ref, w_ref, wabt_ref, o_ref, ot_ref):
    h = _modnorm(x_ref[0], nw_ref[...], sc_ref[0], sh_ref[0]).astype(BF16)
    _project(h, w_ref, wabt_ref, o_ref, ot_ref)


def _inproj(x, nw, sc, sh, w_pad, wabt, layer):
    bsz, t, d = x.shape
    tm = min(ROW_TILE, t)
    nab = wabt.shape[1]
    return pl.pallas_call(
        _inproj_kernel,
        grid=(bsz, t // tm),
        in_specs=[pl.BlockSpec((1, tm, d), lambda b, i: (b, i, 0)),
                  pl.BlockSpec((1, d), lambda b, i: (0, 0)),
                  pl.BlockSpec((1, 1, d), lambda b, i: (b, 0, 0)),
                  pl.BlockSpec((1, 1, d), lambda b, i: (b, 0, 0)),
                  pl.BlockSpec((1, d, D_IN_PAD), lambda b, i: (layer, 0, 0)),
                  pl.BlockSpec((1, nab, d), lambda b, i: (layer, 0, 0))],
        out_specs=[pl.BlockSpec((1, tm, D_IN_PAD), lambda b, i: (b, i, 0)),
                   pl.BlockSpec((1, nab, tm), lambda b, i: (b, 0, i))],
        out_shape=[jax.ShapeDtypeStruct((bsz, t, D_IN_PAD), F32),
                   jax.ShapeDtypeStruct((bsz, nab, t), F32)],
        compiler_params=_cparams(("parallel", "parallel")),
        name="inproj",
    )(x, nw, sc, sh, w_pad, wabt)


def _dwconv3(u, prev_row, next_row, w_ref):
    tt = u.shape[0]
    row = lax.broadcasted_iota(jnp.int32, u.shape, 0)
    dn = jnp.where(row == 0, prev_row, pltpu.roll(u, 1, axis=0))
    up = jnp.where(row == tt - 1, next_row, pltpu.roll(u, tt - 1, axis=0))
    return w_ref[0:1, :] * dn + w_ref[1:2, :] * u + w_ref[2:3, :] * up


def _halo_specs(tt, t, width, colblk):
    nsub = tt // SUBLANES
    last = t // SUBLANES - 1
    return [pl.BlockSpec((1, tt, width), lambda b, i: (b, i, colblk)),
            pl.BlockSpec((1, SUBLANES, width), lambda b, i: (b, jnp.maximum(i * nsub - 1, 0), colblk)),
            pl.BlockSpec((1, SUBLANES, width), lambda b, i: (b, jnp.minimum((i + 1) * nsub, last), colblk))]


def _convmix_tile(m_ref, p_ref, n_ref, w_ref):
    i = pl.program_id(1)
    nt = pl.num_programs(1)
    m = m_ref[0]
    u = m[:, CONV_W:2 * CONV_W] * m[:, 2 * CONV_W:]
    p = p_ref[0]
    n = n_ref[0]
    pu = p[SUBLANES - 1:SUBLANES, CONV_W:2 * CONV_W] * p[SUBLANES - 1:SUBLANES, 2 * CONV_W:]
    nu = n[0:1, CONV_W:2 * CONV_W] * n[0:1, 2 * CONV_W:]
    pu = jnp.where(i == 0, 0.0, pu)
    nu = jnp.where(i == nt - 1, 0.0, nu)
    return m[:, :CONV_W] * _dwconv3(u, pu, nu, w_ref)


NA_RB = NA_KH // 2
NA_TOK = NA_RB * GRID_W
NA_KEYS = 3 * NA_TOK


def _na_bias_table(rpb):
    col = jnp.arange(GRID_W)
    cstart = jnp.clip(col - NA_KW // 2, 0, GRID_W - NA_KW)
    kc = jnp.arange(GRID_W)
    valid = (kc[None, :] >= cstart[:, None]) & (kc[None, :] < cstart[:, None] + NA_KW)
    dc = kc[None, :] - col[:, None] + (NA_KW - 1)
    onehot = (dc[None] == jnp.arange(2 * NA_KW - 1)[:, None, None]) & valid[None]
    cols = jnp.einsum('lhrd,dck->lhcrk', rpb, onehot.astype(F32), precision=lax.Precision.HIGHEST)
    cols = jnp.where(valid[:, None, :], cols, NEG)
    lo = NA_KH - 1 - NA_RB
    blk = jnp.stack([cols[:, :, :, lo - j:lo - j + 3 * NA_RB] for j in range(NA_RB)], axis=2)
    return blk.reshape(rpb.shape[0], NA_HEADS, NA_TOK, NA_KEYS)


def _na_kernel(q_ref, kp_ref, kc_ref, kn_ref, vp_ref, vc_ref, vn_ref, bias_ref, o_ref, *, rows):
    i = pl.program_id(1)
    kbuf = jnp.concatenate([kp_ref[0], kc_ref[0], kn_ref[0]], axis=0).astype(BF16)
    vbuf = jnp.concatenate([vp_ref[0], vc_ref[0], vn_ref[0]], axis=0).astype(BF16)
    q = q_ref[0] * (NA_DH ** -0.5)
    head_of_lane = lax.broadcasted_iota(jnp.int32, (1, NA_W), 1) // NA_DH
    qrow = i * NA_RB + lax.broadcasted_iota(jnp.int32, (NA_TOK, NA_KEYS), 0) // GRID_W
    krow = (i - 1) * NA_RB + lax.broadcasted_iota(jnp.int32, (NA_TOK, NA_KEYS), 1) // GRID_W
    rs = jnp.clip(qrow - NA_KH // 2, 0, rows - NA_KH)
    row_mask = jnp.where((krow >= rs) & (krow < rs + NA_KH), 0.0, NEG)
    acc = jnp.zeros((NA_TOK, NA_W), F32)
    for h in range(NA_HEADS):
        mine = head_of_lane == h
        s = _nt(jnp.where(mine, q, 0.0).astype(BF16), kbuf) + (bias_ref[0, h] + row_mask)
        m = jnp.max(s, axis=-1, keepdims=True)
        p = jnp.exp(s - m)
        l = jnp.sum(p, axis=-1, keepdims=True)
        o = _mm((p * (1.0 / l)).astype(BF16), vbuf)
        acc = acc + jnp.where(mine, o, 0.0)
    o_ref[0] = acc


def _na(proj, bias_tbl, layer):
    bsz, t, _ = proj.shape
    rows = t // GRID_W
    nblk = rows // NA_RB
    qc, kc, vc = COL_NA // NA_W, COL_NA // NA_W + 1, COL_NA // NA_W + 2

    def spec(col, shift):
        return pl.BlockSpec((1, NA_TOK, NA_W), lambda b, i: (b, jnp.clip(i + shift, 0, nblk - 1), col))

    def na_kernel(*refs):
        _na_kernel(*refs, rows=rows)

    return pl.pallas_call(
        na_kernel,
        grid=(bsz, nblk),
        in_specs=[spec(qc, 0), spec(kc, -1), spec(kc, 0), spec(kc, 1), spec(vc, -1), spec(vc, 0), spec(vc, 1),
                  pl.BlockSpec((1, NA_HEADS, NA_TOK, NA_KEYS), lambda b, i: (layer, 0, 0, 0))],
        out_specs=pl.BlockSpec((1, NA_TOK, NA_W), lambda b, i: (b, i, 0)),
        out_shape=jax.ShapeDtypeStruct((bsz, t, NA_W), F32),
        compiler_params=_cparams(("parallel", "parallel")),
        name="na",
    )(proj, proj, proj, proj, proj, proj, proj, bias_tbl)


def _gdn_qkv(m_ref, p_ref, n_ref, w_ref):
    i = pl.program_id(1)
    nt = pl.num_programs(1)
    pu = jnp.where(i == 0, 0.0, p_ref[0, SUBLANES - 1:SUBLANES, :])
    nu = jnp.where(i == nt - 1, 0.0, n_ref[0, 0:1, :])
    c = _silu(_dwconv3(m_ref[0], pu, nu, w_ref))
    heads = []
    for hh in range(GDN_QKV // GDN_DK):
        xh = c[:, hh * GDN_DK:(hh + 1) * GDN_DK]
        if hh < 2 * GDN_HEADS:
            xh = xh * lax.rsqrt(jnp.sum(xh * xh, axis=-1, keepdims=True) + EPS)
            if hh < GDN_HEADS:
                xh = xh * (GDN_DK ** -0.5)
        heads.append(xh)
    return heads[:GDN_HEADS], heads[GDN_HEADS:2 * GDN_HEADS], heads[2 * GDN_HEADS:]


GDN_GS = 128
GDN_UNITS = 2 * GDN_HEADS
GDN_GC = GDN_GS // GDN_CHUNK
GDN_NG = 2
GDN_TS = GDN_NG * GDN_GS


def _softplus(x):
    return jnp.maximum(x, 0.0) + jnp.log1p(jnp.exp(-jnp.abs(x)))


def _seg_cumsum(x, axis, reverse):
    n = x.shape[axis]
    pos = lax.broadcasted_iota(jnp.int32, x.shape, axis) & (GDN_CHUNK - 1)
    s = 1
    while s < GDN_CHUNK:
        if reverse:
            x = x + jnp.where(pos < GDN_CHUNK - s, pltpu.roll(x, n - s, axis=axis), 0.0)
        else:
            x = x + jnp.where(pos >= s, pltpu.roll(x, s, axis=axis), 0.0)
        s *= 2
    return x


def _gdnchunk_kernel(m_ref, p_ref, n_ref, cw_ref, ab_ref, abt_ref, alr_ref, dtr_ref, alc_ref, dtc_ref,
                     u_ref, wq_ref, kd_ref, in_ref, gl_ref):
    gs, c, nh = GDN_GS, GDN_CHUNK, GDN_HEADS
    ri = lax.broadcasted_iota(jnp.int32, (gs, gs), 0)
    ci = lax.broadcasted_iota(jnp.int32, (gs, gs), 1)
    same = (ri // c) == (ci // c)
    eye = (ri == ci).astype(F32)
    rowc = lax.broadcasted_iota(jnp.int32, (gs, 1), 0) // c
    ab = ab_ref[0]
    abt = abt_ref[0]
    graw_c = -jnp.exp(alr_ref[...]) * _softplus(ab + dtr_ref[...])
    graw_r = -jnp.exp(alc_ref[...]) * _softplus(abt[0:2 * nh] + dtc_ref[...])
    beta_c = 1.0 / (1.0 + jnp.exp(-ab))
    g_col = [_seg_cumsum(graw_c, 0, False), _seg_cumsum(graw_c, 0, True)]
    g_row = [_seg_cumsum(graw_r, 1, False), _seg_cumsum(graw_r, 1, True)]
    incl = [same & (ri >= ci), same & (ri <= ci)]
    strict = [same & (ri > ci), same & (ri < ci)]

    qs, ks, vs = _gdn_qkv(m_ref, p_ref, n_ref, cw_ref)
    groups = [slice(g * gs, (g + 1) * gs) for g in range(GDN_NG)]
    grams = [[_nt(jnp.concatenate([q[rg], k[rg]], axis=0).astype(BF16), k[rg].astype(BF16))
              for q, k in zip(qs, ks)] for rg in groups]

    units = [(g, d, h) for g in range(GDN_NG) for d in range(2) for h in range(nh)]
    gcs, bcs, intras, xs, ps = [], [], [], [], []
    for g, d, h in units:
        col = d * nh + h
        rg = groups[g]
        gc = g_col[d][rg, col:col + 1]
        gr = g_row[d][col:col + 1, rg]
        bc = beta_c[rg, 2 * nh + col:2 * nh + col + 1]
        e_incl = jnp.exp(jnp.where(incl[d], gc - gr, NEG))
        a = grams[g][h][gs:] * bc * jnp.where(strict[d], e_incl, 0.0)
        gcs.append(gc), bcs.append(bc)
        intras.append(grams[g][h][:gs] * e_incl)
        xs.append(eye - a), ps.append(a)
    for _ in range(5):
        pbs = [p.astype(BF16) for p in ps]
        ps = [_mm(pb, pb) for pb in pbs]
        xs = [x + _mm(x.astype(BF16), p.astype(BF16)) for x, p in zip(xs, ps)]
    egs = [jnp.exp(gc) for gc in gcs]
    sols = [_mm(x.astype(BF16),
                jnp.concatenate([vs[h][groups[g]] * bc, ks[h][groups[g]] * (bc * eg)], axis=1).astype(BF16))
            for (g, d, h), x, bc, eg in zip(units, xs, bcs, egs)]
    for (g, d, h), gc, eg, sol, intra in zip(units, gcs, egs, sols, intras):
        col = d * nh + h
        rg = groups[g]
        glast_col = jnp.zeros_like(gc)
        for n in range(GDN_GC):
            r = n * c if d == 1 else (n + 1) * c - 1
            glast = gc[r:r + 1, :]
            glast_col = jnp.where(rowc == n, glast, glast_col)
            gl_ref[0, g, col * GDN_GC + n:col * GDN_GC + n + 1, :] = jnp.broadcast_to(jnp.exp(glast), (1, LANES))
        u_ref[0, col, rg, :] = sol[:, :GDN_DV]
        w = sol[:, GDN_DV:].astype(BF16)
        qd = (qs[h][rg] * eg).astype(BF16)
        for n in range(GDN_GC):
            base = 2 * (g * gs + n * c)
            wq_ref[0, col, base:base + c, :] = w[n * c:(n + 1) * c]
            wq_ref[0, col, base + c:base + 2 * c, :] = qd[n * c:(n + 1) * c]
        kd_ref[0, col, rg, :] = (ks[h][rg] * jnp.exp(glast_col - gc)).astype(BF16)
        in_ref[0, col, rg, :] = jnp.concatenate([intra[n * c:(n + 1) * c, n * c:(n + 1) * c] for n in range(GDN_GC)],
                                                axis=0).astype(BF16)


def _gdnchunk(proj, abt, conv_w, a_log, dt_bias):
    bsz, t, _ = proj.shape
    ts, nu = GDN_TS, GDN_UNITS
    nab = abt.shape[1]
    pad = lambda r: jnp.pad(r.reshape(1, -1), ((0, 0), (0, LANES - r.size)))
    alr, dtr = pad(a_log), pad(dt_bias)
    alc, dtc = a_log.reshape(-1, 1), dt_bias.reshape(-1, 1)
    small = lambda shape: pl.BlockSpec(shape, lambda b, i: (0, 0))
    unit = lambda rows, w: pl.BlockSpec((1, nu, rows, w), lambda b, i: (b, 0, i, 0))
    return pl.pallas_call(
        _gdnchunk_kernel,
        grid=(bsz, t // ts),
        in_specs=_halo_specs(ts, t, GDN_QKV, COL_GQKV // GDN_QKV) + [
            small((3, GDN_QKV)),
            pl.BlockSpec((1, ts, LANES), lambda b, i: (b, i, COL_AB // LANES)),
            pl.BlockSpec((1, nab, ts), lambda b, i: (b, 0, i)),
            small((1, LANES)), small((1, LANES)), small((nu, 1)), small((nu, 1))],
        out_specs=[unit(ts, GDN_DV), unit(2 * ts, GDN_DK), unit(ts, GDN_DK), unit(ts, GDN_CHUNK),
                   pl.BlockSpec((1, GDN_NG, nu * GDN_GC, LANES), lambda b, i: (b, i, 0, 0))],
        out_shape=[jax.ShapeDtypeStruct((bsz, nu, t, GDN_DV), F32),
                   jax.ShapeDtypeStruct((bsz, nu, 2 * t, GDN_DK), BF16),
                   jax.ShapeDtypeStruct((bsz, nu, t, GDN_DK), BF16),
                   jax.ShapeDtypeStruct((bsz, nu, t, GDN_CHUNK), BF16),
                   jax.ShapeDtypeStruct((bsz, t // GDN_GS, nu * GDN_GC, LANES), F32)],
        compiler_params=_cparams(("parallel", "parallel")),
        name="gdnchunk",
    )(proj, proj, proj, conv_w, proj, abt, alr, dtr, alc, dtc)


GDN_SB = 8
GDN_ST = GDN_SB * GDN_CHUNK


def _gdnscan_kernel(uf, ub, wqf, wqb, kdf, kdb, inf, inb, glf, glb, of_ref, ob_ref, s_ref):
    i = pl.program_id(0)

    @pl.when(i == 0)
    def _():
        s_ref[...] = jnp.zeros_like(s_ref)

    bsz = uf.shape[0]
    c, nh = GDN_CHUNK, GDN_HEADS
    chains = [(b, d, h) for b in range(bsz) for d in range(2) for h in range(nh)]
    for step in range(GDN_SB):
        zs, vns = [], []
        for b, d, h in chains:
            cc = step if d == 0 else GDN_SB - 1 - step
            wq = (wqf, wqb)[d]
            st = s_ref[(b * 2 + d) * nh + h]
            zs.append(_mm(wq[b, h, 2 * cc * c:(2 * cc + 2) * c, :], st.astype(BF16)))
        for (b, d, h), z in zip(chains, zs):
            cc = step if d == 0 else GDN_SB - 1 - step
            u = (uf, ub)[d]
            vns.append((u[b, h, cc * c:(cc + 1) * c, :] - z[:c]).astype(BF16))
        for (b, d, h), z, vn in zip(chains, zs, vns):
            cc = step if d == 0 else GDN_SB - 1 - step
            rows = slice(cc * c, (cc + 1) * c)
            intra = (inf, inb)[d]
            o_ref = (of_ref, ob_ref)[d]
            o_ref[b, rows, h * GDN_DV:(h + 1) * GDN_DV] = z[c:] + _mm(intra[b, h, rows, :], vn)
        for (b, d, h), vn in zip(chains, vns):
            cc = step if d == 0 else GDN_SB - 1 - step
            rows = slice(cc * c, (cc + 1) * c)
            kd = (kdf, kdb)[d]
            gl = (glf, glb)[d]
            r = (d * nh + h) * GDN_GC + cc % GDN_GC
            sidx = (b * 2 + d) * nh + h
            s_ref[sidx] = s_ref[sidx] * gl[b, cc // GDN_GC, r:r + 1, :] + _tn(kd[b, h, rows, :], vn)


def _gdnscan(u, wq, kd, intra, gl):
    bsz, nu, t, _ = u.shape
    nh = GDN_HEADS
    st = GDN_ST
    nb = t // st
    ngs = st // GDN_GS

    def unit(rows, w, d):
        if d == 0:
            return pl.BlockSpec((bsz, nh, rows, w), lambda i: (0, 0, i, 0))
        return pl.BlockSpec((bsz, nh, rows, w), lambda i: (0, 1, nb - 1 - i, 0))

    glspec = lambda d: pl.BlockSpec((bsz, ngs, nu * GDN_GC, LANES),
                                    (lambda i: (0, i, 0, 0)) if d == 0 else (lambda i: (0, nb - 1 - i, 0, 0)))
    return pl.pallas_call(
        _gdnscan_kernel,
        grid=(nb,),
        in_specs=[unit(st, GDN_DV, 0), unit(st, GDN_DV, 1), unit(2 * st, GDN_DK, 0), unit(2 * st, GDN_DK, 1),
                  unit(st, GDN_DK, 0), unit(st, GDN_DK, 1), unit(st, GDN_CHUNK, 0), unit(st, GDN_CHUNK, 1),
                  glspec(0), glspec(1)],
        out_specs=[pl.BlockSpec((bsz, st, GDN_W), lambda i: (0, i, 0)),
                   pl.BlockSpec((bsz, st, GDN_W), lambda i: (0, nb - 1 - i, 0))],
        out_shape=[jax.ShapeDtypeStruct((bsz, t, GDN_W), F32), jax.ShapeDtypeStruct((bsz, t, GDN_W), F32)],
        scratch_shapes=[pltpu.VMEM((bsz * nu, GDN_DK, GDN_DV), F32)],
        compiler_params=_cparams(("arbitrary",)),
        name="gdnscan",
    )(u, u, wq, wq, kd, kd, intra, intra, gl, gl)


def _outproj_kernel(x_ref, cm_ref, cp_ref, cn_ref, cw_ref, yn_ref, of_ref, ob_ref, z_ref, gnw_ref, wo_ref, g1_ref,
                    nw_ref, sc_ref, sh_ref, wr_ref, br_ref, xo_ref, h_ref, e_ref, g_ref, hist_ref):
    o = of_ref[0] + ob_ref[0]
    z = z_ref[0]
    parts = [_convmix_tile(cm_ref, cp_ref, cn_ref, cw_ref), yn_ref[0]]
    for h in range(GDN_HEADS):
        sl = slice(h * GDN_DV, (h + 1) * GDN_DV)
        oh = o[:, sl]
        oh = oh * lax.rsqrt(jnp.mean(oh * oh, axis=-1, keepdims=True) + EPS) * gnw_ref[...]
        parts.append(oh * _silu(z[:, sl]))
    mixed = _mm(jnp.concatenate(parts, axis=-1).astype(BF16), wo_ref[0])
    xn = x_ref[0] + g1_ref[0] * mixed
    xo_ref[0] = xn
    hf = _modnorm(xn, nw_ref[...], sc_ref[0], sh_ref[0])
    h_ref[0] = hf
    hf_hi = hf.astype(BF16)
    hf_lo = (hf - hf_hi.astype(F32)).astype(BF16)
    logits = (_mm(hf_hi, wr_ref[0]) + (_mm(hf_hi, wr_ref[1]) + _mm(hf_lo, wr_ref[0]))) + br_ref[...]
    lane = lax.broadcasted_iota(jnp.int32, logits.shape, 1)
    gl = jnp.where(lane < N_GROUPS, logits, NEG)
    gm = jnp.max(gl, axis=-1, keepdims=True)
    den = jnp.sum(jnp.exp(gl - gm), axis=-1, keepdims=True)
    grp = jnp.min(jnp.where(gl == gm, lane, LANES), axis=-1, keepdims=True)
    pg_top = 1.0 / den
    ex = lane - N_GROUPS
    in_grp = (ex >= grp * EXPERTS_PER_GROUP) & (ex < (grp + 1) * EXPERTS_PER_GROUP)
    el = jnp.where(in_grp, logits, NEG)
    m1 = jnp.max(el, axis=-1, keepdims=True)
    i1 = jnp.min(jnp.where(el == m1, lane, LANES), axis=-1, keepdims=True)
    el2 = jnp.where(lane == i1, NEG, el)
    m2 = jnp.max(el2, axis=-1, keepdims=True)
    i2 = jnp.min(jnp.where(el2 == m2, lane, LANES), axis=-1, keepdims=True)
    e2 = jnp.exp(m2 - m1)
    w1 = pg_top / (1.0 + e2)
    w2 = pg_top * e2 / (1.0 + e2)
    g_ref[0] = jnp.where(lane == 0, w1, jnp.where(lane == 1, w2, 0.0))
    oh1 = (lane == i1 - N_GROUPS).astype(F32)
    oh2 = (lane == i2 - N_GROUPS).astype(F32)
    both = oh1 + oh2
    tm = logits.shape[0]
    earlier = (lax.broadcasted_iota(jnp.int32, (tm, tm), 0) > lax.broadcasted_iota(jnp.int32, (tm, tm), 1))
    cnt = _mm(earlier.astype(BF16), both.astype(BF16))
    r1 = jnp.sum(cnt * oh1, axis=-1, keepdims=True).astype(jnp.int32)
    r2 = jnp.sum(cnt * oh2, axis=-1, keepdims=True).astype(jnp.int32)
    e_ref[0] = jnp.where(lane == 0, i1 - N_GROUPS, jnp.where(lane == 1, i2 - N_GROUPS,
                         jnp.where(lane == 2, r1, jnp.where(lane == 3, r2, 0))))
    hist_ref[0, 0] = jnp.broadcast_to(jnp.sum(both, axis=0, keepdims=True), (SUBLANES, LANES))


def _outproj(x, conv_w, yn, of, ob, proj, gnw, wo, g1, nw, sc, sh, wr, br, layer):
    bsz, t, d = x.shape
    tm = min(ROW_TILE, t)
    tok = lambda w: pl.BlockSpec((1, tm, w), lambda b, i: (b, i, 0))
    perb = pl.BlockSpec((1, 1, d), lambda b, i: (b, 0, 0))
    full = lambda shape: pl.BlockSpec(shape, lambda b, i: (0, 0))
    return pl.pallas_call(
        _outproj_kernel,
        grid=(bsz, t // tm),
        in_specs=[tok(d)] + _halo_specs(tm, t, 3 * CONV_W, COL_CONV // (3 * CONV_W)) + [
                  full((3, CONV_W)), tok(NA_W), tok(GDN_W), tok(GDN_W),
                  pl.BlockSpec((1, tm, GDN_W), lambda b, i: (b, i, COL_GZ // GDN_W)),
                  full((1, GDN_DV)), pl.BlockSpec((1, d, d), lambda b, i: (layer, 0, 0)), perb, full((1, d)), perb, perb,
                  pl.BlockSpec((2, d, LANES), lambda b, i: (0, 0, 0)), full((1, LANES))],
        out_specs=[tok(d), tok(d), tok(LANES), tok(LANES),
                   pl.BlockSpec((1, 1, SUBLANES, LANES), lambda b, i: (b, i, 0, 0))],
        out_shape=[jax.ShapeDtypeStruct((bsz, t, d), F32), jax.ShapeDtypeStruct((bsz, t, d), F32),
                   jax.ShapeDtypeStruct((bsz, t, LANES), jnp.int32), jax.ShapeDtypeStruct((bsz, t, LANES), F32),
                   jax.ShapeDtypeStruct((bsz, t // tm, SUBLANES, LANES), F32)],
        compiler_params=_cparams(("parallel", "parallel")),
        name="outproj",
    )(x, proj, proj, proj, conv_w, yn, of, ob, proj, gnw, wo, g1, nw, sc, sh, wr, br)


MOE_TM = 256


PIECE = SUBLANES
SORT_ROWS = TOP_K * ROW_TILE + N_EXPERTS * PIECE
PIECE_UNITS = (4, 2, 1)
PIECE_SLOTS = (SORT_ROWS // (PIECE_UNITS[0] * PIECE), N_EXPERTS, N_EXPERTS)
PIECE_TABLE = 2 * sum(PIECE_SLOTS)


def _piece_loops(tbl_ref, cnt_ref, tile, fn):
    off = 0
    for c, (units, slots) in enumerate(zip(PIECE_UNITS, PIECE_SLOTS)):
        def body(g, carry, off=off, units=units, slots=slots):
            fn(pl.multiple_of(tbl_ref[0, 0, off + g] * PIECE, PIECE),
               pl.multiple_of(tbl_ref[0, 0, off + slots + g] * PIECE, PIECE), units * PIECE)
            return carry
        lax.fori_loop(0, cnt_ref[tile * len(PIECE_UNITS) + c], body, 0)
        off += 2 * slots


def _dispatch_kernel(lb_ref, nu_ref, np_ref, tbl_ref, tprev_ref, e_ref, a8_ref, h_ref, xs_out, zbuf, sbuf, sem, zsem):
    i = pl.program_id(0)
    nt = pl.num_programs(0)
    tm = h_ref.shape[0]
    bm = zbuf.shape[0]
    nblk = xs_out.shape[0] // bm
    slot = i % 2

    def zero_copy(blk):
        return pltpu.make_async_copy(zbuf, xs_out.at[pl.ds(pl.multiple_of(blk * bm, bm), bm), :], zsem)

    @pl.when(i == 0)
    def _():
        zbuf[...] = jnp.zeros_like(zbuf)
        for e in range(N_EXPERTS):
            @pl.when(lb_ref[e] >= 0)
            def _():
                zero_copy(lb_ref[e]).start()

        def start_trailing(j, carry):
            zero_copy(j).start()
            return carry

        def wait_one(j, carry):
            zero_copy(0).wait()
            return carry

        lax.fori_loop(nu_ref[0], nblk, start_trailing, 0)
        lax.fori_loop(nu_ref[0], nblk, wait_one, 0)
        for e in range(N_EXPERTS):
            @pl.when(lb_ref[e] >= 0)
            def _():
                zero_copy(0).wait()

    e = e_ref[...]
    lane = lax.broadcasted_iota(jnp.int32, e.shape, 1)
    a8 = a8_ref[0, 0:1, :]
    rows = []
    eye = lax.broadcasted_iota(jnp.int32, (tm, tm), 0) == lax.broadcasted_iota(jnp.int32, (tm, tm), 1)
    for k in range(TOP_K):
        col = (jnp.sum(jnp.where(lane == e[:, k:k + 1], a8, 0.0), axis=-1, keepdims=True)
               + e[:, TOP_K + k:TOP_K + k + 1].astype(F32))
        rows.append(jnp.sum(jnp.where(eye, col, 0.0), axis=0, keepdims=True))
    j = lax.broadcasted_iota(jnp.int32, (SORT_ROWS, tm), 0).astype(F32)
    perm = ((j == rows[0]) | (j == rows[1])).astype(BF16)
    hb = h_ref[...].astype(BF16)

    def piece(s, wait):
        def fn(src, dst, nrows):
            cp = pltpu.make_async_copy(sbuf.at[s, pl.ds(src, nrows), :], xs_out.at[pl.ds(dst, nrows), :], sem.at[s])
            cp.wait() if wait else cp.start()
        return fn

    sbuf[slot] = _mm(perm, hb)
    _piece_loops(tbl_ref, np_ref, i, piece(slot, False))

    @pl.when(i > 0)
    def _():
        _piece_loops(tprev_ref, np_ref, jnp.maximum(i - 1, 0), piece(1 - slot, True))

    @pl.when(i == nt - 1)
    def _():
        _piece_loops(tbl_ref, np_ref, i, piece(slot, True))


def _dispatch(hf, eidx, a8, ptab, npieces, last_blk, n_used, padded_rows):
    n, d = hf.shape
    tm = ROW_TILE
    nt = n // tm
    grid_spec = pltpu.PrefetchScalarGridSpec(
        num_scalar_prefetch=3,
        grid=(nt,),
        in_specs=[pl.BlockSpec((1, 1, PIECE_TABLE), lambda i, lb, nu, npc: (i, 0, 0), memory_space=pltpu.SMEM),
                  pl.BlockSpec((1, 1, PIECE_TABLE), lambda i, lb, nu, npc: (jnp.maximum(i - 1, 0), 0, 0),
                               memory_space=pltpu.SMEM),
                  pl.BlockSpec((tm, LANES), lambda i, lb, nu, npc: (i, 0)),
                  pl.BlockSpec((1, SUBLANES, LANES), lambda i, lb, nu, npc: (i, 0, 0)),
                  pl.BlockSpec((tm, d), lambda i, lb, nu, npc: (i, 0))],
        out_specs=pl.BlockSpec(memory_space=pl.ANY),
        scratch_shapes=[pltpu.VMEM((MOE_BM, d), F32), pltpu.VMEM((2, SORT_ROWS, d), F32),
                        pltpu.SemaphoreType.DMA((2,)), pltpu.SemaphoreType.DMA(())])
    return pl.pallas_call(
        _dispatch_kernel,
        grid_spec=grid_spec,
        out_shape=jax.ShapeDtypeStruct((padded_rows, d), F32),
        compiler_params=_cparams(("arbitrary",)),
        name="dispatch",
    )(last_blk, n_used, npieces, ptab, ptab, eidx, a8, hf)


def _experts_kernel(be_ref, nu_ref, nx_ref, sg_ref, x_ref, w1_hbm, w3_hbm, w2_hbm, o_ref,
                    w1f, w3f, w2f, w1b, w3b, w2b, sem, *, layer):
    i = pl.program_id(0)
    used = nu_ref[0]

    def fetch(e, s):
        return [pltpu.make_async_copy(w_hbm.at[layer, e], wf.at[s], sem.at[s])
                for w_hbm, wf in ((w1_hbm, w1f), (w3_hbm, w3f), (w2_hbm, w2f))]

    first = ((i == 0) | (be_ref[i] != be_ref[jnp.maximum(i - 1, 0)])) & (i < used)

    @pl.when(first)
    def _():
        s = sg_ref[i] % 2

        @pl.when(i == 0)
        def _():
            for cp in fetch(be_ref[0], 0):
                cp.start()

        for cp in fetch(be_ref[i], s):
            cp.wait()
        w1b[...] = w1f[s].astype(BF16)
        w3b[...] = w3f[s].astype(BF16)
        w2b[...] = w2f[s].astype(BF16)

        @pl.when(nx_ref[i] >= 0)
        def _():
            for cp in fetch(nx_ref[i], 1 - s):
                cp.start()

    @pl.when(i < used)
    def _():
        xb = x_ref[...].astype(BF16)
        act = (_silu(_mm(xb, w1b[...])) * _mm(xb, w3b[...])).astype(BF16)
        o_ref[...] = _mm(act, w2b[...])

    @pl.when(i >= used)
    def _():
        o_ref[...] = jnp.zeros_like(o_ref)


def _experts(xs, block_e, n_used, next_e, seg_id, w1, w3, w2, layer):
    padded_rows, d = xs.shape
    bm = MOE_BM
    nblk = padded_rows // bm
    de = w1.shape[-1]
    grid_spec = pltpu.PrefetchScalarGridSpec(
        num_scalar_prefetch=4,
        grid=(nblk,),
        in_specs=[pl.BlockSpec((bm, d), lambda i, be, nu, nx, sg: (jnp.maximum(jnp.minimum(i, nu[0] - 1), 0), 0)),
                  pl.BlockSpec(memory_space=pl.ANY), pl.BlockSpec(memory_space=pl.ANY),
                  pl.BlockSpec(memory_space=pl.ANY)],
        out_specs=pl.BlockSpec((bm, d), lambda i, be, nu, nx, sg: (i, 0)),
        scratch_shapes=[pltpu.VMEM((2, d, de), F32), pltpu.VMEM((2, d, de), F32), pltpu.VMEM((2, de, d), F32),
                        pltpu.VMEM((d, de), BF16), pltpu.VMEM((d, de), BF16), pltpu.VMEM((de, d), BF16),
                        pltpu.SemaphoreType.DMA((2,))])

    def experts_kernel(*refs):
        _experts_kernel(*refs, layer=layer)

    return pl.pallas_call(
        experts_kernel,
        grid_spec=grid_spec,
        out_shape=jax.ShapeDtypeStruct((padded_rows, d), F32),
        compiler_params=_cparams(("arbitrary",)),
        name="experts",
    )(block_e, n_used, next_e, seg_id, xs, w1, w3, w2)


def _gather_rows(idx_ref, ys_hbm, ybuf, sem, slot, lo, hi):
    def body(r, carry):
        pltpu.make_async_copy(ys_hbm.at[pl.ds(idx_ref[0, 0, r], 1), :], ybuf.at[slot, pl.ds(r, 1), :],
                              sem.at[slot]).start()
        return carry
    lax.fori_loop(lo, hi, body, 0, unroll=8)


def _combine_rows(pc_ref, x_ref, gt_ref, g2_ref, ys_hbm, ybuf, sem):
    i = pl.program_id(0)
    tm = x_ref.shape[0]
    slot = i % 2

    @pl.when(i == 0)
    def _():
        _gather_rows(pc_ref, ys_hbm, ybuf, sem, 0, 0, TOP_K * tm)

    pltpu.make_async_copy(ys_hbm.at[pl.ds(0, TOP_K * tm), :], ybuf.at[slot], sem.at[slot]).wait()
    gt = gt_ref[...]
    y = gt[:, 0:1] * ybuf[slot, 0:tm, :] + gt[:, 1:2] * ybuf[slot, tm:2 * tm, :]
    return x_ref[...] + g2_ref[0] * y


def _combine_final_kernel(np_ref, dc_ref, dn_ref, e_ref, a8_ref, x_ref, gt_ref, g2_ref, fw_ref, ys_hbm, o_ref,
                          ybuf, sem):
    i = pl.program_id(0)
    nt = pl.num_programs(0)
    tm = x_ref.shape[0]
    slot = i % 2

    def piece(s, wait):
        def fn(loc, glob, nrows):
            cp = pltpu.make_async_copy(ys_hbm.at[pl.ds(glob, nrows), :], ybuf.at[s, pl.ds(loc, nrows), :], sem.at[s])
            cp.wait() if wait else cp.start()
        return fn

    @pl.when(i == 0)
    def _():
        ybuf[...] = jnp.zeros_like(ybuf)
        _piece_loops(dc_ref, np_ref, 0, piece(0, False))

    @pl.when(i + 1 < nt)
    def _():
        _piece_loops(dn_ref, np_ref, jnp.minimum(i + 1, nt - 1), piece(1 - slot, False))

    _piece_loops(dc_ref, np_ref, i, piece(slot, True))
    e = e_ref[...]
    lane = lax.broadcasted_iota(jnp.int32, e.shape, 1)
    a8 = a8_ref[0, 0:1, :]
    yb = ybuf[slot].astype(BF16)
    j = lax.broadcasted_iota(jnp.int32, (tm, SORT_ROWS), 1).astype(F32)
    gt = gt_ref[...]
    y = jnp.zeros((tm, x_ref.shape[1]), F32)
    for k in range(TOP_K):
        row = (jnp.sum(jnp.where(lane == e[:, k:k + 1], a8, 0.0), axis=-1, keepdims=True)
               + e[:, TOP_K + k:TOP_K + k + 1].astype(F32))
        y = y + gt[:, k:k + 1] * _mm((j == row).astype(BF16), yb)
    xn = x_ref[...] + g2_ref[0] * y
    ms = jnp.mean(xn * xn, axis=-1, keepdims=True)
    o_ref[...] = xn * lax.rsqrt(ms + EPS) * fw_ref[...]


PROJ_CHUNK = 512


def _combine_inproj_kernel(pc_ref, pn_ref, x_ref, gt_ref, g2_ref, nw_ref, sc_ref, sh_ref, w_ref, wabt_ref, ys_hbm,
                           xo_ref, o_ref, ot_ref, ybuf, sem):
    i = pl.program_id(0)
    tm = x_ref.shape[0]
    xn = _combine_rows(pc_ref, x_ref, gt_ref, g2_ref, ys_hbm, ybuf, sem)
    xo_ref[...] = xn
    h = _modnorm(xn, nw_ref[...], sc_ref[0], sh_ref[0]).astype(BF16)
    ot_ref[0] = _nt(wabt_ref[0], h)
    ncol = o_ref.shape[2]
    starts = list(range(0, ncol, PROJ_CHUNK))
    per = TOP_K * tm // len(starts)
    nxt = 1 - i % 2
    for j, c0 in enumerate(starts):
        for r in range(j * per, TOP_K * tm if j == len(starts) - 1 else (j + 1) * per):
            pltpu.make_async_copy(ys_hbm.at[pl.ds(pn_ref[0, 0, r], 1), :], ybuf.at[nxt, pl.ds(r, 1), :],
                                  sem.at[nxt]).start()
        c1 = min(c0 + PROJ_CHUNK, ncol)
        o_ref[0, :, c0:c1] = _mm(h, w_ref[0, :, c0:c1])

    @pl.when(i + 1 == pl.num_programs(0))
    def _():
        pltpu.make_async_copy(ys_hbm.at[pl.ds(0, TOP_K * tm), :], ybuf.at[nxt], sem.at[nxt]).wait()


def _combine_specs(n, d, t):
    tm = MOE_TM
    nt = n // tm
    per_b = t // tm
    specs = [pl.BlockSpec((1, 1, TOP_K * tm), lambda i: (i, 0, 0), memory_space=pltpu.SMEM),
             pl.BlockSpec((1, 1, TOP_K * tm), lambda i: (jnp.minimum(i + 1, nt - 1), 0, 0), memory_space=pltpu.SMEM),
             pl.BlockSpec((tm, d), lambda i: (i, 0)),
             pl.BlockSpec((tm, LANES), lambda i: (i, 0)),
             pl.BlockSpec((1, 1, d), lambda i: (i // per_b, 0, 0))]
    scratch = [pltpu.VMEM((2, TOP_K * tm, d), F32), pltpu.SemaphoreType.DMA((2,))]
    return tm, nt, per_b, specs, scratch


def _combine_final(x2, eidx, gates, a8, ptab, npieces, g2, fw, ys, t):
    n, d = x2.shape
    tm = ROW_TILE
    nt = n // tm
    per_b = t // tm
    grid_spec = pltpu.PrefetchScalarGridSpec(
        num_scalar_prefetch=1,
        grid=(nt,),
        in_specs=[pl.BlockSpec((1, 1, PIECE_TABLE), lambda i, npc: (i, 0, 0), memory_space=pltpu.SMEM),
                  pl.BlockSpec((1, 1, PIECE_TABLE), lambda i, npc: (jnp.minimum(i + 1, nt - 1), 0, 0),
                               memory_space=pltpu.SMEM),
                  pl.BlockSpec((tm, LANES), lambda i, npc: (i, 0)),
                  pl.BlockSpec((1, SUBLANES, LANES), lambda i, npc: (i, 0, 0)),
                  pl.BlockSpec((tm, d), lambda i, npc: (i, 0)),
                  pl.BlockSpec((tm, LANES), lambda i, npc: (i, 0)),
                  pl.BlockSpec((1, 1, d), lambda i, npc: (i // per_b, 0, 0)),
                  pl.BlockSpec((1, d), lambda i, npc: (0, 0)),
                  pl.BlockSpec(memory_space=pl.ANY)],
        out_specs=pl.BlockSpec((tm, d), lambda i, npc: (i, 0)),
        scratch_shapes=[pltpu.VMEM((2, SORT_ROWS, d), F32), pltpu.SemaphoreType.DMA((2,))])
    return pl.pallas_call(
        _combine_final_kernel,
        grid_spec=grid_spec,
        out_shape=jax.ShapeDtypeStruct((n, d), F32),
        compiler_params=_cparams(("arbitrary",)),
        name="combine",
    )(npieces, ptab, ptab, eidx, a8, x2, gates, g2, fw, ys)


def _combine_inproj(x2, gates, g2, ys, pos3, t, nw, sc, sh, w_pad, wabt, layer):
    n, d = x2.shape
    bsz = n // t
    nab = wabt.shape[1]
    tm, nt, per_b, specs, scratch = _combine_specs(n, d, t)
    perb = pl.BlockSpec((1, 1, d), lambda i: (i // per_b, 0, 0))
    return pl.pallas_call(
        _combine_inproj_kernel,
        grid=(nt,),
        in_specs=specs + [pl.BlockSpec((1, d), lambda i: (0, 0)), perb, perb,
                          pl.BlockSpec((1, d, D_IN_PAD), lambda i: (layer, 0, 0)),
                          pl.BlockSpec((1, nab, d), lambda i: (layer, 0, 0)),
                          pl.BlockSpec(memory_space=pl.ANY)],
        out_specs=[pl.BlockSpec((tm, d), lambda i: (i, 0)),
                   pl.BlockSpec((1, tm, D_IN_PAD), lambda i: (i // per_b, i % per_b, 0)),
                   pl.BlockSpec((1, nab, tm), lambda i: (i // per_b, 0, i % per_b))],
        out_shape=[jax.ShapeDtypeStruct((n, d), F32), jax.ShapeDtypeStruct((bsz, t, D_IN_PAD), F32),
                   jax.ShapeDtypeStruct((bsz, nab, t), F32)],
        scratch_shapes=scratch,
        compiler_params=_cparams(("arbitrary",)),
        name="combine_inproj",
    )(pos3, pos3, x2, gates, g2, nw, sc, sh, w_pad, wabt, ys)


def _moe_plan(eidx, hist):
    n = eidx.shape[0]
    bm = MOE_BM
    ntile = n // ROW_TILE
    experts = jnp.arange(N_EXPERTS, dtype=jnp.int32)
    hist = hist[:, 0, :N_EXPERTS].astype(jnp.int32)
    cnt = (hist + PIECE - 1) // PIECE * PIECE
    seg_start = jnp.cumsum(cnt, axis=1) - cnt
    sizes = jnp.sum(cnt, axis=0)
    base = jnp.cumsum(cnt, axis=0) - cnt
    padded = (sizes + bm - 1) // bm * bm
    pad_end = jnp.cumsum(padded)
    tbl = (pad_end - padded)[None, :] + base
    e = eidx[:, :TOP_K].reshape(ntile, ROW_TILE, TOP_K)
    onehot = e[..., None] == experts
    pos = jnp.sum(jnp.where(onehot, tbl[:, None, None, :], 0), axis=-1).reshape(n, TOP_K) + eidx[:, TOP_K:2 * TOP_K]
    left, tables, counts = cnt // PIECE, [], []
    done = jnp.zeros_like(left)
    for units, slots in zip(PIECE_UNITS, PIECE_SLOTS):
        num = left // units
        end = jnp.cumsum(num, axis=1)
        g = jnp.arange(slots, dtype=jnp.int32)
        e_of_g = jnp.minimum(jnp.sum(end[:, None, :] <= g[None, :, None], axis=-1), N_EXPERTS - 1)
        sel = e_of_g[..., None] == experts
        pick = lambda a, sel=sel: jnp.sum(jnp.where(sel, a[:, None, :], 0), axis=-1)
        step = units * (g[None, :] - pick(end - num))
        valid = g[None, :] < end[:, -1:]
        tables += [jnp.where(valid, pick(seg_start // PIECE + done) + step, 0),
                   jnp.where(valid, pick(tbl // PIECE + done) + step, 0)]
        counts.append(end[:, -1])
        done = done + num * units
        left = left - num * units
    ptab = jnp.concatenate(tables, axis=1).astype(jnp.int32).reshape(ntile, 1, PIECE_TABLE)
    npieces = jnp.stack(counts, axis=1).astype(jnp.int32).reshape(-1)
    seg8 = jnp.broadcast_to(jnp.pad(seg_start.astype(F32), ((0, 0), (0, LANES - N_EXPERTS)))[:, None, :],
                            (ntile, SUBLANES, LANES))
    padded_rows = (n * TOP_K + ntile * N_EXPERTS * (PIECE - 1) + N_EXPERTS * (bm - 1) + bm - 1) // bm * bm
    n_blocks = padded_rows // bm
    blk_start = jnp.arange(n_blocks, dtype=jnp.int32) * bm
    block_e = jnp.minimum(jnp.sum(pad_end[None, :] <= blk_start[:, None], axis=1), N_EXPERTS - 1).astype(jnp.int32)
    n_used = (pad_end[-1] // bm).astype(jnp.int32).reshape(1)
    last_blk = jnp.where(padded > 0, pad_end // bm - 1, -1).astype(jnp.int32)
    blk = jnp.arange(n_blocks, dtype=jnp.int32)
    change = (blk == 0) | (block_e != jnp.roll(block_e, 1))
    seg_id = (jnp.cumsum(change.astype(jnp.int32)) - 1).astype(jnp.int32)
    later = (blk[None, :] > blk[:, None]) & (block_e[None, :] != block_e[:, None]) & (blk[None, :] < n_used[0])
    nxt = jnp.min(jnp.where(later, blk[None, :], n_blocks), axis=1)
    next_e = jnp.where(nxt < n_blocks, block_e[jnp.minimum(nxt, n_blocks - 1)], -1).astype(jnp.int32)
    nt = n // MOE_TM
    pos3 = jnp.transpose(pos.astype(jnp.int32).reshape(nt, MOE_TM, TOP_K), (0, 2, 1)).reshape(nt, 1, TOP_K * MOE_TM)
    return pos3, seg8, ptab, npieces, block_e, n_used, next_e, seg_id, last_blk, padded_rows


def kernel(x, c, norm_mix_w, norm_ffn_w, w_ada, b_ada, w_in, conv_a_w, na_rpb, gdn_conv_w, gdn_a_log, gdn_dt_bias,
           gdn_norm_w, w_out, router_group_w, router_group_b, router_expert_w, router_expert_b, expert_w1,
           expert_w3, expert_w2, final_norm_w):
    bsz, t, d = x.shape
    depth = w_ada.shape[0]
    n = bsz * t
    assert t % ROW_TILE == 0 and (t // GRID_W) % NA_KH == 0, "sequence length must be a multiple of 512"
    mod = _ada(c, w_ada, b_ada)
    w_in_pad = jnp.pad(w_in, ((0, 0), (0, 0), (0, D_IN_PAD - D_IN))).astype(BF16)
    wabt = jnp.transpose(w_in[:, :, COL_AB:], (0, 2, 1)).astype(BF16)
    w_out_b = w_out.astype(BF16)
    wr = jnp.pad(jnp.concatenate([router_group_w, router_expert_w], axis=-1),
                 ((0, 0), (0, 0), (0, LANES - N_GROUPS - N_EXPERTS)))
    wr_hi = wr.astype(BF16)
    wr = jnp.stack([wr_hi, (wr - wr_hi.astype(F32)).astype(BF16)], axis=1)
    br = jnp.pad(jnp.concatenate([router_group_b, router_expert_b], axis=-1),
                 ((0, 0), (0, LANES - N_GROUPS - N_EXPERTS)))
    fw = final_norm_w.reshape(1, d)
    na_bias = _na_bias_table(na_rpb)
    mods = [[mod[l, :, j * d:(j + 1) * d].reshape(bsz, 1, d) for j in range(6)] for l in range(depth)]
    proj, abt = _inproj(x, norm_mix_w[0].reshape(1, d), mods[0][1], mods[0][0], w_in_pad, wabt, 0)
    for l in range(depth):
        sh1, sc1, g1, sh2, sc2, g2 = mods[l]
        y_na = _na(proj, na_bias, l)
        o_f, o_b = _gdnscan(*_gdnchunk(proj, abt, gdn_conv_w[l], gdn_a_log[l], gdn_dt_bias[l]))
        x, hf, eidx, gates, hist = _outproj(x, conv_a_w[l], y_na, o_f, o_b, proj, gdn_norm_w[l].reshape(1, GDN_DV),
                                            w_out_b, g1, norm_ffn_w[l].reshape(1, d), sc2, sh2, wr[l],
                                            br[l].reshape(1, LANES), l)
        eidx = eidx.reshape(n, LANES)
        pos3, seg8, ptab, npieces, block_e, n_used, next_e, seg_id, last_blk, padded_rows = _moe_plan(
            eidx, hist.reshape(-1, SUBLANES, LANES))
        xs = _dispatch(hf.reshape(n, d), eidx, seg8, ptab, npieces, last_blk, n_used, padded_rows)
        ys = _experts(xs, block_e, n_used, next_e, seg_id, expert_w1, expert_w3, expert_w2, l)
        x2, gt = x.reshape(n, d), gates.reshape(n, LANES)
        if l + 1 < depth:
            x2, proj, abt = _combine_inproj(x2, gt, g2, ys, pos3, t, norm_mix_w[l + 1].reshape(1, d), mods[l + 1][1],
                                            mods[l + 1][0], w_in_pad, wabt, l + 1)
            x = x2.reshape(bsz, t, d)
        else:
            x = _combine_final(x2, eidx, gt, seg8, ptab, npieces, g2, fw, ys, t).reshape(bsz, t, d)
    return x
```
